```python
import math
import jax, jax.numpy as jnp
from jax import lax
import numpy as np

D_MODEL = 2048
BATCH = 1
SEQ = 8192
DEPTH = 1
DEC_BATCH = 128
DEC_SEQ = 4
PAST_LEN = 2048
PAGE_SIZE = 128

MIX_WIDTH = D_MODEL
ATTN_WIDTH = MIX_WIDTH // 2
POOL_WIDTH = MIX_WIDTH - ATTN_WIDTH
HEAD_DIM = 128
N_HEADS = ATTN_WIDTH // HEAD_DIM
POOL_WINDOWS = (2, 4, 8, 16)
N_POOL_GROUPS = len(POOL_WINDOWS)
POOL_GROUP_W = POOL_WIDTH // N_POOL_GROUPS
POOL_STATE = max(POOL_WINDOWS) - 1
MOBA_BLOCK = 256
MOBA_TOPK = 3
Q_BLOCK = 128
N_EXPERT_GROUPS = 4
EXPERTS_PER_GROUP = 4
N_EXPERTS = N_EXPERT_GROUPS * EXPERTS_PER_GROUP
EXPERT_TOPK = 2
D_EXPERT = 3 * D_MODEL // 8
EPS = 1e-6
NEG = -1e30

kernel_name = "hymba_moba_pool_hmoe_step"


def rmsnorm(x, g):
    xf = x.astype(jnp.float32)
    y = xf * lax.rsqrt(jnp.mean(xf * xf, axis=-1, keepdims=True) + EPS)
    return (y * g.astype(jnp.float32)).astype(x.dtype)


def to_bh(t):
    return jnp.transpose(t, (0, 2, 1, 3))


def pad_blocks(t):
    L = t.shape[2]
    Lp = -(-L // MOBA_BLOCK) * MOBA_BLOCK
    return jnp.pad(t, ((0, 0), (0, 0), (0, Lp - L), (0, 0)))


def block_means(k):
    B, H, Lp, D = k.shape
    return jnp.mean(k.reshape(B, H, Lp // MOBA_BLOCK, MOBA_BLOCK, D).astype(jnp.float32), axis=3)


def mixer_inputs(x, g_mix, w_in, g_q, g_k):
    B, S, _ = x.shape
    h = rmsnorm(x, g_mix)
    z = jnp.einsum('bsd,de->bse', h, w_in)
    q = z[..., :ATTN_WIDTH].reshape(B, S, N_HEADS, HEAD_DIM)
    k = z[..., ATTN_WIDTH:2 * ATTN_WIDTH].reshape(B, S, N_HEADS, HEAD_DIM)
    v = z[..., 2 * ATTN_WIDTH:3 * ATTN_WIDTH].reshape(B, S, N_HEADS, HEAD_DIM)
    u = z[..., 3 * ATTN_WIDTH:]
    return rmsnorm(q, g_q), rmsnorm(k, g_k), v, u


def moba_query_block(q, q0, k, v, kmean):
    B, H, T, D = q.shape
    nb = kmean.shape[2]
    topk = min(MOBA_TOPK, nb)
    own = q0 // MOBA_BLOCK
    qf = q.astype(jnp.float32)
    gate = jnp.einsum('bhtd,bhnd->bhtn', qf, kmean)
    gate = jnp.where(jnp.arange(nb) < own, gate, NEG)
    _, sel = lax.top_k(gate, topk)
    sel_ok = sel < own
    bi = jnp.arange(B)[:, None, None, None]
    hi = jnp.arange(H)[None, :, None, None]
    kb = k.reshape(B, H, nb, MOBA_BLOCK, D)
    vb = v.reshape(B, H, nb, MOBA_BLOCK, D)
    k_sel = kb[bi, hi, sel].astype(jnp.float32)
    v_sel = vb[bi, hi, sel].astype(jnp.float32)
    k_own = lax.dynamic_slice_in_dim(k, own * MOBA_BLOCK, MOBA_BLOCK, axis=2).astype(jnp.float32)
    v_own = lax.dynamic_slice_in_dim(v, own * MOBA_BLOCK, MOBA_BLOCK, axis=2).astype(jnp.float32)
    scale = HEAD_DIM ** -0.5
    s_sel = jnp.einsum('bhtd,bhtnkd->bhtnk', qf, k_sel) * scale
    s_sel = jnp.where(sel_ok[..., None], s_sel, NEG)
    s_own = jnp.einsum('bhtd,bhkd->bhtk', qf, k_own) * scale
    kpos = own * MOBA_BLOCK + jnp.arange(MOBA_BLOCK)
    qpos = q0 + jnp.arange(T)
    s_own = jnp.where(kpos[None, :] <= qpos[:, None], s_own, NEG)
    s = jnp.concatenate([s_sel.reshape(B, H, T, topk * MOBA_BLOCK), s_own], axis=-1)
    p = jax.nn.softmax(s, axis=-1)
    p_sel = p[..., :topk * MOBA_BLOCK].reshape(B, H, T, topk, MOBA_BLOCK)
    p_own = p[..., topk * MOBA_BLOCK:]
    o = jnp.einsum('bhtnk,bhtnkd->bhtd', p_sel, v_sel) + jnp.einsum('bhtk,bhkd->bhtd', p_own, v_own)
    return o.astype(q.dtype)


def moba_prompt(q, k, v):
    B, H, S, D = q.shape
    kp = pad_blocks(k)
    vp = pad_blocks(v)
    kmean = block_means(kp)

    def one(c):
        q0 = c * Q_BLOCK
        qc = lax.dynamic_slice_in_dim(q, q0, Q_BLOCK, axis=2)
        return moba_query_block(qc, q0, kp, vp, kmean)

    o = lax.map(one, jnp.arange(S // Q_BLOCK))
    return jnp.transpose(o, (1, 2, 0, 3, 4)).reshape(B, H, S, D)


def pool_mix(u, prev, start_pos, w_pool, pool_scale):
    B, S, P = u.shape
    ext = jnp.concatenate([prev.astype(u.dtype), u], axis=1).astype(jnp.float32)
    cs = jnp.concatenate([jnp.zeros((B, 1, P), jnp.float32), jnp.cumsum(ext, axis=1)], axis=1)
    pos = (start_pos + jnp.arange(S)).astype(jnp.float32)
    uf = u.astype(jnp.float32)
    end0 = POOL_STATE + 1
    outs = []
    for g, w in enumerate(POOL_WINDOWS):
        c0, c1 = g * POOL_GROUP_W, (g + 1) * POOL_GROUP_W
        wsum = cs[:, end0:end0 + S, c0:c1] - cs[:, end0 - w:end0 - w + S, c0:c1]
        count = jnp.minimum(jnp.float32(w), pos + 1.0)[None, :, None]
        d = wsum / count - uf[..., c0:c1]
        outs.append(jnp.einsum('bsc,ce->bse', d, w_pool[g].astype(jnp.float32)))
    y = jnp.concatenate(outs, axis=-1) * pool_scale.astype(jnp.float32)
    return y.astype(u.dtype), ext[:, -POOL_STATE:].astype(u.dtype)


def mixer_output(x, o_attn, o_pool, w_out):
    B, S, _ = x.shape
    mix = jnp.concatenate([o_attn.reshape(B, S, ATTN_WIDTH).astype(x.dtype), o_pool.astype(x.dtype)], axis=-1)
    return x + jnp.einsum('bse,ed->bsd', mix, w_out)


def hier_moe(h, w_gr, b_gr, w_er, b_er, w_gate, w_up, w_down):
    N = h.shape[0]
    hf = h.astype(jnp.float32)
    g_logits = hf @ w_gr.astype(jnp.float32) + b_gr.astype(jnp.float32)
    g_prob = jax.nn.softmax(g_logits, axis=-1)
    g_top = jnp.argmax(g_logits, axis=-1)
    g_p = jnp.take_along_axis(g_prob, g_top[:, None], axis=-1)
    e_logits = jnp.einsum('nd,dge->nge', hf, w_er.astype(jnp.float32)) + b_er.astype(jnp.float32)
    e_logits = jnp.take_along_axis(e_logits, g_top[:, None, None], axis=1)[:, 0]
    e_val, e_idx = lax.top_k(e_logits, EXPERT_TOPK)
    e_w = jax.nn.softmax(e_val, axis=-1) * g_p
    expert_id = g_top[:, None] * EXPERTS_PER_GROUP + e_idx
    combine = jnp.sum(jax.nn.one_hot(expert_id, N_EXPERTS, dtype=jnp.float32) * e_w[..., None], axis=1)
    a = jnp.einsum('nd,edf->nef', h, w_gate)
    b = jnp.einsum('nd,edf->nef', h, w_up)
    act = jax.nn.silu(a) * b * combine[..., None].astype(h.dtype)
    return jnp.einsum('nef,efd->nd', act, w_down)


def ffn_block(x, g_ffn, w_gr, b_gr, w_er, b_er, w_gate, w_up, w_down):
    h = rmsnorm(x, g_ffn).reshape(-1, x.shape[-1])
    return x + hier_moe(h, w_gr, b_gr, w_er, b_er, w_gate, w_up, w_down).reshape(x.shape).astype(x.dtype)


def setup_inputs(seed: int = 0) -> dict:
    key = jax.random.key(seed)
    ks = jax.random.split(key, 24)
    f32 = jnp.float32
    n_pages = PAST_LEN // PAGE_SIZE
    n_phys = (5 * DEC_BATCH * n_pages + 3) // 4
    L = DEPTH
    mix_in = 3 * ATTN_WIDTH + POOL_WIDTH

    def nrm(k, shape, scale=1.0):
        return jax.random.normal(k, shape, f32) * scale

    def gain(k, shape):
        return 1.0 + 0.02 * jax.random.normal(k, shape, f32)

    perm = jax.random.permutation(ks[5], n_phys)[:DEC_BATCH * n_pages]
    return {
        "x_prompt": nrm(ks[0], (BATCH, SEQ, D_MODEL)),
        "x_sample": nrm(ks[1], (DEC_BATCH, DEC_SEQ, D_MODEL)),
        "cache_k": nrm(ks[2], (L, n_phys, PAGE_SIZE, N_HEADS, HEAD_DIM)),
        "cache_v": nrm(ks[3], (L, n_phys, PAGE_SIZE, N_HEADS, HEAD_DIM)),
        "state_pool": nrm(ks[4], (L, DEC_BATCH, POOL_STATE, POOL_WIDTH)),
        "page_table": perm.reshape(DEC_BATCH, n_pages).astype(jnp.int32),
        "g_mix": gain(ks[6], (L, D_MODEL)),
        "w_in": nrm(ks[7], (L, D_MODEL, mix_in), D_MODEL ** -0.5),
        "g_q": gain(ks[8], (L, HEAD_DIM)),
        "g_k": gain(ks[9], (L, HEAD_DIM)),
        "w_pool": nrm(ks[10], (L, N_POOL_GROUPS, POOL_GROUP_W, POOL_GROUP_W), POOL_GROUP_W ** -0.5),
        "pool_scale": 1.0 + 0.1 * nrm(ks[11], (L, POOL_WIDTH)),
        "w_out": nrm(ks[12], (L, MIX_WIDTH, D_MODEL), MIX_WIDTH ** -0.5),
        "g_ffn": gain(ks[13], (L, D_MODEL)),
        "w_group_router": nrm(ks[14], (L, D_MODEL, N_EXPERT_GROUPS), D_MODEL ** -0.5),
        "b_group_router": nrm(ks[15], (L, N_EXPERT_GROUPS), 0.01),
        "w_expert_router": nrm(ks[16], (L, D_MODEL, N_EXPERT_GROUPS, EXPERTS_PER_GROUP), D_MODEL ** -0.5),
        "b_expert_router": nrm(ks[17], (L, N_EXPERT_GROUPS, EXPERTS_PER_GROUP), 0.01),
        "w_gate": nrm(ks[18], (L, N_EXPERTS, D_MODEL, D_EXPERT), D_MODEL ** -0.5),
        "w_up": nrm(ks[19], (L, N_EXPERTS, D_MODEL, D_EXPERT), D_MODEL ** -0.5),
        "w_down": nrm(ks[20], (L, N_EXPERTS, D_EXPERT, D_MODEL), D_EXPERT ** -0.5),
    }


def reference(x_prompt, x_sample, cache_k, cache_v, state_pool, page_table, g_mix, w_in, g_q, g_k,
              w_pool, pool_scale, w_out, g_ffn, w_group_router, b_group_router, w_expert_router,
              b_expert_router, w_gate, w_up, w_down):
    B, S, _ = x_prompt.shape
    DB, T, _ = x_sample.shape
    n_pages = page_table.shape[1]
    past_len = n_pages * cache_k.shape[2]
    h_p, h_s = x_prompt, x_sample
    kp_l, vp_l, pp_l, ks_l, vs_l, ps_l = [], [], [], [], [], []
    for l in range(DEPTH):
        q, k, v, u = mixer_inputs(h_p, g_mix[l], w_in[l], g_q[l], g_k[l])
        o_attn = jnp.transpose(moba_prompt(to_bh(q), to_bh(k), to_bh(v)), (0, 2, 1, 3))
        o_pool, pool_new = pool_mix(u, jnp.zeros((B, POOL_STATE, POOL_WIDTH), u.dtype), 0, w_pool[l], pool_scale[l])
        h_p = mixer_output(h_p, o_attn, o_pool, w_out[l])
        h_p = ffn_block(h_p, g_ffn[l], w_group_router[l], b_group_router[l], w_expert_router[l],
                        b_expert_router[l], w_gate[l], w_up[l], w_down[l])
        kp_l.append(k); vp_l.append(v); pp_l.append(pool_new)
        q, k, v, u = mixer_inputs(h_s, g_mix[l], w_in[l], g_q[l], g_k[l])
        k_past = cache_k[l][page_table].reshape(DB, past_len, N_HEADS, HEAD_DIM)
        v_past = cache_v[l][page_table].reshape(DB, past_len, N_HEADS, HEAD_DIM)
        k_all = pad_blocks(to_bh(jnp.concatenate([k_past, k.astype(k_past.dtype)], axis=1)))
        v_all = pad_blocks(to_bh(jnp.concatenate([v_past, v.astype(v_past.dtype)], axis=1)))
        o_attn = moba_query_block(to_bh(q), past_len, k_all, v_all, block_means(k_all))
        o_attn = jnp.transpose(o_attn, (0, 2, 1, 3))
        o_pool, pool_new = pool_mix(u, state_pool[l], past_len, w_pool[l], pool_scale[l])
        h_s = mixer_output(h_s, o_attn, o_pool, w_out[l])
        h_s = ffn_block(h_s, g_ffn[l], w_group_router[l], b_group_router[l], w_expert_router[l],
                        b_expert_router[l], w_gate[l], w_up[l], w_down[l])
        ks_l.append(k); vs_l.append(v); ps_l.append(pool_new)
    y_prompt = h_p
    y_sample = h_s
    k_prompt = jnp.stack(kp_l)
    v_prompt = jnp.stack(vp_l)
    pool_prompt = jnp.stack(pp_l)
    k_sample = jnp.stack(ks_l)
    v_sample = jnp.stack(vs_l)
    pool_sample = jnp.stack(ps_l)
    return (y_prompt, y_sample, k_prompt, v_prompt, pool_prompt, k_sample, v_sample, pool_sample)
```

```python
import functools

import jax
import jax.numpy as jnp
from jax import lax
from jax.experimental import pallas as pl
from jax.experimental.pallas import tpu as pltpu

D_MODEL = 2048
ATTN_WIDTH = 1024
POOL_WIDTH = 1024
HEAD_DIM = 128
N_HEADS = 8
POOL_WINDOWS = (2, 4, 8, 16)
POOL_GROUP_W = 256
POOL_STATE = 15
MOBA_BLOCK = 256
MOBA_TOPK = 3
N_EXPERT_GROUPS = 4
EXPERTS_PER_GROUP = 4
N_EXPERTS = 16
D_EXPERT = 768
EPS = 1e-6
NEG = -1e30
LANES = 128
VMEM_LIMIT = 56 * 1024 * 1024

BF16 = jnp.bfloat16
F32 = jnp.float32


def _cparams(sem):
    return pltpu.CompilerParams(dimension_semantics=sem, vmem_limit_bytes=VMEM_LIMIT)


def _dot(a, b):
    return jnp.dot(a, b, preferred_element_type=F32)


def _dot_nt(a, b, precision=None):
    return lax.dot_general(a, b, (((1,), (1,)), ((), ())), precision=precision,
                           preferred_element_type=F32)


def _top_mask(g, col, k):
    sel = jnp.zeros(g.shape, F32)
    for _ in range(k):
        m = jnp.max(g, axis=-1, keepdims=True)
        idx = jnp.min(jnp.where(g == m, col, jnp.float32(g.shape[-1])), axis=-1, keepdims=True)
        pick = col == idx
        sel = jnp.where(pick, 1.0, sel)
        g = jnp.where(pick, -jnp.inf, g)
    return sel


def _in_proj_kernel(x_ref, gmix_ref, w_ref, gq_ref, gk_ref,
                    q_ref, k_ref, kb_ref, km_ref, v_ref, vb_ref, u_ref, hb_ref, *, tm):
    j = pl.program_id(1)

    @pl.when(j == 0)
    def _():
        xf = x_ref[...]
        ms = jnp.mean(xf * xf, axis=-1, keepdims=True)
        hb_ref[...] = (xf * lax.rsqrt(ms + EPS) * gmix_ref[...]).astype(BF16)

    z = _dot(hb_ref[...], w_ref[...])

    def head_norm(out_ref, g_ref):
        for h in range(N_HEADS):
            zh = z[:, h * HEAD_DIM:(h + 1) * HEAD_DIM]
            ms = jnp.mean(zh * zh, axis=-1, keepdims=True)
            out_ref[:, h * HEAD_DIM:(h + 1) * HEAD_DIM] = zh * lax.rsqrt(ms + EPS) * g_ref[...]

    @pl.when(j == 0)
    def _():
        head_norm(q_ref, gq_ref)

    @pl.when(j == 1)
    def _():
        head_norm(k_ref, gk_ref)
        kb_ref[...] = k_ref[...].astype(BF16)
        for g in range(tm // MOBA_BLOCK):
            blk = k_ref[g * MOBA_BLOCK:(g + 1) * MOBA_BLOCK, :]
            km_ref[g] = jnp.mean(blk, axis=0, keepdims=True)

    @pl.when(j == 2)
    def _():
        v_ref[...] = z
        vb_ref[...] = z.astype(BF16)

    @pl.when(j == 3)
    def _():
        u_ref[...] = z


def in_proj(x, g_mix, w_in_bf, g_q, g_k, *, tm):
    n = x.shape[0]
    wide = lambda i, j: (i, 0)
    const = lambda i, j: (0, 0)
    out_shape = (
        jax.ShapeDtypeStruct((n, ATTN_WIDTH), F32),
        jax.ShapeDtypeStruct((n, ATTN_WIDTH), F32),
        jax.ShapeDtypeStruct((n, ATTN_WIDTH), BF16),
        jax.ShapeDtypeStruct((n // MOBA_BLOCK, 1, ATTN_WIDTH), F32),
        jax.ShapeDtypeStruct((n, ATTN_WIDTH), F32),
        jax.ShapeDtypeStruct((n, ATTN_WIDTH), BF16),
        jax.ShapeDtypeStruct((n, POOL_WIDTH), F32),
    )
    blk = pl.BlockSpec((tm, ATTN_WIDTH), wide)
    return pl.pallas_call(
        functools.partial(_in_proj_kernel, tm=tm),
        grid=(n // tm, 4),
        in_specs=[
            pl.BlockSpec((tm, D_MODEL), wide),
            pl.BlockSpec((1, D_MODEL), const),
            pl.BlockSpec((D_MODEL, ATTN_WIDTH), lambda i, j: (0, j)),
            pl.BlockSpec((1, HEAD_DIM), const),
            pl.BlockSpec((1, HEAD_DIM), const),
        ],
        out_specs=(blk, blk, blk,
                   pl.BlockSpec((tm // MOBA_BLOCK, 1, ATTN_WIDTH), lambda i, j: (i, 0, 0)),
                   blk, blk, blk),
        out_shape=out_shape,
        scratch_shapes=[pltpu.VMEM((tm, D_MODEL), BF16)],
        compiler_params=_cparams(("arbitrary", "arbitrary")),
        name="in_proj",
    )(x, g_mix, w_in_bf, g_q, g_k)


def _moba_prompt_kernel(q_ref, k_ref, v_ref, km_ref, o_ref, m_ref, l_ref, acc_ref, *, nb):
    own = pl.program_id(1)
    q = q_ref[...]
    qs = (q * (HEAD_DIM ** -0.5)).astype(BF16)

    gate = _dot_nt(q, km_ref[...], precision=lax.Precision.HIGHEST)
    col = lax.broadcasted_iota(jnp.int32, gate.shape, 1).astype(F32)
    valid = col < own.astype(F32)
    sel = _top_mask(jnp.where(valid, gate, NEG), col, MOBA_TOPK)
    sel_bias = jnp.where(jnp.logical_and(sel > 0.5, valid), 0.0, NEG).astype(BF16)

    base = pl.multiple_of(own * MOBA_BLOCK, MOBA_BLOCK)
    s = _dot_nt(qs, k_ref[pl.ds(base, MOBA_BLOCK), :])
    r_i = lax.broadcasted_iota(jnp.int32, s.shape, 0)
    c_i = lax.broadcasted_iota(jnp.int32, s.shape, 1)
    s = jnp.where(c_i <= r_i, s, NEG)
    m0 = jnp.max(s, axis=-1, keepdims=True)
    p = jnp.exp(s - m0)
    m_ref[...] = jnp.broadcast_to(m0, m_ref.shape)
    l_ref[...] = jnp.broadcast_to(jnp.sum(p, axis=-1, keepdims=True), l_ref.shape)
    acc_ref[...] = _dot(p.astype(BF16), v_ref[pl.ds(base, MOBA_BLOCK), :])

    blk_row = lax.broadcasted_iota(jnp.int32, (nb, MOBA_BLOCK), 0)

    def body(n, carry):
        off = pl.multiple_of(n * MOBA_BLOCK, MOBA_BLOCK)
        expand = jnp.where(blk_row == n, 1.0, 0.0).astype(BF16)
        s = _dot_nt(qs, k_ref[pl.ds(off, MOBA_BLOCK), :]) + _dot(sel_bias, expand)
        m_prev = m_ref[...]
        m_new = jnp.maximum(m_prev, jnp.max(s, axis=-1, keepdims=True))
        alpha = jnp.exp(m_prev - m_new)
        p = jnp.exp(s - m_new[:, :1])
        l_ref[...] = alpha * l_ref[...] + jnp.sum(p, axis=-1, keepdims=True)
        acc_ref[...] = alpha * acc_ref[...] + _dot(p.astype(BF16), v_ref[pl.ds(off, MOBA_BLOCK), :])
        m_ref[...] = m_new
        return carry

    lax.fori_loop(0, own, body, 0)
    o_ref[...] = (acc_ref[...] * (1.0 / l_ref[...])).astype(o_ref.dtype)


def moba_prompt(q, k_bf, v_bf, kmean):
    s = q.shape[0]
    nb = s // MOBA_BLOCK
    return pl.pallas_call(
        functools.partial(_moba_prompt_kernel, nb=nb),
        grid=(N_HEADS, nb),
        in_specs=[
            pl.BlockSpec((MOBA_BLOCK, HEAD_DIM), lambda h, j: (j, h)),
            pl.BlockSpec((s, HEAD_DIM), lambda h, j: (0, h)),
            pl.BlockSpec((s, HEAD_DIM), lambda h, j: (0, h)),
            pl.BlockSpec((nb, HEAD_DIM), lambda h, j: (0, h)),
        ],
        out_specs=pl.BlockSpec((MOBA_BLOCK, HEAD_DIM), lambda h, j: (j, h)),
        out_shape=jax.ShapeDtypeStruct((s, ATTN_WIDTH), BF16),
        scratch_shapes=[pltpu.VMEM((MOBA_BLOCK, HEAD_DIM), F32)] * 3,
        compiler_params=_cparams(("arbitrary", "arbitrary")),
        name="moba_prompt",
    )(q, k_bf, v_bf, kmean)


def _moba_sample_kernel(pt_ref, q_ref, kn_ref, vn_ref, *refs, n_pages, t_new):
    k_pages = refs[:n_pages]
    v_pages = refs[n_pages:2 * n_pages]
    o_ref = refs[2 * n_pages]
    s_ref = refs[2 * n_pages + 1]
    page = k_pages[0].shape[0]
    rows = t_new * N_HEADS
    n_blocks = n_pages * page // MOBA_BLOCK
    per_blk = MOBA_BLOCK // page

    qrep = jnp.concatenate(
        [jnp.broadcast_to(q_ref[0, t:t + 1, :], (N_HEADS, ATTN_WIDTH)) for t in range(t_new)], axis=0)
    r_i = lax.broadcasted_iota(jnp.int32, (rows, ATTN_WIDTH), 0)
    c_i = lax.broadcasted_iota(jnp.int32, (rows, ATTN_WIDTH), 1)
    diag = (c_i // HEAD_DIM) == (r_i % N_HEADS)
    qbd = jnp.where(diag, qrep * (HEAD_DIM ** -0.5), 0.0).astype(BF16)

    gcol = lax.broadcasted_iota(jnp.int32, (rows, LANES), 1).astype(F32)
    gate = jnp.full((rows, LANES), NEG, F32)
    for n in range(n_blocks):
        tot = jnp.zeros((rows, 1), F32)
        for pp in range(per_blk):
            p_i = n * per_blk + pp
            sp = _dot_nt(qbd, k_pages[p_i][...].astype(BF16))
            s_ref[:, p_i * page:(p_i + 1) * page] = sp
            tot = tot + jnp.sum(sp, axis=-1, keepdims=True)
        gate = jnp.where(gcol == float(n), tot * (1.0 / MOBA_BLOCK), gate)
    sel = _top_mask(gate, gcol, min(MOBA_TOPK, n_blocks))

    s_own = _dot_nt(qbd, kn_ref[0].astype(BF16))
    oc = lax.broadcasted_iota(jnp.int32, s_own.shape, 1)
    orow = lax.broadcasted_iota(jnp.int32, s_own.shape, 0)
    s_own = jnp.where(oc <= orow // N_HEADS, s_own, NEG)

    m = jnp.max(s_own, axis=-1, keepdims=True)
    for n in range(n_blocks):
        sb = s_ref[:, n * MOBA_BLOCK:(n + 1) * MOBA_BLOCK]
        sb = jnp.where(sel[:, n:n + 1] > 0.5, sb, NEG)
        m = jnp.maximum(m, jnp.max(sb, axis=-1, keepdims=True))

    p_own = jnp.exp(s_own - m)
    l = jnp.sum(p_own, axis=-1, keepdims=True)
    acc = _dot(p_own.astype(BF16), vn_ref[0].astype(BF16))
    for p_i in range(n_pages):
        n = p_i // per_blk
        sp = s_ref[:, p_i * page:(p_i + 1) * page]
        pp = jnp.where(sel[:, n:n + 1] > 0.5, jnp.exp(sp - m), 0.0)
        l = l + jnp.sum(pp, axis=-1, keepdims=True)
        acc = acc + _dot(pp.astype(BF16), v_pages[p_i][...].astype(BF16))
    o = jnp.where(diag, acc * (1.0 / l), 0.0)
    for t in range(t_new):
        o_ref[0, t:t + 1, :] = jnp.sum(o[t * N_HEADS:(t + 1) * N_HEADS, :], axis=0, keepdims=True)


def moba_sample(q, k_new, v_new, cache_k, cache_v, page_table):
    db, t_new, _ = q.shape
    n_pages = page_table.shape[1]
    n_phys, page = cache_k.shape[0], cache_k.shape[1]
    ck = cache_k.reshape(n_phys, page, ATTN_WIDTH)
    cv = cache_v.reshape(n_phys, page, ATTN_WIDTH)
    tok = pl.BlockSpec((1, t_new, ATTN_WIDTH), lambda b, pt: (b, 0, 0))

    def page_spec(p):
        return pl.BlockSpec((None, page, ATTN_WIDTH), lambda b, pt: (pt[b, p], 0, 0))

    grid_spec = pltpu.PrefetchScalarGridSpec(
        num_scalar_prefetch=1,
        grid=(db,),
        in_specs=[tok, tok, tok] + [page_spec(p) for p in range(n_pages)] * 2,
        out_specs=tok,
        scratch_shapes=[pltpu.VMEM((t_new * N_HEADS, n_pages * page), F32)],
    )
    return pl.pallas_call(
        functools.partial(_moba_sample_kernel, n_pages=n_pages, t_new=t_new),
        grid_spec=grid_spec,
        out_shape=jax.ShapeDtypeStruct((db, t_new, ATTN_WIDTH), F32),
        compiler_params=_cparams(("arbitrary",)),
        name="moba_sample",
    )(page_table, q, k_new, v_new, *([ck] * n_pages), *([cv] * n_pages))


def _pool_prompt_kernel(u_ref, prev_ref, d_ref, ext_ref, *, tm):
    i = pl.program_id(0)
    pad = prev_ref.shape[0]
    prev = jnp.where(i > 0, prev_ref[...], 0.0)
    ext_ref[0:pad, :] = prev
    ext_ref[pad:pad + tm, :] = u_ref[...]
    pos = (i * tm + lax.broadcasted_iota(jnp.int32, (tm, 1), 0)).astype(F32)
    for g, w in enumerate(POOL_WINDOWS):
        c0, c1 = g * POOL_GROUP_W, (g + 1) * POOL_GROUP_W
        wsum = ext_ref[pad:pad + tm, c0:c1]
        for jj in range(1, w):
            wsum = wsum + ext_ref[pad - jj:pad - jj + tm, c0:c1]
        inv = 1.0 / jnp.minimum(jnp.float32(w), pos + 1.0)
        d_ref[:, c0:c1] = (wsum * inv - u_ref[:, c0:c1]).astype(d_ref.dtype)


def pool_prompt(u, *, tm):
    n = u.shape[0]
    pad = 16
    return pl.pallas_call(
        functools.partial(_pool_prompt_kernel, tm=tm),
        grid=(n // tm,),
        in_specs=[
            pl.BlockSpec((tm, POOL_WIDTH), lambda i: (i, 0)),
            pl.BlockSpec((pad, POOL_WIDTH), lambda i: (jnp.maximum(i * (tm // pad) - 1, 0), 0)),
        ],
        out_specs=pl.BlockSpec((tm, POOL_WIDTH), lambda i: (i, 0)),
        out_shape=jax.ShapeDtypeStruct((n, POOL_WIDTH), BF16),
        scratch_shapes=[pltpu.VMEM((tm + pad, POOL_WIDTH), F32)],
        compiler_params=_cparams(("arbitrary",)),
        name="pool_prompt",
    )(u, u)


def _pool_sample_kernel(u_ref, st_ref, d_ref, new_ref, *, t_new, start_pos):
    ext = [st_ref[:, r, :] for r in range(POOL_STATE)] + [u_ref[:, t, :] for t in range(t_new)]
    for t in range(t_new):
        e = POOL_STATE + t
        parts = []
        for g, w in enumerate(POOL_WINDOWS):
            c0, c1 = g * POOL_GROUP_W, (g + 1) * POOL_GROUP_W
            wsum = ext[e][:, c0:c1]
            for jj in range(1, w):
                wsum = wsum + ext[e - jj][:, c0:c1]
            count = min(float(w), float(start_pos + t) + 1.0)
            parts.append(wsum * (1.0 / count) - ext[e][:, c0:c1])
        d_ref[:, t, :] = jnp.concatenate(parts, axis=-1).astype(d_ref.dtype)
    for r in range(POOL_STATE):
        new_ref[:, r, :] = ext[t_new + r]


def pool_sample(u, state, *, start_pos, bb=32):
    db, t_new, _ = u.shape
    return pl.pallas_call(
        functools.partial(_pool_sample_kernel, t_new=t_new, start_pos=start_pos),
        grid=(db // bb,),
        in_specs=[
            pl.BlockSpec((bb, t_new, POOL_WIDTH), lambda i: (i, 0, 0)),
            pl.BlockSpec((bb, POOL_STATE, POOL_WIDTH), lambda i: (i, 0, 0)),
        ],
        out_specs=(
            pl.BlockSpec((bb, t_new, POOL_WIDTH), lambda i: (i, 0, 0)),
            pl.BlockSpec((bb, POOL_STATE, POOL_WIDTH), lambda i: (i, 0, 0)),
        ),
        out_shape=(
            jax.ShapeDtypeStruct((db, t_new, POOL_WIDTH), F32),
            jax.ShapeDtypeStruct((db, POOL_STATE, POOL_WIDTH), F32),
        ),
        compiler_params=_cparams(("arbitrary",)),
        name="pool_sample",
    )(u, state)


def _out_proj_kernel(x_ref, oa_ref, d_ref, wp_ref, ps_ref, wo_ref, gf_ref, wr_ref, br_ref,
                     h2_ref, hn_ref, cmb_ref):
    dd = d_ref[...].astype(BF16)
    acc = _dot(oa_ref[...].astype(BF16), wo_ref[0:ATTN_WIDTH, :])
    for g in range(len(POOL_WINDOWS)):
        c0, c1 = g * POOL_GROUP_W, (g + 1) * POOL_GROUP_W
        yg = _dot(dd[:, c0:c1], wp_ref[g].astype(BF16)) * ps_ref[:, c0:c1]
        acc = acc + _dot(yg.astype(BF16), wo_ref[ATTN_WIDTH + c0:ATTN_WIDTH + c1, :])
    h2 = x_ref[...] + acc
    h2_ref[...] = h2
    ms = jnp.mean(h2 * h2, axis=-1, keepdims=True)
    hn = h2 * lax.rsqrt(ms + EPS) * gf_ref[...]
    hn_ref[...] = hn.astype(BF16)

    logits = jnp.dot(hn, wr_ref[...], precision=lax.Precision.HIGHEST,
                     preferred_element_type=F32) + br_ref[...]
    lane = lax.broadcasted_iota(jnp.int32, logits.shape, 1).astype(F32)
    far = jnp.float32(LANES)
    is_g = lane < N_EXPERT_GROUPS
    gl = jnp.where(is_g, logits, -jnp.inf)
    g_max = jnp.max(gl, axis=-1, keepdims=True)
    g_top = jnp.min(jnp.where(gl == g_max, lane, far), axis=-1, keepdims=True)
    g_p = 1.0 / jnp.sum(jnp.where(is_g, jnp.exp(gl - g_max), 0.0), axis=-1, keepdims=True)
    lo = N_EXPERT_GROUPS + g_top * EXPERTS_PER_GROUP
    in_grp = jnp.logical_and(lane >= lo, lane < lo + EXPERTS_PER_GROUP)
    el = jnp.where(in_grp, logits, -jnp.inf)
    e1 = jnp.max(el, axis=-1, keepdims=True)
    i1 = jnp.min(jnp.where(el == e1, lane, far), axis=-1, keepdims=True)
    el2 = jnp.where(lane == i1, -jnp.inf, el)
    e2 = jnp.max(el2, axis=-1, keepdims=True)
    i2 = jnp.min(jnp.where(el2 == e2, lane, far), axis=-1, keepdims=True)
    ex2 = jnp.exp(e2 - e1)
    den = 1.0 + ex2
    w1 = (1.0 / den) * g_p
    w2 = (ex2 / den) * g_p
    cmb_ref[...] = jnp.where(lane == i1, w1, 0.0) + jnp.where(lane == i2, w2, 0.0)


def out_proj(x, o_attn, d, w_pool, pool_scale, w_out_bf, g_ffn, w_router, b_router, *, tm):
    n = x.shape[0]
    row = lambda i: (i, 0)
    const = lambda i: (0, 0)
    return pl.pallas_call(
        _out_proj_kernel,
        grid=(n // tm,),
        in_specs=[
            pl.BlockSpec((tm, D_MODEL), row),
            pl.BlockSpec((tm, ATTN_WIDTH), row),
            pl.BlockSpec((tm, POOL_WIDTH), row),
            pl.BlockSpec(w_pool.shape, lambda i: (0, 0, 0)),
            pl.BlockSpec((1, POOL_WIDTH), const),
            pl.BlockSpec((D_MODEL, D_MODEL), const),
            pl.BlockSpec((1, D_MODEL), const),
            pl.BlockSpec((D_MODEL, LANES), const),
            pl.BlockSpec((1, LANES), const),
        ],
        out_specs=(
            pl.BlockSpec((tm, D_MODEL), row),
            pl.BlockSpec((tm, D_MODEL), row),
            pl.BlockSpec((tm, LANES), row),
        ),
        out_shape=(
            jax.ShapeDtypeStruct((n, D_MODEL), F32),
            jax.ShapeDtypeStruct((n, D_MODEL), BF16),
            jax.ShapeDtypeStruct((n, LANES), F32),
        ),
        compiler_params=_cparams(("arbitrary",)),
        name="out_proj",
    )(x, o_attn, d, w_pool, pool_scale, w_out_bf, g_ffn, w_router, b_router)


def _moe_dense_kernel(hn_ref, h2_ref, cmb_ref, wg_ref, wu_ref, wd_ref, y_ref):
    e = pl.program_id(1)

    @pl.when(e == 0)
    def _():
        y_ref[...] = h2_ref[...]

    x = hn_ref[...]
    a = _dot(x, wg_ref[0])
    b = _dot(x, wu_ref[0])
    cmb = cmb_ref[...]
    lane = lax.broadcasted_iota(jnp.int32, cmb.shape, 1)
    cw = jnp.sum(jnp.where(lane == e + N_EXPERT_GROUPS, cmb, 0.0), axis=-1, keepdims=True)
    act = (a * (1.0 / (1.0 + jnp.exp(-a)))) * b * cw
    y_ref[...] += _dot(act.astype(BF16), wd_ref[0])


def moe_dense(hn, h2, cmb, wg_bf, wu_bf, wd_bf, *, tm):
    n = hn.shape[0]
    row = lambda i, e: (i, 0)
    return pl.pallas_call(
        _moe_dense_kernel,
        grid=(n // tm, N_EXPERTS),
        in_specs=[
            pl.BlockSpec((tm, D_MODEL), row),
            pl.BlockSpec((tm, D_MODEL), row),
            pl.BlockSpec((tm, LANES), row),
            pl.BlockSpec((1, D_MODEL, D_EXPERT), lambda i, e: (e, 0, 0)),
            pl.BlockSpec((1, D_MODEL, D_EXPERT), lambda i, e: (e, 0, 0)),
            pl.BlockSpec((1, D_EXPERT, D_MODEL), lambda i, e: (e, 0, 0)),
        ],
        out_specs=pl.BlockSpec((tm, D_MODEL), row),
        out_shape=jax.ShapeDtypeStruct((n, D_MODEL), F32),
        compiler_params=_cparams(("arbitrary", "arbitrary")),
        name="moe_dense",
    )(hn, h2, cmb, wg_bf, wu_bf, wd_bf)


def kernel(x_prompt, x_sample, cache_k, cache_v, state_pool, page_table, g_mix, w_in, g_q, g_k, w_pool, pool_scale, w_out, g_ffn, w_group_router, b_group_router, w_expert_router, b_expert_router, w_gate, w_up, w_down):
    B, S, _ = x_prompt.shape
    DB, T, _ = x_sample.shape
    depth = w_in.shape[0]
    assert B == 1 and depth == 1
    past_len = page_table.shape[1] * cache_k.shape[2]
    l = 0

    w_in_bf = w_in[l].astype(BF16)
    w_out_bf = w_out[l].astype(BF16)
    wg_bf, wu_bf, wd_bf = w_gate[l].astype(BF16), w_up[l].astype(BF16), w_down[l].astype(BF16)
    gm, gq, gk, gf = g_mix[l][None], g_q[l][None], g_k[l][None], g_ffn[l][None]
    ps = pool_scale[l][None]
    n_r = N_EXPERT_GROUPS + N_EXPERTS
    w_router = jnp.concatenate([w_group_router[l], w_expert_router[l].reshape(D_MODEL, N_EXPERTS)], axis=1)
    w_router = jnp.pad(w_router, ((0, 0), (0, LANES - n_r)))
    b_router = jnp.concatenate([b_group_router[l], b_expert_router[l].reshape(N_EXPERTS)])
    b_router = jnp.pad(b_router, (0, LANES - n_r))[None]

    def tail(x2d, o_attn, d):
        h2, hn, cmb = out_proj(x2d, o_attn, d, w_pool[l], ps, w_out_bf, gf, w_router, b_router, tm=512)
        return moe_dense(hn, h2, cmb, wg_bf, wu_bf, wd_bf, tm=512)

    xp = x_prompt.reshape(S, D_MODEL)
    q_p, k_p, kb_p, km_p, v_p, vb_p, u_p = in_proj(xp, gm, w_in_bf, gq, gk, tm=512)
    o_p = moba_prompt(q_p, kb_p, vb_p, km_p.reshape(S // MOBA_BLOCK, ATTN_WIDTH))
    d_p = pool_prompt(u_p, tm=512)
    y_prompt = tail(xp, o_p, d_p).reshape(B, S, D_MODEL)

    xs = x_sample.reshape(DB * T, D_MODEL)
    q_s, k_s, _, _, v_s, _, u_s = in_proj(xs, gm, w_in_bf, gq, gk, tm=DB * T)
    r3 = lambda a: a.reshape(DB, T, ATTN_WIDTH)
    o_s = moba_sample(r3(q_s), r3(k_s), r3(v_s), cache_k[l], cache_v[l], page_table)
    d_s, pool_s = pool_sample(u_s.reshape(DB, T, POOL_WIDTH), state_pool[l], start_pos=past_len)
    y_sample = tail(xs, o_s.reshape(DB * T, ATTN_WIDTH), d_s.reshape(DB * T, POOL_WIDTH)).reshape(DB, T, D_MODEL)

    hd = (N_HEADS, HEAD_DIM)
    return (
        y_prompt,
        y_sample,
        k_p.reshape(1, B, S, *hd),
        v_p.reshape(1, B, S, *hd),
        u_p[S - POOL_STATE:].reshape(1, B, POOL_STATE, POOL_WIDTH),
        k_s.reshape(1, DB, T, *hd),
        v_s.reshape(1, DB, T, *hd),
        pool_s.reshape(1, DB, POOL_STATE, POOL_WIDTH),
    )
```

```python
import functools

import jax
import jax.numpy as jnp
from jax import lax
from jax.experimental import pallas as pl
from jax.experimental.pallas import tpu as pltpu

D_MODEL = 2048
ATTN_WIDTH = 1024
POOL_WIDTH = 1024
HEAD_DIM = 128
N_HEADS = 8
POOL_WINDOWS = (2, 4, 8, 16)
POOL_GROUP_W = 256
POOL_STATE = 15
MOBA_BLOCK = 256
MOBA_TOPK = 3
N_EXPERT_GROUPS = 4
EXPERTS_PER_GROUP = 4
N_EXPERTS = 16
D_EXPERT = 768
EPS = 1e-6
NEG = -1e30
LANES = 128
VMEM_LIMIT = 56 * 1024 * 1024

BF16 = jnp.bfloat16
F32 = jnp.float32


def _cparams(sem):
    return pltpu.CompilerParams(dimension_semantics=sem, vmem_limit_bytes=VMEM_LIMIT)


def _dot(a, b):
    return jnp.dot(a, b, preferred_element_type=F32)


def _dot_nt(a, b, precision=None):
    return lax.dot_general(a, b, (((1,), (1,)), ((), ())), precision=precision,
                           preferred_element_type=F32)


def _top_mask(g, ids, k, axis=-1):
    sel = jnp.zeros(g.shape, F32)
    for _ in range(k):
        m = jnp.max(g, axis=axis, keepdims=True)
        idx = jnp.min(jnp.where(g == m, ids, jnp.float32(g.shape[axis])), axis=axis, keepdims=True)
        pick = ids == idx
        sel = jnp.where(pick, 1.0, sel)
        g = jnp.where(pick, -jnp.inf, g)
    return sel


def _in_proj_kernel(x_ref, gmix_ref, w_ref, gq_ref, gk_ref,
                    q_ref, k_ref, kb_ref, km_ref, v_ref, vb_ref, u_ref, hb_ref, *, tm):
    j = pl.program_id(1)

    @pl.when(j == 0)
    def _():
        xf = x_ref[...]
        ms = jnp.mean(xf * xf, axis=-1, keepdims=True)
        hb_ref[...] = (xf * lax.rsqrt(ms + EPS) * gmix_ref[...]).astype(BF16)

    z = _dot(hb_ref[...], w_ref[...])

    def head_norm(out_ref, g_ref):
        for h in range(N_HEADS):
            zh = z[:, h * HEAD_DIM:(h + 1) * HEAD_DIM]
            ms = jnp.mean(zh * zh, axis=-1, keepdims=True)
            out_ref[:, h * HEAD_DIM:(h + 1) * HEAD_DIM] = zh * lax.rsqrt(ms + EPS) * g_ref[...]

    @pl.when(j == 0)
    def _():
        head_norm(q_ref, gq_ref)

    @pl.when(j == 1)
    def _():
        head_norm(k_ref, gk_ref)
        kb_ref[...] = k_ref[...].astype(BF16)
        for g in range(tm // MOBA_BLOCK):
            blk = k_ref[g * MOBA_BLOCK:(g + 1) * MOBA_BLOCK, :]
            km_ref[g] = jnp.mean(blk, axis=0, keepdims=True)

    @pl.when(j == 2)
    def _():
        v_ref[...] = z
        vb_ref[...] = z.astype(BF16)

    @pl.when(j == 3)
    def _():
        u_ref[...] = z


def in_proj(x, g_mix, w_in_bf, g_q, g_k, *, tm):
    n = x.shape[0]
    wide = lambda i, j: (i, 0)
    const = lambda i, j: (0, 0)
    out_shape = (
        jax.ShapeDtypeStruct((n, ATTN_WIDTH), F32),
        jax.ShapeDtypeStruct((n, ATTN_WIDTH), F32),
        jax.ShapeDtypeStruct((n, ATTN_WIDTH), BF16),
        jax.ShapeDtypeStruct((n // MOBA_BLOCK, 1, ATTN_WIDTH), F32),
        jax.ShapeDtypeStruct((n, ATTN_WIDTH), F32),
        jax.ShapeDtypeStruct((n, ATTN_WIDTH), BF16),
        jax.ShapeDtypeStruct((n, POOL_WIDTH), F32),
    )
    blk = pl.BlockSpec((tm, ATTN_WIDTH), wide)
    return pl.pallas_call(
        functools.partial(_in_proj_kernel, tm=tm),
        grid=(n // tm, 4),
        in_specs=[
            pl.BlockSpec((tm, D_MODEL), wide),
            pl.BlockSpec((1, D_MODEL), const),
            pl.BlockSpec((D_MODEL, ATTN_WIDTH), lambda i, j: (0, j)),
            pl.BlockSpec((1, HEAD_DIM), const),
            pl.BlockSpec((1, HEAD_DIM), const),
        ],
        out_specs=(blk, blk, blk,
                   pl.BlockSpec((tm // MOBA_BLOCK, 1, ATTN_WIDTH), lambda i, j: (i, 0, 0)),
                   blk, blk, blk),
        out_shape=out_shape,
        scratch_shapes=[pltpu.VMEM((tm, D_MODEL), BF16)],
        compiler_params=_cparams(("arbitrary", "arbitrary")),
        name="in_proj",
    )(x, g_mix, w_in_bf, g_q, g_k)


LOG2E = 1.4426950408889634
ATT_HEADS = 2
ATT_GROUP = 4


def _moba_prompt_kernel(q_ref, k_ref, vt_ref, km_ref, o_ref, qs_ref, bias_ref, m_ref, l_ref, acc_ref,
                        *, nb, hp, grp):
    own = pl.program_id(1)
    base = pl.multiple_of(own * MOBA_BLOCK, MOBA_BLOCK)
    blk_id = lax.broadcasted_iota(jnp.int32, (nb, MOBA_BLOCK), 0).astype(F32)
    valid = blk_id < own.astype(F32)
    key_i = lax.broadcasted_iota(jnp.int32, (MOBA_BLOCK, MOBA_BLOCK), 0)
    qry_i = lax.broadcasted_iota(jnp.int32, (MOBA_BLOCK, MOBA_BLOCK), 1)

    for hh in range(hp):
        cols = slice(hh * HEAD_DIM, (hh + 1) * HEAD_DIM)
        q = q_ref[:, cols]
        gate = _dot_nt(km_ref[:, cols], q, precision=lax.Precision.HIGHEST)
        sel = _top_mask(jnp.where(valid, gate, NEG), blk_id, MOBA_TOPK, axis=0)
        bias = jnp.where(jnp.logical_and(sel > 0.5, valid), 0.0, NEG)
        for n in range(nb):
            bias_ref[hh, n] = bias[n:n + 1, :]
        qs = (q * (HEAD_DIM ** -0.5 * LOG2E)).astype(BF16)
        qs_ref[hh] = qs
        s = _dot_nt(k_ref[pl.ds(base, MOBA_BLOCK), cols], qs)
        s = jnp.where(key_i <= qry_i, s, NEG)
        m0 = jnp.max(s, axis=0, keepdims=True)
        p = jnp.exp2(s - m0)
        m_ref[hh] = m0
        l_ref[hh] = jnp.sum(p, axis=0, keepdims=True)
        acc_ref[hh] = _dot(vt_ref[own, cols, :], p.astype(BF16))

    def body(i, carry):
        for hh in range(hp):
            cols = slice(hh * HEAD_DIM, (hh + 1) * HEAD_DIM)
            qs = qs_ref[hh]
            m_prev = m_ref[hh]
            m_new = m_prev
            scores = []
            for gg in range(grp):
                n = i * grp + gg
                off = pl.multiple_of(n * MOBA_BLOCK, MOBA_BLOCK)
                s = _dot_nt(k_ref[pl.ds(off, MOBA_BLOCK), cols], qs) + bias_ref[hh, n]
                scores.append(s)
                m_new = jnp.maximum(m_new, jnp.max(s, axis=0, keepdims=True))
            alpha = jnp.exp2(m_prev - m_new)
            l_new = alpha * l_ref[hh]
            acc = alpha * acc_ref[hh]
            for gg in range(grp):
                p = jnp.exp2(scores[gg] - m_new)
                l_new = l_new + jnp.sum(p, axis=0, keepdims=True)
                acc = acc + _dot(vt_ref[i * grp + gg, cols, :], p.astype(BF16))
            m_ref[hh] = m_new
            l_ref[hh] = l_new
            acc_ref[hh] = acc
        return carry

    lax.fori_loop(0, (own + grp - 1) // grp, body, 0)
    for hh in range(hp):
        o = acc_ref[hh] * (1.0 / l_ref[hh])
        o_ref[:, hh * HEAD_DIM:(hh + 1) * HEAD_DIM] = o.T.astype(o_ref.dtype)


def moba_prompt(q, k_bf, v_bf, kmean):
    s = q.shape[0]
    nb = s // MOBA_BLOCK
    hp = ATT_HEADS
    grp = ATT_GROUP if nb % ATT_GROUP == 0 else 1
    wide = hp * HEAD_DIM
    vt = v_bf.reshape(nb, MOBA_BLOCK, ATTN_WIDTH).transpose(0, 2, 1)
    return pl.pallas_call(
        functools.partial(_moba_prompt_kernel, nb=nb, hp=hp, grp=grp),
        grid=(N_HEADS // hp, nb),
        in_specs=[
            pl.BlockSpec((MOBA_BLOCK, wide), lambda h, j: (j, h)),
            pl.BlockSpec((s, wide), lambda h, j: (0, h)),
            pl.BlockSpec((nb, wide, MOBA_BLOCK), lambda h, j: (0, h, 0)),
            pl.BlockSpec((nb, wide), lambda h, j: (0, h)),
        ],
        out_specs=pl.BlockSpec((MOBA_BLOCK, wide), lambda h, j: (j, h)),
        out_shape=jax.ShapeDtypeStruct((s, ATTN_WIDTH), BF16),
        scratch_shapes=[
            pltpu.VMEM((hp, MOBA_BLOCK, HEAD_DIM), BF16),
            pltpu.VMEM((hp, nb, 1, MOBA_BLOCK), F32),
            pltpu.VMEM((hp, 1, MOBA_BLOCK), F32),
            pltpu.VMEM((hp, 1, MOBA_BLOCK), F32),
            pltpu.VMEM((hp, HEAD_DIM, MOBA_BLOCK), F32),
        ],
        compiler_params=_cparams(("arbitrary", "arbitrary")),
        name="moba_prompt",
    )(q, k_bf, vt, kmean)


def _page_bf16(ref):
    page = ref.shape[0] // N_HEADS
    heads = [ref[pl.ds(h, page, stride=N_HEADS), :] for h in range(N_HEADS)]
    return jnp.concatenate(heads, axis=1).astype(BF16)


def _moba_sample_kernel(pt_ref, q_ref, kn_ref, vn_ref, *refs, n_pages, t_new):
    k_pages = refs[:n_pages]
    v_pages = refs[n_pages:2 * n_pages]
    o_ref = refs[2 * n_pages]
    s_ref = refs[2 * n_pages + 1]
    page = k_pages[0].shape[0] // N_HEADS
    rows = t_new * N_HEADS
    n_blocks = n_pages * page // MOBA_BLOCK
    per_blk = MOBA_BLOCK // page

    qrep = jnp.concatenate(
        [jnp.broadcast_to(q_ref[0, t:t + 1, :], (N_HEADS, ATTN_WIDTH)) for t in range(t_new)], axis=0)
    r_i = lax.broadcasted_iota(jnp.int32, (rows, ATTN_WIDTH), 0)
    c_i = lax.broadcasted_iota(jnp.int32, (rows, ATTN_WIDTH), 1)
    diag = (c_i // HEAD_DIM) == (r_i % N_HEADS)
    qbd = jnp.where(diag, qrep * (HEAD_DIM ** -0.5), 0.0).astype(BF16)

    gcol = lax.broadcasted_iota(jnp.int32, (rows, LANES), 1).astype(F32)
    gate = jnp.full((rows, LANES), NEG, F32)
    for n in range(n_blocks):
        tot = jnp.zeros((rows, 1), F32)
        for pp in range(per_blk):
            p_i = n * per_blk + pp
            sp = _dot_nt(qbd, _page_bf16(k_pages[p_i]))
            s_ref[:, p_i * page:(p_i + 1) * page] = sp
            tot = tot + jnp.sum(sp, axis=-1, keepdims=True)
        gate = jnp.where(gcol == float(n), tot * (1.0 / MOBA_BLOCK), gate)
    sel = _top_mask(gate, gcol, min(MOBA_TOPK, n_blocks))

    s_own = _dot_nt(qbd, kn_ref[0].astype(BF16))
    oc = lax.broadcasted_iota(jnp.int32, s_own.shape, 1)
    orow = lax.broadcasted_iota(jnp.int32, s_own.shape, 0)
    s_own = jnp.where(oc <= orow // N_HEADS, s_own, NEG)

    m = jnp.max(s_own, axis=-1, keepdims=True)
    for n in range(n_blocks):
        sb = s_ref[:, n * MOBA_BLOCK:(n + 1) * MOBA_BLOCK]
        sb = jnp.where(sel[:, n:n + 1] > 0.5, sb, NEG)
        m = jnp.maximum(m, jnp.max(sb, axis=-1, keepdims=True))

    p_own = jnp.exp(s_own - m)
    l = jnp.sum(p_own, axis=-1, keepdims=True)
    acc = _dot(p_own.astype(BF16), vn_ref[0].astype(BF16))
    for p_i in range(n_pages):
        n = p_i // per_blk
        sp = s_ref[:, p_i * page:(p_i + 1) * page]
        pp = jnp.where(sel[:, n:n + 1] > 0.5, jnp.exp(sp - m), 0.0)
        l = l + jnp.sum(pp, axis=-1, keepdims=True)
        acc = acc + _dot(pp.astype(BF16), _page_bf16(v_pages[p_i]))
    o = jnp.where(diag, acc * (1.0 / l), 0.0)
    for t in range(t_new):
        o_ref[0, t:t + 1, :] = jnp.sum(o[t * N_HEADS:(t + 1) * N_HEADS, :], axis=0, keepdims=True)


def moba_sample(q, k_new, v_new, cache_k, cache_v, page_table):
    db, t_new, _ = q.shape
    n_pages = page_table.shape[1]
    page = cache_k.shape[-3]
    ck = cache_k.reshape(-1, page * N_HEADS, HEAD_DIM)
    cv = cache_v.reshape(-1, page * N_HEADS, HEAD_DIM)
    tok = pl.BlockSpec((1, t_new, ATTN_WIDTH), lambda b, pt: (b, 0, 0))

    def page_spec(p):
        return pl.BlockSpec((None, page * N_HEADS, HEAD_DIM), lambda b, pt: (pt[b, p], 0, 0))

    grid_spec = pltpu.PrefetchScalarGridSpec(
        num_scalar_prefetch=1,
        grid=(db,),
        in_specs=[tok, tok, tok] + [page_spec(p) for p in range(n_pages)] * 2,
        out_specs=tok,
        scratch_shapes=[pltpu.VMEM((t_new * N_HEADS, n_pages * page), F32)],
    )
    return pl.pallas_call(
        functools.partial(_moba_sample_kernel, n_pages=n_pages, t_new=t_new),
        grid_spec=grid_spec,
        out_shape=jax.ShapeDtypeStruct((db, t_new, ATTN_WIDTH), F32),
        compiler_params=_cparams(("arbitrary",)),
        name="moba_sample",
    )(page_table, q, k_new, v_new, *([ck] * n_pages), *([cv] * n_pages))


def _pool_prompt_kernel(u_ref, prev_ref, d_ref, ext_ref, *, tm):
    i = pl.program_id(0)
    pad = prev_ref.shape[0]
    prev = jnp.where(i > 0, prev_ref[...], 0.0)
    ext_ref[0:pad, :] = prev
    ext_ref[pad:pad + tm, :] = u_ref[...]
    pos = (i * tm + lax.broadcasted_iota(jnp.int32, (tm, 1), 0)).astype(F32)
    for g, w in enumerate(POOL_WINDOWS):
        c0, c1 = g * POOL_GROUP_W, (g + 1) * POOL_GROUP_W
        wsum = ext_ref[pad:pad + tm, c0:c1]
        for jj in range(1, w):
            wsum = wsum + ext_ref[pad - jj:pad - jj + tm, c0:c1]
        inv = 1.0 / jnp.minimum(jnp.float32(w), pos + 1.0)
        d_ref[:, c0:c1] = (wsum * inv - u_ref[:, c0:c1]).astype(d_ref.dtype)


def pool_prompt(u, *, tm):
    n = u.shape[0]
    pad = 16
    return pl.pallas_call(
        functools.partial(_pool_prompt_kernel, tm=tm),
        grid=(n // tm,),
        in_specs=[
            pl.BlockSpec((tm, POOL_WIDTH), lambda i: (i, 0)),
            pl.BlockSpec((pad, POOL_WIDTH), lambda i: (jnp.maximum(i * (tm // pad) - 1, 0), 0)),
        ],
        out_specs=pl.BlockSpec((tm, POOL_WIDTH), lambda i: (i, 0)),
        out_shape=jax.ShapeDtypeStruct((n, POOL_WIDTH), BF16),
        scratch_shapes=[pltpu.VMEM((tm + pad, POOL_WIDTH), F32)],
        compiler_params=_cparams(("arbitrary",)),
        name="pool_prompt",
    )(u, u)


def _pool_sample_kernel(u_ref, st_ref, d_ref, new_ref, *, t_new, start_pos):
    ext = [st_ref[:, r, :] for r in range(POOL_STATE)] + [u_ref[:, t, :] for t in range(t_new)]
    for t in range(t_new):
        e = POOL_STATE + t
        parts = []
        for g, w in enumerate(POOL_WINDOWS):
            c0, c1 = g * POOL_GROUP_W, (g + 1) * POOL_GROUP_W
            wsum = ext[e][:, c0:c1]
            for jj in range(1, w):
                wsum = wsum + ext[e - jj][:, c0:c1]
            count = min(float(w), float(start_pos + t) + 1.0)
            parts.append(wsum * (1.0 / count) - ext[e][:, c0:c1])
        d_ref[:, t, :] = jnp.concatenate(parts, axis=-1).astype(d_ref.dtype)
    for r in range(POOL_STATE):
        new_ref[:, r, :] = ext[t_new + r]


def pool_sample(u, state, *, start_pos, bb=32):
    db, t_new, _ = u.shape
    return pl.pallas_call(
        functools.partial(_pool_sample_kernel, t_new=t_new, start_pos=start_pos),
        grid=(db // bb,),
        in_specs=[
            pl.BlockSpec((bb, t_new, POOL_WIDTH), lambda i: (i, 0, 0)),
            pl.BlockSpec((bb, POOL_STATE, POOL_WIDTH), lambda i: (i, 0, 0)),
        ],
        out_specs=(
            pl.BlockSpec((bb, t_new, POOL_WIDTH), lambda i: (i, 0, 0)),
            pl.BlockSpec((bb, POOL_STATE, POOL_WIDTH), lambda i: (i, 0, 0)),
        ),
        out_shape=(
            jax.ShapeDtypeStruct((db, t_new, POOL_WIDTH), F32),
            jax.ShapeDtypeStruct((db, POOL_STATE, POOL_WIDTH), F32),
        ),
        compiler_params=_cparams(("arbitrary",)),
        name="pool_sample",
    )(u, state)


def _out_proj_kernel(x_ref, oa_ref, d_ref, wp_ref, ps_ref, wo_ref, gf_ref, wr_ref, br_ref,
                     h2_ref, hn_ref, cmb_ref):
    dd = d_ref[...].astype(BF16)
    acc = _dot(oa_ref[...].astype(BF16), wo_ref[0:ATTN_WIDTH, :])
    for g in range(len(POOL_WINDOWS)):
        c0, c1 = g * POOL_GROUP_W, (g + 1) * POOL_GROUP_W
        yg = _dot(dd[:, c0:c1], wp_ref[g].astype(BF16)) * ps_ref[:, c0:c1]
        acc = acc + _dot(yg.astype(BF16), wo_ref[ATTN_WIDTH + c0:ATTN_WIDTH + c1, :])
    h2 = x_ref[...] + acc
    h2_ref[...] = h2
    ms = jnp.mean(h2 * h2, axis=-1, keepdims=True)
    hn = h2 * lax.rsqrt(ms + EPS) * gf_ref[...]
    hn_ref[...] = hn.astype(BF16)

    logits = jnp.dot(hn, wr_ref[...], precision=lax.Precision.HIGHEST,
                     preferred_element_type=F32) + br_ref[...]
    lane = lax.broadcasted_iota(jnp.int32, logits.shape, 1).astype(F32)
    far = jnp.float32(LANES)
    is_g = lane < N_EXPERT_GROUPS
    gl = jnp.where(is_g, logits, -jnp.inf)
    g_max = jnp.max(gl, axis=-1, keepdims=True)
    g_top = jnp.min(jnp.where(gl == g_max, lane, far), axis=-1, keepdims=True)
    g_p = 1.0 / jnp.sum(jnp.where(is_g, jnp.exp(gl - g_max), 0.0), axis=-1, keepdims=True)
    lo = N_EXPERT_GROUPS + g_top * EXPERTS_PER_GROUP
    in_grp = jnp.logical_and(lane >= lo, lane < lo + EXPERTS_PER_GROUP)
    el = jnp.where(in_grp, logits, -jnp.inf)
    e1 = jnp.max(el, axis=-1, keepdims=True)
    i1 = jnp.min(jnp.where(el == e1, lane, far), axis=-1, keepdims=True)
    el2 = jnp.where(lane == i1, -jnp.inf, el)
    e2 = jnp.max(el2, axis=-1, keepdims=True)
    i2 = jnp.min(jnp.where(el2 == e2, lane, far), axis=-1, keepdims=True)
    ex2 = jnp.exp(e2 - e1)
    den = 1.0 + ex2
    w1 = (1.0 / den) * g_p
    w2 = (ex2 / den) * g_p
    cmb_ref[...] = jnp.where(lane == i1, w1, 0.0) + jnp.where(lane == i2, w2, 0.0)


def out_proj(x, o_attn, d, w_pool, pool_scale, w_out_bf, g_ffn, w_router, b_router, *, tm):
    n = x.shape[0]
    row = lambda i: (i, 0)
    const = lambda i: (0, 0)
    return pl.pallas_call(
        _out_proj_kernel,
        grid=(n // tm,),
        in_specs=[
            pl.BlockSpec((tm, D_MODEL), row),
            pl.BlockSpec((tm, ATTN_WIDTH), row),
            pl.BlockSpec((tm, POOL_WIDTH), row),
            pl.BlockSpec(w_pool.shape, lambda i: (0, 0, 0)),
            pl.BlockSpec((1, POOL_WIDTH), const),
            pl.BlockSpec((D_MODEL, D_MODEL), const),
            pl.BlockSpec((1, D_MODEL), const),
            pl.BlockSpec((D_MODEL, LANES), const),
            pl.BlockSpec((1, LANES), const),
        ],
        out_specs=(
            pl.BlockSpec((tm, D_MODEL), row),
            pl.BlockSpec((tm, D_MODEL), row),
            pl.BlockSpec((tm, LANES), row),
        ),
        out_shape=(
            jax.ShapeDtypeStruct((n, D_MODEL), F32),
            jax.ShapeDtypeStruct((n, D_MODEL), BF16),
            jax.ShapeDtypeStruct((n, LANES), F32),
        ),
        compiler_params=_cparams(("arbitrary",)),
        name="out_proj",
    )(x, o_attn, d, w_pool, pool_scale, w_out_bf, g_ffn, w_router, b_router)


def _moe_dense_kernel(hn_ref, h2_ref, cmb_ref, wg_ref, wu_ref, wd_ref, y_ref):
    e = pl.program_id(1)

    @pl.when(e == 0)
    def _():
        y_ref[...] = h2_ref[...]

    x = hn_ref[...]
    a = _dot(x, wg_ref[0])
    b = _dot(x, wu_ref[0])
    cmb = cmb_ref[...]
    lane = lax.broadcasted_iota(jnp.int32, cmb.shape, 1)
    cw = jnp.sum(jnp.where(lane == e + N_EXPERT_GROUPS, cmb, 0.0), axis=-1, keepdims=True)
    act = (a * (1.0 / (1.0 + jnp.exp(-a)))) * b * cw
    y_ref[...] += _dot(act.astype(BF16), wd_ref[0])


def moe_dense(hn, h2, cmb, wg_bf, wu_bf, wd_bf, *, tm):
    n = hn.shape[0]
    row = lambda i, e: (i, 0)
    return pl.pallas_call(
        _moe_dense_kernel,
        grid=(n // tm, N_EXPERTS),
        in_specs=[
            pl.BlockSpec((tm, D_MODEL), row),
            pl.BlockSpec((tm, D_MODEL), row),
            pl.BlockSpec((tm, LANES), row),
            pl.BlockSpec((1, D_MODEL, D_EXPERT), lambda i, e: (e, 0, 0)),
            pl.BlockSpec((1, D_MODEL, D_EXPERT), lambda i, e: (e, 0, 0)),
            pl.BlockSpec((1, D_EXPERT, D_MODEL), lambda i, e: (e, 0, 0)),
        ],
        out_specs=pl.BlockSpec((tm, D_MODEL), row),
        out_shape=jax.ShapeDtypeStruct((n, D_MODEL), F32),
        compiler_params=_cparams(("arbitrary", "arbitrary")),
        name="moe_dense",
    )(hn, h2, cmb, wg_bf, wu_bf, wd_bf)


def kernel(x_prompt, x_sample, cache_k, cache_v, state_pool, page_table, g_mix, w_in, g_q, g_k, w_pool, pool_scale, w_out, g_ffn, w_group_router, b_group_router, w_expert_router, b_expert_router, w_gate, w_up, w_down):
    B, S, _ = x_prompt.shape
    DB, T, _ = x_sample.shape
    depth = w_in.shape[0]
    assert B == 1 and depth == 1
    past_len = page_table.shape[1] * cache_k.shape[2]
    l = 0

    w_in_bf = w_in[l].astype(BF16)
    w_out_bf = w_out[l].astype(BF16)
    wg_bf, wu_bf, wd_bf = w_gate[l].astype(BF16), w_up[l].astype(BF16), w_down[l].astype(BF16)
    gm, gq, gk, gf = g_mix[l][None], g_q[l][None], g_k[l][None], g_ffn[l][None]
    ps = pool_scale[l][None]
    n_r = N_EXPERT_GROUPS + N_EXPERTS
    w_router = jnp.concatenate([w_group_router[l], w_expert_router[l].reshape(D_MODEL, N_EXPERTS)], axis=1)
    w_router = jnp.pad(w_router, ((0, 0), (0, LANES - n_r)))
    b_router = jnp.concatenate([b_group_router[l], b_expert_router[l].reshape(N_EXPERTS)])
    b_router = jnp.pad(b_router, (0, LANES - n_r))[None]

    def tail(x2d, o_attn, d):
        h2, hn, cmb = out_proj(x2d, o_attn, d, w_pool[l], ps, w_out_bf, gf, w_router, b_router, tm=512)
        return moe_dense(hn, h2, cmb, wg_bf, wu_bf, wd_bf, tm=512)

    xp = x_prompt.reshape(S, D_MODEL)
    q_p, k_p, kb_p, km_p, v_p, vb_p, u_p = in_proj(xp, gm, w_in_bf, gq, gk, tm=512)
    o_p = moba_prompt(q_p, kb_p, vb_p, km_p.reshape(S // MOBA_BLOCK, ATTN_WIDTH))
    d_p = pool_prompt(u_p, tm=512)
    y_prompt = tail(xp, o_p, d_p).reshape(B, S, D_MODEL)

    xs = x_sample.reshape(DB * T, D_MODEL)
    q_s, k_s, _, _, v_s, _, u_s = in_proj(xs, gm, w_in_bf, gq, gk, tm=DB * T)
    r3 = lambda a: a.reshape(DB, T, ATTN_WIDTH)
    o_s = moba_sample(r3(q_s), r3(k_s), r3(v_s), cache_k, cache_v, page_table + l * cache_k.shape[1])
    d_s, pool_s = pool_sample(u_s.reshape(DB, T, POOL_WIDTH), state_pool[l], start_pos=past_len)
    y_sample = tail(xs, o_s.reshape(DB * T, ATTN_WIDTH), d_s.reshape(DB * T, POOL_WIDTH)).reshape(DB, T, D_MODEL)

    hd = (N_HEADS, HEAD_DIM)
    return (
        y_prompt,
        y_sample,
        k_p.reshape(1, B, S, *hd),
        v_p.reshape(1, B, S, *hd),
        u_p[S - POOL_STATE:].reshape(1, B, POOL_STATE, POOL_WIDTH),
        k_s.reshape(1, DB, T, *hd),
        v_s.reshape(1, DB, T, *hd),
        pool_s.reshape(1, DB, POOL_STATE, POOL_WIDTH),
    )
```

```python
import functools

import jax
import jax.numpy as jnp
from jax import lax
from jax.experimental import pallas as pl
from jax.experimental.pallas import tpu as pltpu

D_MODEL = 2048
ATTN_WIDTH = 1024
POOL_WIDTH = 1024
HEAD_DIM = 128
N_HEADS = 8
POOL_WINDOWS = (2, 4, 8, 16)
POOL_GROUP_W = 256
POOL_STATE = 15
MOBA_BLOCK = 256
MOBA_TOPK = 3
N_EXPERT_GROUPS = 4
EXPERTS_PER_GROUP = 4
N_EXPERTS = 16
D_EXPERT = 768
EPS = 1e-6
NEG = -1e30
LANES = 128
VMEM_LIMIT = 56 * 1024 * 1024

BF16 = jnp.bfloat16
F32 = jnp.float32


def _cparams(sem):
    return pltpu.CompilerParams(dimension_semantics=sem, vmem_limit_bytes=VMEM_LIMIT)


def _dot(a, b):
    return jnp.dot(a, b, preferred_element_type=F32)


def _dot_nt(a, b, precision=None):
    return lax.dot_general(a, b, (((1,), (1,)), ((), ())), precision=precision,
                           preferred_element_type=F32)


def _top_mask(g, ids, k, axis=-1):
    sel = jnp.zeros(g.shape, F32)
    for _ in range(k):
        m = jnp.max(g, axis=axis, keepdims=True)
        idx = jnp.min(jnp.where(g == m, ids, jnp.float32(g.shape[axis])), axis=axis, keepdims=True)
        pick = ids == idx
        sel = jnp.where(pick, 1.0, sel)
        g = jnp.where(pick, -jnp.inf, g)
    return sel


def _in_proj_kernel(x_ref, gmix_ref, w_ref, gq_ref, gk_ref,
                    q_ref, k_ref, kb_ref, km_ref, v_ref, vb_ref, u_ref, hb_ref, *, tm):
    j = pl.program_id(1)

    @pl.when(j == 0)
    def _():
        xf = x_ref[...]
        ms = jnp.mean(xf * xf, axis=-1, keepdims=True)
        hb_ref[...] = (xf * lax.rsqrt(ms + EPS) * gmix_ref[...]).astype(BF16)

    z = _dot(hb_ref[...], w_ref[...])

    def head_norm(out_ref, g_ref):
        for h in range(N_HEADS):
            zh = z[:, h * HEAD_DIM:(h + 1) * HEAD_DIM]
            ms = jnp.mean(zh * zh, axis=-1, keepdims=True)
            out_ref[:, h * HEAD_DIM:(h + 1) * HEAD_DIM] = zh * lax.rsqrt(ms + EPS) * g_ref[...]

    @pl.when(j == 0)
    def _():
        head_norm(q_ref, gq_ref)

    @pl.when(j == 1)
    def _():
        head_norm(k_ref, gk_ref)
        kb_ref[...] = k_ref[...].astype(BF16)
        for g in range(tm // MOBA_BLOCK):
            blk = k_ref[g * MOBA_BLOCK:(g + 1) * MOBA_BLOCK, :]
            km_ref[g] = jnp.mean(blk, axis=0, keepdims=True)

    @pl.when(j == 2)
    def _():
        v_ref[...] = z
        vb_ref[...] = z.astype(BF16)

    @pl.when(j == 3)
    def _():
        u_ref[...] = z


def in_proj(x, g_mix, w_in_bf, g_q, g_k, *, tm):
    n = x.shape[0]
    wide = lambda i, j: (i, 0)
    const = lambda i, j: (0, 0)
    out_shape = (
        jax.ShapeDtypeStruct((n, ATTN_WIDTH), F32),
        jax.ShapeDtypeStruct((n, ATTN_WIDTH), F32),
        jax.ShapeDtypeStruct((n, ATTN_WIDTH), BF16),
        jax.ShapeDtypeStruct((n // MOBA_BLOCK, 1, ATTN_WIDTH), F32),
        jax.ShapeDtypeStruct((n, ATTN_WIDTH), F32),
        jax.ShapeDtypeStruct((n, ATTN_WIDTH), BF16),
        jax.ShapeDtypeStruct((n, POOL_WIDTH), F32),
    )
    blk = pl.BlockSpec((tm, ATTN_WIDTH), wide)
    return pl.pallas_call(
        functools.partial(_in_proj_kernel, tm=tm),
        grid=(n // tm, 4),
        in_specs=[
            pl.BlockSpec((tm, D_MODEL), wide),
            pl.BlockSpec((1, D_MODEL), const),
            pl.BlockSpec((D_MODEL, ATTN_WIDTH), lambda i, j: (0, j)),
            pl.BlockSpec((1, HEAD_DIM), const),
            pl.BlockSpec((1, HEAD_DIM), const),
        ],
        out_specs=(blk, blk, blk,
                   pl.BlockSpec((tm // MOBA_BLOCK, 1, ATTN_WIDTH), lambda i, j: (i, 0, 0)),
                   blk, blk, blk),
        out_shape=out_shape,
        scratch_shapes=[pltpu.VMEM((tm, D_MODEL), BF16)],
        compiler_params=_cparams(("arbitrary", "arbitrary")),
        name="in_proj",
    )(x, g_mix, w_in_bf, g_q, g_k)


LOG2E = 1.4426950408889634
ATT_HEADS = 2
ATT_GROUP = 4


def _moba_prompt_kernel(q_ref, k_ref, vt_ref, km_ref, o_ref, qs_ref, bias_ref, m_ref, l_ref, acc_ref,
                        *, nb, hp, grp):
    own = pl.program_id(1)
    base = pl.multiple_of(own * MOBA_BLOCK, MOBA_BLOCK)
    blk_id = lax.broadcasted_iota(jnp.int32, (nb, MOBA_BLOCK), 0).astype(F32)
    valid = blk_id < own.astype(F32)
    key_i = lax.broadcasted_iota(jnp.int32, (MOBA_BLOCK, MOBA_BLOCK), 0)
    qry_i = lax.broadcasted_iota(jnp.int32, (MOBA_BLOCK, MOBA_BLOCK), 1)

    for hh in range(hp):
        cols = slice(hh * HEAD_DIM, (hh + 1) * HEAD_DIM)
        q = q_ref[:, cols]
        gate = _dot_nt(km_ref[:, cols], q, precision=lax.Precision.HIGHEST)
        sel = _top_mask(jnp.where(valid, gate, NEG), blk_id, MOBA_TOPK, axis=0)
        bias = jnp.where(jnp.logical_and(sel > 0.5, valid), 0.0, NEG)
        for n in range(nb):
            bias_ref[hh, n] = bias[n:n + 1, :]
        qs = (q * (HEAD_DIM ** -0.5 * LOG2E)).astype(BF16)
        qs_ref[hh] = qs
        s = _dot_nt(k_ref[pl.ds(base, MOBA_BLOCK), cols], qs)
        s = jnp.where(key_i <= qry_i, s, NEG)
        m0 = jnp.max(s, axis=0, keepdims=True)
        p = jnp.exp2(s - m0)
        m_ref[hh] = m0
        l_ref[hh] = jnp.sum(p, axis=0, keepdims=True)
        acc_ref[hh] = _dot(vt_ref[own, cols, :], p.astype(BF16))

    def body(i, carry):
        for hh in range(hp):
            cols = slice(hh * HEAD_DIM, (hh + 1) * HEAD_DIM)
            qs = qs_ref[hh]
            m_prev = m_ref[hh]
            m_new = m_prev
            scores = []
            for gg in range(grp):
                n = i * grp + gg
                off = pl.multiple_of(n * MOBA_BLOCK, MOBA_BLOCK)
                s = _dot_nt(k_ref[pl.ds(off, MOBA_BLOCK), cols], qs) + bias_ref[hh, n]
                scores.append(s)
                m_new = jnp.maximum(m_new, jnp.max(s, axis=0, keepdims=True))
            alpha = jnp.exp2(m_prev - m_new)
            l_new = alpha * l_ref[hh]
            acc = alpha * acc_ref[hh]
            for gg in range(grp):
                p = jnp.exp2(scores[gg] - m_new)
                l_new = l_new + jnp.sum(p, axis=0, keepdims=True)
                acc = acc + _dot(vt_ref[i * grp + gg, cols, :], p.astype(BF16))
            m_ref[hh] = m_new
            l_ref[hh] = l_new
            acc_ref[hh] = acc
        return carry

    lax.fori_loop(0, (own + grp - 1) // grp, body, 0)
    for hh in range(hp):
        o = acc_ref[hh] * (1.0 / l_ref[hh])
        o_ref[:, hh * HEAD_DIM:(hh + 1) * HEAD_DIM] = o.T.astype(o_ref.dtype)


def moba_prompt(q, k_bf, v_bf, kmean):
    s = q.shape[0]
    nb = s // MOBA_BLOCK
    hp = ATT_HEADS
    grp = ATT_GROUP if nb % ATT_GROUP == 0 else 1
    wide = hp * HEAD_DIM
    vt = v_bf.reshape(nb, MOBA_BLOCK, ATTN_WIDTH).transpose(0, 2, 1)
    return pl.pallas_call(
        functools.partial(_moba_prompt_kernel, nb=nb, hp=hp, grp=grp),
        grid=(N_HEADS // hp, nb),
        in_specs=[
            pl.BlockSpec((MOBA_BLOCK, wide), lambda h, j: (j, h)),
            pl.BlockSpec((s, wide), lambda h, j: (0, h)),
            pl.BlockSpec((nb, wide, MOBA_BLOCK), lambda h, j: (0, h, 0)),
            pl.BlockSpec((nb, wide), lambda h, j: (0, h)),
        ],
        out_specs=pl.BlockSpec((MOBA_BLOCK, wide), lambda h, j: (j, h)),
        out_shape=jax.ShapeDtypeStruct((s, ATTN_WIDTH), BF16),
        scratch_shapes=[
            pltpu.VMEM((hp, MOBA_BLOCK, HEAD_DIM), BF16),
            pltpu.VMEM((hp, nb, 1, MOBA_BLOCK), F32),
            pltpu.VMEM((hp, 1, MOBA_BLOCK), F32),
            pltpu.VMEM((hp, 1, MOBA_BLOCK), F32),
            pltpu.VMEM((hp, HEAD_DIM, MOBA_BLOCK), F32),
        ],
        compiler_params=_cparams(("arbitrary", "arbitrary")),
        name="moba_prompt",
    )(q, k_bf, vt, kmean)


def _page_bf16(ref):
    page = ref.shape[0] // N_HEADS
    heads = [ref[pl.ds(h, page, stride=N_HEADS), :] for h in range(N_HEADS)]
    return jnp.concatenate(heads, axis=1).astype(BF16)


def _moba_sample_kernel(pt_ref, q_ref, kn_ref, vn_ref, *refs, n_pages, t_new):
    k_pages = refs[:n_pages]
    v_pages = refs[n_pages:2 * n_pages]
    o_ref = refs[2 * n_pages]
    s_ref = refs[2 * n_pages + 1]
    page = k_pages[0].shape[0] // N_HEADS
    rows = t_new * N_HEADS
    n_blocks = n_pages * page // MOBA_BLOCK
    per_blk = MOBA_BLOCK // page

    qrep = jnp.concatenate(
        [jnp.broadcast_to(q_ref[0, t:t + 1, :], (N_HEADS, ATTN_WIDTH)) for t in range(t_new)], axis=0)
    r_i = lax.broadcasted_iota(jnp.int32, (rows, ATTN_WIDTH), 0)
    c_i = lax.broadcasted_iota(jnp.int32, (rows, ATTN_WIDTH), 1)
    diag = (c_i // HEAD_DIM) == (r_i % N_HEADS)
    qbd = jnp.where(diag, qrep * (HEAD_DIM ** -0.5), 0.0).astype(BF16)

    gcol = lax.broadcasted_iota(jnp.int32, (rows, LANES), 1).astype(F32)
    gate = jnp.full((rows, LANES), NEG, F32)
    for n in range(n_blocks):
        tot = jnp.zeros((rows, 1), F32)
        for pp in range(per_blk):
            p_i = n * per_blk + pp
            sp = _dot_nt(qbd, _page_bf16(k_pages[p_i]))
            s_ref[:, p_i * page:(p_i + 1) * page] = sp
            tot = tot + jnp.sum(sp, axis=-1, keepdims=True)
        gate = jnp.where(gcol == float(n), tot * (1.0 / MOBA_BLOCK), gate)
    sel = _top_mask(gate, gcol, min(MOBA_TOPK, n_blocks))

    s_own = _dot_nt(qbd, kn_ref[0].astype(BF16))
    oc = lax.broadcasted_iota(jnp.int32, s_own.shape, 1)
    orow = lax.broadcasted_iota(jnp.int32, s_own.shape, 0)
    s_own = jnp.where(oc <= orow // N_HEADS, s_own, NEG)

    m = jnp.max(s_own, axis=-1, keepdims=True)
    for n in range(n_blocks):
        sb = s_ref[:, n * MOBA_BLOCK:(n + 1) * MOBA_BLOCK]
        sb = jnp.where(sel[:, n:n + 1] > 0.5, sb, NEG)
        m = jnp.maximum(m, jnp.max(sb, axis=-1, keepdims=True))

    p_own = jnp.exp(s_own - m)
    l = jnp.sum(p_own, axis=-1, keepdims=True)
    acc = _dot(p_own.astype(BF16), vn_ref[0].astype(BF16))
    for p_i in range(n_pages):
        n = p_i // per_blk
        sp = s_ref[:, p_i * page:(p_i + 1) * page]
        pp = jnp.where(sel[:, n:n + 1] > 0.5, jnp.exp(sp - m), 0.0)
        l = l + jnp.sum(pp, axis=-1, keepdims=True)
        acc = acc + _dot(pp.astype(BF16), _page_bf16(v_pages[p_i]))
    o = jnp.where(diag, acc * (1.0 / l), 0.0)
    for t in range(t_new):
        o_ref[0, t:t + 1, :] = jnp.sum(o[t * N_HEADS:(t + 1) * N_HEADS, :], axis=0, keepdims=True)


def moba_sample(q, k_new, v_new, cache_k, cache_v, page_table):
    db, t_new, _ = q.shape
    n_pages = page_table.shape[1]
    page = cache_k.shape[-3]
    ck = cache_k.reshape(-1, page * N_HEADS, HEAD_DIM)
    cv = cache_v.reshape(-1, page * N_HEADS, HEAD_DIM)
    tok = pl.BlockSpec((1, t_new, ATTN_WIDTH), lambda b, pt: (b, 0, 0))

    def page_spec(p):
        return pl.BlockSpec((None, page * N_HEADS, HEAD_DIM), lambda b, pt: (pt[b, p], 0, 0))

    grid_spec = pltpu.PrefetchScalarGridSpec(
        num_scalar_prefetch=1,
        grid=(db,),
        in_specs=[tok, tok, tok] + [page_spec(p) for p in range(n_pages)] * 2,
        out_specs=tok,
        scratch_shapes=[pltpu.VMEM((t_new * N_HEADS, n_pages * page), F32)],
    )
    return pl.pallas_call(
        functools.partial(_moba_sample_kernel, n_pages=n_pages, t_new=t_new),
        grid_spec=grid_spec,
        out_shape=jax.ShapeDtypeStruct((db, t_new, ATTN_WIDTH), F32),
        compiler_params=_cparams(("arbitrary",)),
        name="moba_sample",
    )(page_table, q, k_new, v_new, *([ck] * n_pages), *([cv] * n_pages))


def _pool_prompt_kernel(u_ref, prev_ref, d_ref, ext_ref, *, tm):
    i = pl.program_id(0)
    pad = prev_ref.shape[0]
    prev = jnp.where(i > 0, prev_ref[...], 0.0)
    ext_ref[0:pad, :] = prev
    ext_ref[pad:pad + tm, :] = u_ref[...]
    pos = (i * tm + lax.broadcasted_iota(jnp.int32, (tm, 1), 0)).astype(F32)
    for g, w in enumerate(POOL_WINDOWS):
        c0, c1 = g * POOL_GROUP_W, (g + 1) * POOL_GROUP_W
        wsum = ext_ref[pad:pad + tm, c0:c1]
        for jj in range(1, w):
            wsum = wsum + ext_ref[pad - jj:pad - jj + tm, c0:c1]
        inv = 1.0 / jnp.minimum(jnp.float32(w), pos + 1.0)
        d_ref[:, c0:c1] = (wsum * inv - u_ref[:, c0:c1]).astype(d_ref.dtype)


def pool_prompt(u, *, tm):
    n = u.shape[0]
    pad = 16
    return pl.pallas_call(
        functools.partial(_pool_prompt_kernel, tm=tm),
        grid=(n // tm,),
        in_specs=[
            pl.BlockSpec((tm, POOL_WIDTH), lambda i: (i, 0)),
            pl.BlockSpec((pad, POOL_WIDTH), lambda i: (jnp.maximum(i * (tm // pad) - 1, 0), 0)),
        ],
        out_specs=pl.BlockSpec((tm, POOL_WIDTH), lambda i: (i, 0)),
        out_shape=jax.ShapeDtypeStruct((n, POOL_WIDTH), BF16),
        scratch_shapes=[pltpu.VMEM((tm + pad, POOL_WIDTH), F32)],
        compiler_params=_cparams(("arbitrary",)),
        name="pool_prompt",
    )(u, u)


def _pool_sample_kernel(u_ref, st_ref, d_ref, new_ref, *, t_new, start_pos):
    ext = [st_ref[:, r, :] for r in range(POOL_STATE)] + [u_ref[:, t, :] for t in range(t_new)]
    for t in range(t_new):
        e = POOL_STATE + t
        parts = []
        for g, w in enumerate(POOL_WINDOWS):
            c0, c1 = g * POOL_GROUP_W, (g + 1) * POOL_GROUP_W
            wsum = ext[e][:, c0:c1]
            for jj in range(1, w):
                wsum = wsum + ext[e - jj][:, c0:c1]
            count = min(float(w), float(start_pos + t) + 1.0)
            parts.append(wsum * (1.0 / count) - ext[e][:, c0:c1])
        d_ref[:, t, :] = jnp.concatenate(parts, axis=-1).astype(d_ref.dtype)
    for r in range(POOL_STATE):
        new_ref[:, r, :] = ext[t_new + r]


def pool_sample(u, state, *, start_pos, bb=32):
    db, t_new, _ = u.shape
    return pl.pallas_call(
        functools.partial(_pool_sample_kernel, t_new=t_new, start_pos=start_pos),
        grid=(db // bb,),
        in_specs=[
            pl.BlockSpec((bb, t_new, POOL_WIDTH), lambda i: (i, 0, 0)),
            pl.BlockSpec((bb, POOL_STATE, POOL_WIDTH), lambda i: (i, 0, 0)),
        ],
        out_specs=(
            pl.BlockSpec((bb, t_new, POOL_WIDTH), lambda i: (i, 0, 0)),
            pl.BlockSpec((bb, POOL_STATE, POOL_WIDTH), lambda i: (i, 0, 0)),
        ),
        out_shape=(
            jax.ShapeDtypeStruct((db, t_new, POOL_WIDTH), F32),
            jax.ShapeDtypeStruct((db, POOL_STATE, POOL_WIDTH), F32),
        ),
        compiler_params=_cparams(("arbitrary",)),
        name="pool_sample",
    )(u, state)


def _out_proj_kernel(x_ref, oa_ref, d_ref, wp_ref, ps_ref, wo_ref, gf_ref, wr_ref, br_ref,
                     h2_ref, hn_ref, rt_ref):
    dd = d_ref[...].astype(BF16)
    acc = _dot(oa_ref[...].astype(BF16), wo_ref[0:ATTN_WIDTH, :])
    for g in range(len(POOL_WINDOWS)):
        c0, c1 = g * POOL_GROUP_W, (g + 1) * POOL_GROUP_W
        yg = _dot(dd[:, c0:c1], wp_ref[g].astype(BF16)) * ps_ref[:, c0:c1]
        acc = acc + _dot(yg.astype(BF16), wo_ref[ATTN_WIDTH + c0:ATTN_WIDTH + c1, :])
    h2 = x_ref[...] + acc
    h2_ref[...] = h2
    ms = jnp.mean(h2 * h2, axis=-1, keepdims=True)
    hn = h2 * lax.rsqrt(ms + EPS) * gf_ref[...]
    bits = lax.bitcast_convert_type(hn.astype(BF16).astype(F32), jnp.uint32)
    hn_ref[...] = bits[:, :D_MODEL // 2] | (bits[:, D_MODEL // 2:] >> 16)

    logits = jnp.dot(hn, wr_ref[...], precision=lax.Precision.HIGHEST,
                     preferred_element_type=F32) + br_ref[...]
    lane = lax.broadcasted_iota(jnp.int32, logits.shape, 1).astype(F32)
    far = jnp.float32(LANES)
    is_g = lane < N_EXPERT_GROUPS
    gl = jnp.where(is_g, logits, -jnp.inf)
    g_max = jnp.max(gl, axis=-1, keepdims=True)
    g_top = jnp.min(jnp.where(gl == g_max, lane, far), axis=-1, keepdims=True)
    g_p = 1.0 / jnp.sum(jnp.where(is_g, jnp.exp(gl - g_max), 0.0), axis=-1, keepdims=True)
    lo = N_EXPERT_GROUPS + g_top * EXPERTS_PER_GROUP
    in_grp = jnp.logical_and(lane >= lo, lane < lo + EXPERTS_PER_GROUP)
    el = jnp.where(in_grp, logits, -jnp.inf)
    e1 = jnp.max(el, axis=-1, keepdims=True)
    i1 = jnp.min(jnp.where(el == e1, lane, far), axis=-1, keepdims=True)
    el2 = jnp.where(lane == i1, -jnp.inf, el)
    e2 = jnp.max(el2, axis=-1, keepdims=True)
    i2 = jnp.min(jnp.where(el2 == e2, lane, far), axis=-1, keepdims=True)
    ex2 = jnp.exp(e2 - e1)
    den = 1.0 + ex2
    w1 = (1.0 / den) * g_p
    w2 = (ex2 / den) * g_p
    rt_ref[...] = jnp.where(lane == 0.0, i1 - N_EXPERT_GROUPS,
                            jnp.where(lane == 1.0, i2 - N_EXPERT_GROUPS,
                                      jnp.where(lane == 2.0, w1, jnp.where(lane == 3.0, w2, 0.0))))


def out_proj(x, o_attn, d, w_pool, pool_scale, w_out_bf, g_ffn, w_router, b_router, *, tm):
    n = x.shape[0]
    row = lambda i: (i, 0)
    const = lambda i: (0, 0)
    return pl.pallas_call(
        _out_proj_kernel,
        grid=(n // tm,),
        in_specs=[
            pl.BlockSpec((tm, D_MODEL), row),
            pl.BlockSpec((tm, ATTN_WIDTH), row),
            pl.BlockSpec((tm, POOL_WIDTH), row),
            pl.BlockSpec(w_pool.shape, lambda i: (0, 0, 0)),
            pl.BlockSpec((1, POOL_WIDTH), const),
            pl.BlockSpec((D_MODEL, D_MODEL), const),
            pl.BlockSpec((1, D_MODEL), const),
            pl.BlockSpec((D_MODEL, LANES), const),
            pl.BlockSpec((1, LANES), const),
        ],
        out_specs=(
            pl.BlockSpec((tm, D_MODEL), row),
            pl.BlockSpec((tm, D_MODEL // 2), row),
            pl.BlockSpec((tm, LANES), row),
        ),
        out_shape=(
            jax.ShapeDtypeStruct((n, D_MODEL), F32),
            jax.ShapeDtypeStruct((n, D_MODEL // 2), jnp.uint32),
            jax.ShapeDtypeStruct((n, LANES), F32),
        ),
        compiler_params=_cparams(("arbitrary",)),
        name="out_proj",
    )(x, o_attn, d, w_pool, pool_scale, w_out_bf, g_ffn, w_router, b_router)


MOE_TILE = 256


def _route_kernel(rt_ref, pos_ref, meta_ref, *, n_tiles):
    lane = lax.broadcasted_iota(jnp.int32, (MOE_TILE, LANES), 1).astype(F32)
    r_i = lax.broadcasted_iota(jnp.int32, (MOE_TILE, MOE_TILE), 0)
    c_i = lax.broadcasted_iota(jnp.int32, (MOE_TILE, MOE_TILE), 1)
    tri = jnp.where(c_i < r_i, 1.0, 0.0).astype(BF16)

    def one_hot(t):
        rt = rt_ref[pl.ds(pl.multiple_of(t * MOE_TILE, MOE_TILE), MOE_TILE), :]
        e1, e2 = rt[:, 0:1], rt[:, 1:2]
        return e1, e2, jnp.where(jnp.logical_or(lane == e1, lane == e2), 1.0, 0.0)

    def count(t, cnt):
        return cnt + jnp.sum(one_hot(t)[2], axis=0, keepdims=True)

    cnt = lax.fori_loop(0, n_tiles, count, jnp.zeros((1, LANES), F32))
    tiles_per = jnp.floor((cnt + (MOE_TILE - 1)) * (1.0 / MOE_TILE))
    e_r = lax.broadcasted_iota(jnp.int32, (LANES, LANES), 0)
    e_c = lax.broadcasted_iota(jnp.int32, (LANES, LANES), 1)
    upper = jnp.where(e_r < e_c, 1.0, 0.0).astype(BF16)
    off_tiles = _dot(jnp.broadcast_to(tiles_per, (8, LANES)).astype(BF16), upper)[0:1]
    base = off_tiles * MOE_TILE

    def place(t, run):
        e1, e2, oh = one_hot(t)
        dest = base + run + _dot(tri, oh.astype(BF16))
        p1 = jnp.sum(jnp.where(lane == e1, dest, 0.0), axis=1, keepdims=True)
        p2 = jnp.sum(jnp.where(lane == e2, dest, 0.0), axis=1, keepdims=True)
        pos = jnp.where(lane == 0.0, p1, jnp.where(lane == 1.0, p2, 0.0))
        pos_ref[pl.ds(pl.multiple_of(t * MOE_TILE, MOE_TILE), MOE_TILE), :] = pos.astype(jnp.int32)
        return run + jnp.sum(oh, axis=0, keepdims=True)

    lax.fori_loop(0, n_tiles, place, jnp.zeros((1, LANES), F32))
    row = lax.broadcasted_iota(jnp.int32, (8, LANES), 0)
    meta_ref[...] = jnp.where(row == 0, tiles_per, jnp.where(row == 1, off_tiles, 0.0))


def route(rt):
    n = rt.shape[0]
    assert n % MOE_TILE == 0
    return pl.pallas_call(
        functools.partial(_route_kernel, n_tiles=n // MOE_TILE),
        out_shape=(
            jax.ShapeDtypeStruct((n, LANES), jnp.int32),
            jax.ShapeDtypeStruct((8, LANES), F32),
        ),
        compiler_params=pltpu.CompilerParams(vmem_limit_bytes=VMEM_LIMIT),
        name="route",
    )(rt)


def _dispatch_kernel(p1_ref, p2_ref, x_ref, xs_in_ref, xs_ref, sem, *, tm):
    del xs_in_ref
    t0 = pl.program_id(0) * tm

    def issue(r, c):
        src = x_ref.at[pl.ds(r, 1)]
        pltpu.make_async_copy(src, xs_ref.at[pl.ds(p1_ref[t0 + r], 1)], sem).start()
        pltpu.make_async_copy(src, xs_ref.at[pl.ds(p2_ref[t0 + r], 1)], sem).start()
        return c

    lax.fori_loop(0, tm, issue, 0)

    def drain(r, c):
        pltpu.make_async_copy(x_ref.at[pl.ds(0, 1)], xs_ref.at[pl.ds(0, 1)], sem).wait()
        pltpu.make_async_copy(x_ref.at[pl.ds(0, 1)], xs_ref.at[pl.ds(0, 1)], sem).wait()
        return c

    lax.fori_loop(0, tm, drain, 0)


def dispatch(p1, p2, x_packed, xs, *, tm):
    n, w = x_packed.shape
    grid_spec = pltpu.PrefetchScalarGridSpec(
        num_scalar_prefetch=2,
        grid=(n // tm,),
        in_specs=[pl.BlockSpec((tm, w), lambda i, a, b: (i, 0)), pl.BlockSpec(memory_space=pl.ANY)],
        out_specs=pl.BlockSpec(memory_space=pl.ANY),
        scratch_shapes=[pltpu.SemaphoreType.DMA(())],
    )
    return pl.pallas_call(
        functools.partial(_dispatch_kernel, tm=tm),
        grid_spec=grid_spec,
        out_shape=jax.ShapeDtypeStruct(xs.shape, xs.dtype),
        input_output_aliases={3: 0},
        compiler_params=_cparams(("arbitrary",)),
        name="dispatch",
    )(p1, p2, x_packed, xs)


def _experts_kernel(te_ref, ts_ref, nu_ref, x_ref, wg_ref, wu_ref, wd_ref, o_ref):
    @pl.when(pl.program_id(0) < nu_ref[0])
    def _():
        p = x_ref[...]
        hi = lax.bitcast_convert_type(p & jnp.uint32(0xFFFF0000), F32)
        lo = lax.bitcast_convert_type(p << 16, F32)
        x = jnp.concatenate([hi, lo], axis=1).astype(BF16)
        a = _dot(x, wg_ref[0])
        b = _dot(x, wu_ref[0])
        act = (a * (1.0 / (1.0 + jnp.exp(-a)))) * b
        o_ref[...] = _dot(act.astype(BF16), wd_ref[0])

    @pl.when(pl.program_id(0) >= nu_ref[0])
    def _():
        o_ref[...] = jnp.zeros(o_ref.shape, o_ref.dtype)


def experts(tile_expert, tile_src, n_used, xs, wg_bf, wu_bf, wd_bf):
    rows, w = xs.shape
    wspec = lambda shape: pl.BlockSpec((1,) + shape, lambda j, te, ts, nu: (te[j], 0, 0))
    grid_spec = pltpu.PrefetchScalarGridSpec(
        num_scalar_prefetch=3,
        grid=(rows // MOE_TILE,),
        in_specs=[
            pl.BlockSpec((MOE_TILE, w), lambda j, te, ts, nu: (ts[j], 0)),
            wspec((D_MODEL, D_EXPERT)), wspec((D_MODEL, D_EXPERT)), wspec((D_EXPERT, D_MODEL)),
        ],
        out_specs=pl.BlockSpec((MOE_TILE, D_MODEL), lambda j, te, ts, nu: (j, 0)),
    )
    return pl.pallas_call(
        _experts_kernel,
        grid_spec=grid_spec,
        out_shape=jax.ShapeDtypeStruct((rows, D_MODEL), F32),
        compiler_params=_cparams(("arbitrary",)),
        name="experts",
    )(tile_expert, tile_src, n_used, xs, wg_bf, wu_bf, wd_bf)


def _combine_kernel(p1_ref, p2_ref, h2_ref, rt_ref, os_ref, y_ref, buf_ref, sem, *, tm):
    t0 = pl.program_id(0) * tm

    def issue(r, c):
        pltpu.make_async_copy(os_ref.at[pl.ds(p1_ref[t0 + r], 1)], buf_ref.at[0, pl.ds(r, 1)], sem).start()
        pltpu.make_async_copy(os_ref.at[pl.ds(p2_ref[t0 + r], 1)], buf_ref.at[1, pl.ds(r, 1)], sem).start()
        return c

    lax.fori_loop(0, tm, issue, 0)

    def drain(r, c):
        pltpu.make_async_copy(os_ref.at[pl.ds(0, 1)], buf_ref.at[0, pl.ds(0, 1)], sem).wait()
        pltpu.make_async_copy(os_ref.at[pl.ds(0, 1)], buf_ref.at[1, pl.ds(0, 1)], sem).wait()
        return c

    lax.fori_loop(0, tm, drain, 0)
    rt = rt_ref[...]
    y_ref[...] = h2_ref[...] + rt[:, 2:3] * buf_ref[0] + rt[:, 3:4] * buf_ref[1]


def combine(p1, p2, h2, rt, os, *, tm):
    n = h2.shape[0]
    row = lambda i, a, b: (i, 0)
    grid_spec = pltpu.PrefetchScalarGridSpec(
        num_scalar_prefetch=2,
        grid=(n // tm,),
        in_specs=[pl.BlockSpec((tm, D_MODEL), row), pl.BlockSpec((tm, LANES), row),
                  pl.BlockSpec(memory_space=pl.ANY)],
        out_specs=pl.BlockSpec((tm, D_MODEL), row),
        scratch_shapes=[pltpu.VMEM((2, tm, D_MODEL), F32), pltpu.SemaphoreType.DMA(())],
    )
    return pl.pallas_call(
        functools.partial(_combine_kernel, tm=tm),
        grid_spec=grid_spec,
        out_shape=jax.ShapeDtypeStruct((n, D_MODEL), F32),
        compiler_params=_cparams(("arbitrary",)),
        name="combine",
    )(p1, p2, h2, rt, os)


def kernel(x_prompt, x_sample, cache_k, cache_v, state_pool, page_table, g_mix, w_in, g_q, g_k, w_pool, pool_scale, w_out, g_ffn, w_group_router, b_group_router, w_expert_router, b_expert_router, w_gate, w_up, w_down):
    B, S, _ = x_prompt.shape
    DB, T, _ = x_sample.shape
    depth = w_in.shape[0]
    assert B == 1 and depth == 1
    past_len = page_table.shape[1] * cache_k.shape[2]
    l = 0

    w_in_bf = w_in[l].astype(BF16)
    w_out_bf = w_out[l].astype(BF16)
    wg_bf, wu_bf, wd_bf = w_gate[l].astype(BF16), w_up[l].astype(BF16), w_down[l].astype(BF16)
    gm, gq, gk, gf = g_mix[l][None], g_q[l][None], g_k[l][None], g_ffn[l][None]
    ps = pool_scale[l][None]
    n_r = N_EXPERT_GROUPS + N_EXPERTS
    w_router = jnp.concatenate([w_group_router[l], w_expert_router[l].reshape(D_MODEL, N_EXPERTS)], axis=1)
    w_router = jnp.pad(w_router, ((0, 0), (0, LANES - n_r)))
    b_router = jnp.concatenate([b_group_router[l], b_expert_router[l].reshape(N_EXPERTS)])
    b_router = jnp.pad(b_router, (0, LANES - n_r))[None]

    def mixer_tail(x2d, o_attn, d):
        return out_proj(x2d, o_attn, d, w_pool[l], ps, w_out_bf, gf, w_router, b_router, tm=512)

    xp = x_prompt.reshape(S, D_MODEL)
    q_p, k_p, kb_p, km_p, v_p, vb_p, u_p = in_proj(xp, gm, w_in_bf, gq, gk, tm=512)
    o_p = moba_prompt(q_p, kb_p, vb_p, km_p.reshape(S // MOBA_BLOCK, ATTN_WIDTH))
    d_p = pool_prompt(u_p, tm=512)
    h2_p, hn_p, rt_p = mixer_tail(xp, o_p, d_p)

    n_s = DB * T
    xs = x_sample.reshape(n_s, D_MODEL)
    q_s, k_s, _, _, v_s, _, u_s = in_proj(xs, gm, w_in_bf, gq, gk, tm=n_s)
    r3 = lambda a: a.reshape(DB, T, ATTN_WIDTH)
    o_s = moba_sample(r3(q_s), r3(k_s), r3(v_s), cache_k, cache_v, page_table + l * cache_k.shape[1])
    d_s, pool_s = pool_sample(u_s.reshape(DB, T, POOL_WIDTH), state_pool[l], start_pos=past_len)
    h2_s, hn_s, rt_s = mixer_tail(xs, o_s.reshape(n_s, ATTN_WIDTH), d_s.reshape(n_s, POOL_WIDTH))

    pos, meta = route(jnp.concatenate([rt_p, rt_s], axis=0))
    p1, p2 = pos[:, 0], pos[:, 1]
    max_tiles = 2 * (S + n_s) // MOE_TILE + N_EXPERTS
    ends = (meta[0, :N_EXPERTS] + meta[1, :N_EXPERTS]).astype(jnp.int32)
    n_used = ends[N_EXPERTS - 1]
    tile_src = jnp.minimum(jnp.arange(max_tiles, dtype=jnp.int32), n_used - 1)
    tile_expert = jnp.minimum(jnp.sum(tile_src[:, None] >= ends[None, :], axis=1), N_EXPERTS - 1).astype(jnp.int32)
    x_sorted = jnp.zeros((max_tiles * MOE_TILE, D_MODEL // 2), jnp.uint32)
    x_sorted = dispatch(p1[:S], p2[:S], hn_p, x_sorted, tm=MOE_TILE)
    x_sorted = dispatch(p1[S:], p2[S:], hn_s, x_sorted, tm=MOE_TILE)
    o_sorted = experts(tile_expert, tile_src, n_used[None], x_sorted, wg_bf, wu_bf, wd_bf)
    y_prompt = combine(p1[:S], p2[:S], h2_p, rt_p, o_sorted, tm=MOE_TILE).reshape(B, S, D_MODEL)
    y_sample = combine(p1[S:], p2[S:], h2_s, rt_s, o_sorted, tm=MOE_TILE).reshape(DB, T, D_MODEL)

    hd = (N_HEADS, HEAD_DIM)
    return (
        y_prompt,
        y_sample,
        k_p.reshape(1, B, S, *hd),
        v_p.reshape(1, B, S, *hd),
        u_p[S - POOL_STATE:].reshape(1, B, POOL_STATE, POOL_WIDTH),
        k_s.reshape(1, DB, T, *hd),
        v_s.reshape(1, DB, T, *hd),
        pool_s.reshape(1, DB, POOL_STATE, POOL_WIDTH),
    )
```

```python
import functools

import jax
import jax.numpy as jnp
from jax import lax
from jax.experimental import pallas as pl
from jax.experimental.pallas import tpu as pltpu

D_MODEL = 2048
ATTN_WIDTH = 1024
POOL_WIDTH = 1024
HEAD_DIM = 128
N_HEADS = 8
POOL_WINDOWS = (2, 4, 8, 16)
POOL_GROUP_W = 256
POOL_STATE = 15
MOBA_BLOCK = 256
MOBA_TOPK = 3
N_EXPERT_GROUPS = 4
EXPERTS_PER_GROUP = 4
N_EXPERTS = 16
D_EXPERT = 768
EPS = 1e-6
NEG = -1e30
LANES = 128
VMEM_LIMIT = 56 * 1024 * 1024

BF16 = jnp.bfloat16
F32 = jnp.float32


def _cparams(sem):
    return pltpu.CompilerParams(dimension_semantics=sem, vmem_limit_bytes=VMEM_LIMIT)


def _dot(a, b):
    return jnp.dot(a, b, preferred_element_type=F32)


def _dot_nt(a, b, precision=None):
    return lax.dot_general(a, b, (((1,), (1,)), ((), ())), precision=precision,
                           preferred_element_type=F32)


def _top_mask(g, ids, k, axis=-1):
    sel = jnp.zeros(g.shape, F32)
    for _ in range(k):
        m = jnp.max(g, axis=axis, keepdims=True)
        idx = jnp.min(jnp.where(g == m, ids, jnp.float32(g.shape[axis])), axis=axis, keepdims=True)
        pick = ids == idx
        sel = jnp.where(pick, 1.0, sel)
        g = jnp.where(pick, -jnp.inf, g)
    return sel


def _in_proj_kernel(x_ref, gmix_ref, w_ref, gq_ref, gk_ref,
                    q_ref, k_ref, kb_ref, km_ref, v_ref, vb_ref, u_ref, hb_ref, *, tm):
    xf = x_ref[...]
    ms = jnp.mean(xf * xf, axis=-1, keepdims=True)
    hb_ref[...] = (xf * lax.rsqrt(ms + EPS) * gmix_ref[...]).astype(BF16)

    def head_norm(z, out_ref, g_ref):
        for h in range(N_HEADS):
            zh = z[:, h * HEAD_DIM:(h + 1) * HEAD_DIM]
            ms = jnp.mean(zh * zh, axis=-1, keepdims=True)
            out_ref[:, h * HEAD_DIM:(h + 1) * HEAD_DIM] = zh * lax.rsqrt(ms + EPS) * g_ref[...]

    head_norm(_dot(hb_ref[...], w_ref[:, 0:ATTN_WIDTH]), q_ref, gq_ref)
    head_norm(_dot(hb_ref[...], w_ref[:, ATTN_WIDTH:2 * ATTN_WIDTH]), k_ref, gk_ref)
    kb_ref[...] = k_ref[...].astype(BF16)
    for g in range(tm // MOBA_BLOCK):
        blk = k_ref[g * MOBA_BLOCK:(g + 1) * MOBA_BLOCK, :]
        km_ref[g] = jnp.mean(blk, axis=0, keepdims=True)
    zv = _dot(hb_ref[...], w_ref[:, 2 * ATTN_WIDTH:3 * ATTN_WIDTH])
    v_ref[...] = zv
    vb_ref[...] = zv.astype(BF16)
    u_ref[...] = _dot(hb_ref[...], w_ref[:, 3 * ATTN_WIDTH:])


def in_proj(x, g_mix, w_in_bf, g_q, g_k, *, tm):
    n = x.shape[0]
    wide = lambda i: (i, 0)
    const = lambda i: (0, 0)
    out_shape = (
        jax.ShapeDtypeStruct((n, ATTN_WIDTH), F32),
        jax.ShapeDtypeStruct((n, ATTN_WIDTH), F32),
        jax.ShapeDtypeStruct((n, ATTN_WIDTH), BF16),
        jax.ShapeDtypeStruct((n // MOBA_BLOCK, 1, ATTN_WIDTH), F32),
        jax.ShapeDtypeStruct((n, ATTN_WIDTH), F32),
        jax.ShapeDtypeStruct((n, ATTN_WIDTH), BF16),
        jax.ShapeDtypeStruct((n, POOL_WIDTH), F32),
    )
    blk = pl.BlockSpec((tm, ATTN_WIDTH), wide)
    return pl.pallas_call(
        functools.partial(_in_proj_kernel, tm=tm),
        grid=(n // tm,),
        in_specs=[
            pl.BlockSpec((tm, D_MODEL), wide),
            pl.BlockSpec((1, D_MODEL), const),
            pl.BlockSpec(w_in_bf.shape, const, pipeline_mode=pl.Buffered(1)),
            pl.BlockSpec((1, HEAD_DIM), const),
            pl.BlockSpec((1, HEAD_DIM), const),
        ],
        out_specs=(blk, blk, blk,
                   pl.BlockSpec((tm // MOBA_BLOCK, 1, ATTN_WIDTH), lambda i: (i, 0, 0)),
                   blk, blk, blk),
        out_shape=out_shape,
        scratch_shapes=[pltpu.VMEM((tm, D_MODEL), BF16)],
        compiler_params=_cparams(("arbitrary",)),
        name="in_proj",
    )(x, g_mix, w_in_bf, g_q, g_k)


LOG2E = 1.4426950408889634
ATT_HEADS = 2
ATT_GROUP = 4


def _moba_prompt_kernel(q_ref, k_ref, vt_ref, km_ref, o_ref, qs_ref, bias_ref, m_ref, l_ref, acc_ref,
                        *, nb, hp, grp):
    own = pl.program_id(1)
    base = pl.multiple_of(own * MOBA_BLOCK, MOBA_BLOCK)
    blk_id = lax.broadcasted_iota(jnp.int32, (nb, MOBA_BLOCK), 0).astype(F32)
    valid = blk_id < own.astype(F32)
    key_i = lax.broadcasted_iota(jnp.int32, (MOBA_BLOCK, MOBA_BLOCK), 0)
    qry_i = lax.broadcasted_iota(jnp.int32, (MOBA_BLOCK, MOBA_BLOCK), 1)

    for hh in range(hp):
        cols = slice(hh * HEAD_DIM, (hh + 1) * HEAD_DIM)
        q = q_ref[:, cols]
        gate = _dot_nt(km_ref[:, cols], q, precision=lax.Precision.HIGHEST)
        sel = _top_mask(jnp.where(valid, gate, NEG), blk_id, MOBA_TOPK, axis=0)
        bias = jnp.where(jnp.logical_and(sel > 0.5, valid), 0.0, NEG)
        for n in range(nb):
            bias_ref[hh, n] = bias[n:n + 1, :]
        qs = (q * (HEAD_DIM ** -0.5 * LOG2E)).astype(BF16)
        qs_ref[hh] = qs
        s = _dot_nt(k_ref[pl.ds(base, MOBA_BLOCK), cols], qs)
        s = jnp.where(key_i <= qry_i, s, NEG)
        m0 = jnp.max(s, axis=0, keepdims=True)
        p = jnp.exp2(s - m0)
        m_ref[hh] = m0
        l_ref[hh] = jnp.sum(p, axis=0, keepdims=True)
        acc_ref[hh] = _dot(vt_ref[own, cols, :], p.astype(BF16))

    def body(i, carry):
        for hh in range(hp):
            cols = slice(hh * HEAD_DIM, (hh + 1) * HEAD_DIM)
            qs = qs_ref[hh]
            m_prev = m_ref[hh]
            m_new = m_prev
            scores = []
            for gg in range(grp):
                n = i * grp + gg
                off = pl.multiple_of(n * MOBA_BLOCK, MOBA_BLOCK)
                s = _dot_nt(k_ref[pl.ds(off, MOBA_BLOCK), cols], qs) + bias_ref[hh, n]
                scores.append(s)
                m_new = jnp.maximum(m_new, jnp.max(s, axis=0, keepdims=True))
            alpha = jnp.exp2(m_prev - m_new)
            l_new = alpha * l_ref[hh]
            acc = alpha * acc_ref[hh]
            for gg in range(grp):
                p = jnp.exp2(scores[gg] - m_new)
                l_new = l_new + jnp.sum(p, axis=0, keepdims=True)
                acc = acc + _dot(vt_ref[i * grp + gg, cols, :], p.astype(BF16))
            m_ref[hh] = m_new
            l_ref[hh] = l_new
            acc_ref[hh] = acc
        return carry

    lax.fori_loop(0, (own + grp - 1) // grp, body, 0)
    for hh in range(hp):
        o = acc_ref[hh] * (1.0 / l_ref[hh])
        o_ref[:, hh * HEAD_DIM:(hh + 1) * HEAD_DIM] = o.T.astype(o_ref.dtype)


def moba_prompt(q, k_bf, v_bf, kmean):
    s = q.shape[0]
    nb = s // MOBA_BLOCK
    hp = ATT_HEADS
    grp = ATT_GROUP if nb % ATT_GROUP == 0 else 1
    wide = hp * HEAD_DIM
    vt = v_bf.reshape(nb, MOBA_BLOCK, ATTN_WIDTH).transpose(0, 2, 1)
    return pl.pallas_call(
        functools.partial(_moba_prompt_kernel, nb=nb, hp=hp, grp=grp),
        grid=(N_HEADS // hp, nb),
        in_specs=[
            pl.BlockSpec((MOBA_BLOCK, wide), lambda h, j: (j, h)),
            pl.BlockSpec((s, wide), lambda h, j: (0, h)),
            pl.BlockSpec((nb, wide, MOBA_BLOCK), lambda h, j: (0, h, 0)),
            pl.BlockSpec((nb, wide), lambda h, j: (0, h)),
        ],
        out_specs=pl.BlockSpec((MOBA_BLOCK, wide), lambda h, j: (j, h)),
        out_shape=jax.ShapeDtypeStruct((s, ATTN_WIDTH), BF16),
        scratch_shapes=[
            pltpu.VMEM((hp, MOBA_BLOCK, HEAD_DIM), BF16),
            pltpu.VMEM((hp, nb, 1, MOBA_BLOCK), F32),
            pltpu.VMEM((hp, 1, MOBA_BLOCK), F32),
            pltpu.VMEM((hp, 1, MOBA_BLOCK), F32),
            pltpu.VMEM((hp, HEAD_DIM, MOBA_BLOCK), F32),
        ],
        compiler_params=_cparams(("arbitrary", "arbitrary")),
        name="moba_prompt",
    )(q, k_bf, vt, kmean)


def _page_bf16(ref):
    page = ref.shape[0] // N_HEADS
    heads = [ref[pl.ds(h, page, stride=N_HEADS), :] for h in range(N_HEADS)]
    return jnp.concatenate(heads, axis=1).astype(BF16)


def _moba_sample_kernel(pt_ref, q_ref, kn_ref, vn_ref, *refs, n_pages, t_new):
    k_pages = refs[:n_pages]
    v_pages = refs[n_pages:2 * n_pages]
    o_ref = refs[2 * n_pages]
    s_ref = refs[2 * n_pages + 1]
    page = k_pages[0].shape[0] // N_HEADS
    rows = t_new * N_HEADS
    n_blocks = n_pages * page // MOBA_BLOCK
    per_blk = MOBA_BLOCK // page

    qrep = jnp.concatenate(
        [jnp.broadcast_to(q_ref[0, t:t + 1, :], (N_HEADS, ATTN_WIDTH)) for t in range(t_new)], axis=0)
    r_i = lax.broadcasted_iota(jnp.int32, (rows, ATTN_WIDTH), 0)
    c_i = lax.broadcasted_iota(jnp.int32, (rows, ATTN_WIDTH), 1)
    diag = (c_i // HEAD_DIM) == (r_i % N_HEADS)
    qbd = jnp.where(diag, qrep * (HEAD_DIM ** -0.5), 0.0).astype(BF16)

    gcol = lax.broadcasted_iota(jnp.int32, (rows, LANES), 1).astype(F32)
    gate = jnp.full((rows, LANES), NEG, F32)
    for n in range(n_blocks):
        tot = jnp.zeros((rows, 1), F32)
        for pp in range(per_blk):
            p_i = n * per_blk + pp
            sp = _dot_nt(qbd, _page_bf16(k_pages[p_i]))
            s_ref[:, p_i * page:(p_i + 1) * page] = sp
            tot = tot + jnp.sum(sp, axis=-1, keepdims=True)
        gate = jnp.where(gcol == float(n), tot * (1.0 / MOBA_BLOCK), gate)
    sel = _top_mask(gate, gcol, min(MOBA_TOPK, n_blocks))

    s_own = _dot_nt(qbd, kn_ref[0].astype(BF16))
    oc = lax.broadcasted_iota(jnp.int32, s_own.shape, 1)
    orow = lax.broadcasted_iota(jnp.int32, s_own.shape, 0)
    s_own = jnp.where(oc <= orow // N_HEADS, s_own, NEG)

    m = jnp.max(s_own, axis=-1, keepdims=True)
    for n in range(n_blocks):
        sb = s_ref[:, n * MOBA_BLOCK:(n + 1) * MOBA_BLOCK]
        sb = jnp.where(sel[:, n:n + 1] > 0.5, sb, NEG)
        m = jnp.maximum(m, jnp.max(sb, axis=-1, keepdims=True))

    p_own = jnp.exp(s_own - m)
    l = jnp.sum(p_own, axis=-1, keepdims=True)
    acc = _dot(p_own.astype(BF16), vn_ref[0].astype(BF16))
    for p_i in range(n_pages):
        n = p_i // per_blk
        sp = s_ref[:, p_i * page:(p_i + 1) * page]
        pp = jnp.where(sel[:, n:n + 1] > 0.5, jnp.exp(sp - m), 0.0)
        l = l + jnp.sum(pp, axis=-1, keepdims=True)
        acc = acc + _dot(pp.astype(BF16), _page_bf16(v_pages[p_i]))
    o = jnp.where(diag, acc * (1.0 / l), 0.0)
    for t in range(t_new):
        o_ref[0, t:t + 1, :] = jnp.sum(o[t * N_HEADS:(t + 1) * N_HEADS, :], axis=0, keepdims=True)


def moba_sample(q, k_new, v_new, cache_k, cache_v, page_table):
    db, t_new, _ = q.shape
    n_pages = page_table.shape[1]
    page = cache_k.shape[-3]
    ck = cache_k.reshape(-1, page * N_HEADS, HEAD_DIM)
    cv = cache_v.reshape(-1, page * N_HEADS, HEAD_DIM)
    tok = pl.BlockSpec((1, t_new, ATTN_WIDTH), lambda b, pt: (b, 0, 0))

    def page_spec(p):
        return pl.BlockSpec((None, page * N_HEADS, HEAD_DIM), lambda b, pt: (pt[b, p], 0, 0))

    grid_spec = pltpu.PrefetchScalarGridSpec(
        num_scalar_prefetch=1,
        grid=(db,),
        in_specs=[tok, tok, tok] + [page_spec(p) for p in range(n_pages)] * 2,
        out_specs=tok,
        scratch_shapes=[pltpu.VMEM((t_new * N_HEADS, n_pages * page), F32)],
    )
    return pl.pallas_call(
        functools.partial(_moba_sample_kernel, n_pages=n_pages, t_new=t_new),
        grid_spec=grid_spec,
        out_shape=jax.ShapeDtypeStruct((db, t_new, ATTN_WIDTH), F32),
        compiler_params=_cparams(("arbitrary",)),
        name="moba_sample",
    )(page_table, q, k_new, v_new, *([ck] * n_pages), *([cv] * n_pages))


def _pool_prompt_kernel(u_ref, prev_ref, d_ref, ext_ref, *, tm):
    i = pl.program_id(0)
    pad = prev_ref.shape[0]
    prev = jnp.where(i > 0, prev_ref[...], 0.0)
    ext_ref[0:pad, :] = prev
    ext_ref[pad:pad + tm, :] = u_ref[...]
    pos = (i * tm + lax.broadcasted_iota(jnp.int32, (tm, 1), 0)).astype(F32)
    for g, w in enumerate(POOL_WINDOWS):
        c0, c1 = g * POOL_GROUP_W, (g + 1) * POOL_GROUP_W
        wsum = ext_ref[pad:pad + tm, c0:c1]
        for jj in range(1, w):
            wsum = wsum + ext_ref[pad - jj:pad - jj + tm, c0:c1]
        inv = 1.0 / jnp.minimum(jnp.float32(w), pos + 1.0)
        d_ref[:, c0:c1] = (wsum * inv - u_ref[:, c0:c1]).astype(d_ref.dtype)


def pool_prompt(u, *, tm):
    n = u.shape[0]
    pad = 16
    return pl.pallas_call(
        functools.partial(_pool_prompt_kernel, tm=tm),
        grid=(n // tm,),
        in_specs=[
            pl.BlockSpec((tm, POOL_WIDTH), lambda i: (i, 0)),
            pl.BlockSpec((pad, POOL_WIDTH), lambda i: (jnp.maximum(i * (tm // pad) - 1, 0), 0)),
        ],
        out_specs=pl.BlockSpec((tm, POOL_WIDTH), lambda i: (i, 0)),
        out_shape=jax.ShapeDtypeStruct((n, POOL_WIDTH), BF16),
        scratch_shapes=[pltpu.VMEM((tm + pad, POOL_WIDTH), F32)],
        compiler_params=_cparams(("arbitrary",)),
        name="pool_prompt",
    )(u, u)


def _pool_sample_kernel(u_ref, st_ref, d_ref, new_ref, *, t_new, start_pos):
    ext = [st_ref[:, r, :] for r in range(POOL_STATE)] + [u_ref[:, t, :] for t in range(t_new)]
    for t in range(t_new):
        e = POOL_STATE + t
        parts = []
        for g, w in enumerate(POOL_WINDOWS):
            c0, c1 = g * POOL_GROUP_W, (g + 1) * POOL_GROUP_W
            wsum = ext[e][:, c0:c1]
            for jj in range(1, w):
                wsum = wsum + ext[e - jj][:, c0:c1]
            count = min(float(w), float(start_pos + t) + 1.0)
            parts.append(wsum * (1.0 / count) - ext[e][:, c0:c1])
        d_ref[:, t, :] = jnp.concatenate(parts, axis=-1).astype(d_ref.dtype)
    for r in range(POOL_STATE):
        new_ref[:, r, :] = ext[t_new + r]


def pool_sample(u, state, *, start_pos, bb=32):
    db, t_new, _ = u.shape
    return pl.pallas_call(
        functools.partial(_pool_sample_kernel, t_new=t_new, start_pos=start_pos),
        grid=(db // bb,),
        in_specs=[
            pl.BlockSpec((bb, t_new, POOL_WIDTH), lambda i: (i, 0, 0)),
            pl.BlockSpec((bb, POOL_STATE, POOL_WIDTH), lambda i: (i, 0, 0)),
        ],
        out_specs=(
            pl.BlockSpec((bb, t_new, POOL_WIDTH), lambda i: (i, 0, 0)),
            pl.BlockSpec((bb, POOL_STATE, POOL_WIDTH), lambda i: (i, 0, 0)),
        ),
        out_shape=(
            jax.ShapeDtypeStruct((db, t_new, POOL_WIDTH), F32),
            jax.ShapeDtypeStruct((db, POOL_STATE, POOL_WIDTH), F32),
        ),
        compiler_params=_cparams(("arbitrary",)),
        name="pool_sample",
    )(u, state)


def _out_proj_kernel(x_ref, oa_ref, d_ref, wp_ref, ps_ref, wo_ref, gf_ref, wr_ref, br_ref,
                     h2_ref, rt_ref, mix_ref):
    mix_ref[:, 0:ATTN_WIDTH] = oa_ref[...].astype(BF16)
    dd = d_ref[...].astype(BF16)
    for g in range(len(POOL_WINDOWS)):
        c0, c1 = g * POOL_GROUP_W, (g + 1) * POOL_GROUP_W
        yg = _dot(dd[:, c0:c1], wp_ref[g].astype(BF16)) * ps_ref[:, c0:c1]
        mix_ref[:, ATTN_WIDTH + c0:ATTN_WIDTH + c1] = yg.astype(BF16)
    h2 = x_ref[...] + _dot(mix_ref[...], wo_ref[...])
    h2_ref[...] = h2
    ms = jnp.mean(h2 * h2, axis=-1, keepdims=True)
    hn = h2 * lax.rsqrt(ms + EPS) * gf_ref[...]

    hn_hi = hn.astype(BF16)
    hn_lo = (hn - hn_hi.astype(F32)).astype(BF16)
    t = _dot(hn_hi, wr_ref[...])
    logits = t[:, :LANES] + t[:, LANES:] + _dot(hn_lo, wr_ref[:, :LANES]) + br_ref[...]
    lane = lax.broadcasted_iota(jnp.int32, logits.shape, 1).astype(F32)
    far = jnp.float32(LANES)
    is_g = lane < N_EXPERT_GROUPS
    gl = jnp.where(is_g, logits, -jnp.inf)
    g_max = jnp.max(gl, axis=-1, keepdims=True)
    g_top = jnp.min(jnp.where(gl == g_max, lane, far), axis=-1, keepdims=True)
    g_p = 1.0 / jnp.sum(jnp.where(is_g, jnp.exp(gl - g_max), 0.0), axis=-1, keepdims=True)
    lo = N_EXPERT_GROUPS + g_top * EXPERTS_PER_GROUP
    in_grp = jnp.logical_and(lane >= lo, lane < lo + EXPERTS_PER_GROUP)
    el = jnp.where(in_grp, logits, -jnp.inf)
    e1 = jnp.max(el, axis=-1, keepdims=True)
    i1 = jnp.min(jnp.where(el == e1, lane, far), axis=-1, keepdims=True)
    el2 = jnp.where(lane == i1, -jnp.inf, el)
    e2 = jnp.max(el2, axis=-1, keepdims=True)
    i2 = jnp.min(jnp.where(el2 == e2, lane, far), axis=-1, keepdims=True)
    ex2 = jnp.exp(e2 - e1)
    den = 1.0 + ex2
    w1 = (1.0 / den) * g_p
    w2 = (ex2 / den) * g_p
    rt_ref[...] = jnp.where(lane == 0.0, i1 - N_EXPERT_GROUPS,
                            jnp.where(lane == 1.0, i2 - N_EXPERT_GROUPS,
                                      jnp.where(lane == 2.0, w1, jnp.where(lane == 3.0, w2, 0.0))))


def out_proj(x, o_attn, d, w_pool, pool_scale, w_out_bf, g_ffn, w_router, b_router, *, tm):
    n = x.shape[0]
    row = lambda i: (i, 0)
    const = lambda i: (0, 0)
    return pl.pallas_call(
        _out_proj_kernel,
        grid=(n // tm,),
        in_specs=[
            pl.BlockSpec((tm, D_MODEL), row),
            pl.BlockSpec((tm, ATTN_WIDTH), row),
            pl.BlockSpec((tm, POOL_WIDTH), row),
            pl.BlockSpec(w_pool.shape, lambda i: (0, 0, 0)),
            pl.BlockSpec((1, POOL_WIDTH), const),
            pl.BlockSpec((D_MODEL, D_MODEL), const),
            pl.BlockSpec((1, D_MODEL), const),
            pl.BlockSpec((D_MODEL, 2 * LANES), const),
            pl.BlockSpec((1, LANES), const),
        ],
        out_specs=(
            pl.BlockSpec((tm, D_MODEL), row),
            pl.BlockSpec((tm, LANES), row),
        ),
        out_shape=(
            jax.ShapeDtypeStruct((n, D_MODEL), F32),
            jax.ShapeDtypeStruct((n, LANES), F32),
        ),
        scratch_shapes=[pltpu.VMEM((tm, D_MODEL), BF16)],
        compiler_params=_cparams(("arbitrary",)),
        name="out_proj",
    )(x, o_attn, d, w_pool, pool_scale, w_out_bf, g_ffn, w_router, b_router)


MOE_TILE = 256


def _route_kernel(rt_ref, pos_ref, meta_ref, *, n_tiles):
    lane = lax.broadcasted_iota(jnp.int32, (MOE_TILE, LANES), 1).astype(F32)
    r_i = lax.broadcasted_iota(jnp.int32, (MOE_TILE, MOE_TILE), 0)
    c_i = lax.broadcasted_iota(jnp.int32, (MOE_TILE, MOE_TILE), 1)
    tri = jnp.where(c_i < r_i, 1.0, 0.0).astype(BF16)

    def one_hot(t):
        rt = rt_ref[pl.ds(pl.multiple_of(t * MOE_TILE, MOE_TILE), MOE_TILE), :]
        e1, e2 = rt[:, 0:1], rt[:, 1:2]
        return e1, e2, jnp.where(jnp.logical_or(lane == e1, lane == e2), 1.0, 0.0)

    def count(t, cnt):
        return cnt + jnp.sum(one_hot(t)[2], axis=0, keepdims=True)

    cnt = lax.fori_loop(0, n_tiles, count, jnp.zeros((1, LANES), F32))
    tiles_per = jnp.floor((cnt + (MOE_TILE - 1)) * (1.0 / MOE_TILE))
    e_r = lax.broadcasted_iota(jnp.int32, (LANES, LANES), 0)
    e_c = lax.broadcasted_iota(jnp.int32, (LANES, LANES), 1)
    upper = jnp.where(e_r < e_c, 1.0, 0.0).astype(BF16)
    off_tiles = _dot(jnp.broadcast_to(tiles_per, (8, LANES)).astype(BF16), upper)[0:1]
    base = off_tiles * MOE_TILE

    def place(t, run):
        e1, e2, oh = one_hot(t)
        dest = base + run + _dot(tri, oh.astype(BF16))
        p1 = jnp.sum(jnp.where(lane == e1, dest, 0.0), axis=1, keepdims=True)
        p2 = jnp.sum(jnp.where(lane == e2, dest, 0.0), axis=1, keepdims=True)
        pos = jnp.where(lane == 0.0, p1, jnp.where(lane == 1.0, p2, 0.0))
        pos_ref[pl.ds(pl.multiple_of(t * MOE_TILE, MOE_TILE), MOE_TILE), :] = pos.astype(jnp.int32)
        return run + jnp.sum(oh, axis=0, keepdims=True)

    lax.fori_loop(0, n_tiles, place, jnp.zeros((1, LANES), F32))
    row = lax.broadcasted_iota(jnp.int32, (8, LANES), 0)
    meta_ref[...] = jnp.where(row == 0, tiles_per, jnp.where(row == 1, off_tiles, 0.0))


def route(rt):
    n = rt.shape[0]
    assert n % MOE_TILE == 0
    return pl.pallas_call(
        functools.partial(_route_kernel, n_tiles=n // MOE_TILE),
        out_shape=(
            jax.ShapeDtypeStruct((n, LANES), jnp.int32),
            jax.ShapeDtypeStruct((8, LANES), F32),
        ),
        compiler_params=pltpu.CompilerParams(vmem_limit_bytes=VMEM_LIMIT),
        name="route",
    )(rt)


ROW_UNROLL = 8


def _dispatch_kernel(p1_ref, p2_ref, zt_ref, xa_ref, xb_ref, xs_ref, zero_ref, sem, zsem, *, tm, tiles_a, n_zero):
    i = pl.program_id(0)

    @pl.when(i == 0)
    def _():
        zero_ref[...] = jnp.zeros(zero_ref.shape, zero_ref.dtype)

        def zcopy(z):
            row0 = pl.multiple_of(zt_ref[z] * tm, tm)
            return pltpu.make_async_copy(zero_ref, xs_ref.at[pl.ds(row0, tm)], zsem)

        for z in range(n_zero):
            @pl.when(zt_ref[z] >= 0)
            def _():
                zcopy(z).start()
        for z in range(n_zero):
            @pl.when(zt_ref[z] >= 0)
            def _():
                zcopy(z).wait()

    def scatter(x_ref):
        t0 = i * tm

        def issue(r, c):
            src = x_ref.at[pl.ds(r, 1)]
            pltpu.make_async_copy(src, xs_ref.at[pl.ds(p1_ref[t0 + r], 1)], sem).start()
            pltpu.make_async_copy(src, xs_ref.at[pl.ds(p2_ref[t0 + r], 1)], sem).start()
            return c

        lax.fori_loop(0, tm, issue, 0, unroll=ROW_UNROLL)
        for _ in range(2):
            pltpu.make_async_copy(x_ref, xs_ref.at[pl.ds(0, tm)], sem).wait()

    @pl.when(i < tiles_a)
    def _():
        scatter(xa_ref)

    @pl.when(i >= tiles_a)
    def _():
        scatter(xb_ref)


def dispatch(p1, p2, zero_tiles, xa, xb, rows, *, tm):
    tiles_a, tiles_b = xa.shape[0] // tm, xb.shape[0] // tm
    grid_spec = pltpu.PrefetchScalarGridSpec(
        num_scalar_prefetch=3,
        grid=(tiles_a + tiles_b,),
        in_specs=[
            pl.BlockSpec((tm, D_MODEL), lambda i, a, b, z: (jnp.minimum(i, tiles_a - 1), 0)),
            pl.BlockSpec((tm, D_MODEL), lambda i, a, b, z: (jnp.maximum(i - tiles_a, 0), 0)),
        ],
        out_specs=pl.BlockSpec(memory_space=pl.ANY),
        scratch_shapes=[pltpu.VMEM((tm, D_MODEL), F32), pltpu.SemaphoreType.DMA(()), pltpu.SemaphoreType.DMA(())],
    )
    return pl.pallas_call(
        functools.partial(_dispatch_kernel, tm=tm, tiles_a=tiles_a, n_zero=zero_tiles.shape[0]),
        grid_spec=grid_spec,
        out_shape=jax.ShapeDtypeStruct((rows, D_MODEL), F32),
        compiler_params=_cparams(("arbitrary",)),
        name="dispatch",
    )(p1, p2, zero_tiles, xa, xb)


def _experts_kernel(te_ref, ts_ref, nu_ref, x_ref, gf_ref, wg_ref, wu_ref, wd_ref, o_ref):
    @pl.when(pl.program_id(0) < nu_ref[0])
    def _():
        h = x_ref[...]
        ms = jnp.mean(h * h, axis=-1, keepdims=True)
        x = (h * lax.rsqrt(ms + EPS) * gf_ref[...]).astype(BF16)
        a = _dot(x, wg_ref[0])
        b = _dot(x, wu_ref[0])
        act = (a * (1.0 / (1.0 + jnp.exp(-a)))) * b
        o_ref[...] = _dot(act.astype(BF16), wd_ref[0])

    @pl.when(pl.program_id(0) >= nu_ref[0])
    def _():
        o_ref[...] = jnp.zeros(o_ref.shape, o_ref.dtype)


def experts(tile_expert, tile_src, n_used, xs, g_ffn, wg_bf, wu_bf, wd_bf):
    rows, w = xs.shape
    wspec = lambda shape: pl.BlockSpec((1,) + shape, lambda j, te, ts, nu: (te[j], 0, 0))
    grid_spec = pltpu.PrefetchScalarGridSpec(
        num_scalar_prefetch=3,
        grid=(rows // MOE_TILE,),
        in_specs=[
            pl.BlockSpec((MOE_TILE, w), lambda j, te, ts, nu: (ts[j], 0)),
            pl.BlockSpec((1, D_MODEL), lambda j, te, ts, nu: (0, 0)),
            wspec((D_MODEL, D_EXPERT)), wspec((D_MODEL, D_EXPERT)), wspec((D_EXPERT, D_MODEL)),
        ],
        out_specs=pl.BlockSpec((MOE_TILE, D_MODEL), lambda j, te, ts, nu: (j, 0)),
    )
    return pl.pallas_call(
        _experts_kernel,
        grid_spec=grid_spec,
        out_shape=jax.ShapeDtypeStruct((rows, D_MODEL), F32),
        compiler_params=_cparams(("arbitrary",)),
        name="experts",
    )(tile_expert, tile_src, n_used, xs, g_ffn, wg_bf, wu_bf, wd_bf)


def _combine_kernel(p1_ref, p2_ref, h2_ref, rt_ref, os_ref, y_ref, buf_ref, sem, *, tm):
    t0 = pl.program_id(0) * tm

    def issue(r, c):
        pltpu.make_async_copy(os_ref.at[pl.ds(p1_ref[t0 + r], 1)], buf_ref.at[0, pl.ds(r, 1)], sem).start()
        pltpu.make_async_copy(os_ref.at[pl.ds(p2_ref[t0 + r], 1)], buf_ref.at[1, pl.ds(r, 1)], sem).start()
        return c

    lax.fori_loop(0, tm, issue, 0, unroll=ROW_UNROLL)
    for s in range(2):
        pltpu.make_async_copy(os_ref.at[pl.ds(0, tm)], buf_ref.at[s], sem).wait()
    rt = rt_ref[...]
    y_ref[...] = h2_ref[...] + rt[:, 2:3] * buf_ref[0] + rt[:, 3:4] * buf_ref[1]


def combine(p1, p2, h2, rt, os, *, tm):
    n = h2.shape[0]
    row = lambda i, a, b: (i, 0)
    grid_spec = pltpu.PrefetchScalarGridSpec(
        num_scalar_prefetch=2,
        grid=(n // tm,),
        in_specs=[pl.BlockSpec((tm, D_MODEL), row), pl.BlockSpec((tm, LANES), row),
                  pl.BlockSpec(memory_space=pl.ANY)],
        out_specs=pl.BlockSpec((tm, D_MODEL), row),
        scratch_shapes=[pltpu.VMEM((2, tm, D_MODEL), F32), pltpu.SemaphoreType.DMA(())],
    )
    return pl.pallas_call(
        functools.partial(_combine_kernel, tm=tm),
        grid_spec=grid_spec,
        out_shape=jax.ShapeDtypeStruct((n, D_MODEL), F32),
        compiler_params=_cparams(("arbitrary",)),
        name="combine",
    )(p1, p2, h2, rt, os)


def kernel(x_prompt, x_sample, cache_k, cache_v, state_pool, page_table, g_mix, w_in, g_q, g_k, w_pool, pool_scale, w_out, g_ffn, w_group_router, b_group_router, w_expert_router, b_expert_router, w_gate, w_up, w_down):
    B, S, _ = x_prompt.shape
    DB, T, _ = x_sample.shape
    depth = w_in.shape[0]
    assert B == 1 and depth == 1
    past_len = page_table.shape[1] * cache_k.shape[2]
    l = 0

    w_in_bf = w_in[l].astype(BF16)
    w_out_bf = w_out[l].astype(BF16)
    wg_bf, wu_bf, wd_bf = w_gate[l].astype(BF16), w_up[l].astype(BF16), w_down[l].astype(BF16)
    gm, gq, gk, gf = g_mix[l][None], g_q[l][None], g_k[l][None], g_ffn[l][None]
    ps = pool_scale[l][None]
    n_r = N_EXPERT_GROUPS + N_EXPERTS
    w_router = jnp.concatenate([w_group_router[l], w_expert_router[l].reshape(D_MODEL, N_EXPERTS)], axis=1)
    w_router = jnp.pad(w_router, ((0, 0), (0, LANES - n_r)))
    w_router_hi = w_router.astype(BF16)
    w_router = jnp.concatenate([w_router_hi, (w_router - w_router_hi.astype(F32)).astype(BF16)], axis=1)
    b_router = jnp.concatenate([b_group_router[l], b_expert_router[l].reshape(N_EXPERTS)])
    b_router = jnp.pad(b_router, (0, LANES - n_r))[None]

    def mixer_tail(x2d, o_attn, d):
        return out_proj(x2d, o_attn, d, w_pool[l], ps, w_out_bf, gf, w_router, b_router, tm=512)

    xp = x_prompt.reshape(S, D_MODEL)
    q_p, k_p, kb_p, km_p, v_p, vb_p, u_p = in_proj(xp, gm, w_in_bf, gq, gk, tm=512)
    o_p = moba_prompt(q_p, kb_p, vb_p, km_p.reshape(S // MOBA_BLOCK, ATTN_WIDTH))
    d_p = pool_prompt(u_p, tm=512)
    h2_p, rt_p = mixer_tail(xp, o_p, d_p)

    n_s = DB * T
    xs = x_sample.reshape(n_s, D_MODEL)
    q_s, k_s, _, _, v_s, _, u_s = in_proj(xs, gm, w_in_bf, gq, gk, tm=n_s)
    r3 = lambda a: a.reshape(DB, T, ATTN_WIDTH)
    o_s = moba_sample(r3(q_s), r3(k_s), r3(v_s), cache_k, cache_v, page_table + l * cache_k.shape[1])
    d_s, pool_s = pool_sample(u_s.reshape(DB, T, POOL_WIDTH), state_pool[l], start_pos=past_len)
    h2_s, rt_s = mixer_tail(xs, o_s.reshape(n_s, ATTN_WIDTH), d_s.reshape(n_s, POOL_WIDTH))

    pos, meta = route(jnp.concatenate([rt_p, rt_s], axis=0))
    p1, p2 = pos[:, 0], pos[:, 1]
    max_tiles = 2 * (S + n_s) // MOE_TILE + N_EXPERTS
    tiles_per = meta[0, :N_EXPERTS].astype(jnp.int32)
    ends = tiles_per + meta[1, :N_EXPERTS].astype(jnp.int32)
    n_used = ends[N_EXPERTS - 1]
    tile_src = jnp.minimum(jnp.arange(max_tiles, dtype=jnp.int32), n_used - 1)
    tile_expert = jnp.minimum(jnp.sum(tile_src[:, None] >= ends[None, :], axis=1), N_EXPERTS - 1).astype(jnp.int32)
    tail = n_used + jnp.arange(N_EXPERTS, dtype=jnp.int32)
    zero_tiles = jnp.concatenate([jnp.where(tiles_per > 0, ends - 1, -1), jnp.where(tail < max_tiles, tail, -1)])
    x_sorted = dispatch(p1, p2, zero_tiles, h2_p, h2_s, max_tiles * MOE_TILE, tm=MOE_TILE)
    o_sorted = experts(tile_expert, tile_src, n_used[None], x_sorted, gf, wg_bf, wu_bf, wd_bf)
    y_prompt = combine(p1[:S], p2[:S], h2_p, rt_p, o_sorted, tm=MOE_TILE).reshape(B, S, D_MODEL)
    y_sample = combine(p1[S:], p2[S:], h2_s, rt_s, o_sorted, tm=MOE_TILE).reshape(DB, T, D_MODEL)

    hd = (N_HEADS, HEAD_DIM)
    return (
        y_prompt,
        y_sample,
        k_p.reshape(1, B, S, *hd),
        v_p.reshape(1, B, S, *hd),
        u_p[S - POOL_STATE:].reshape(1, B, POOL_STATE, POOL_WIDTH),
        k_s.reshape(1, DB, T, *hd),
        v_s.reshape(1, DB, T, *hd),
        pool_s.reshape(1, DB, POOL_STATE, POOL_WIDTH),
    )
```

```python
import functools

import jax
import jax.numpy as jnp
from jax import lax
from jax.experimental import pallas as pl
from jax.experimental.pallas import tpu as pltpu

D_MODEL = 2048
ATTN_WIDTH = 1024
POOL_WIDTH = 1024
HEAD_DIM = 128
N_HEADS = 8
POOL_WINDOWS = (2, 4, 8, 16)
POOL_GROUP_W = 256
POOL_STATE = 15
MOBA_BLOCK = 256
MOBA_TOPK = 3
N_EXPERT_GROUPS = 4
EXPERTS_PER_GROUP = 4
N_EXPERTS = 16
D_EXPERT = 768
EPS = 1e-6
NEG = -1e30
LANES = 128
VMEM_LIMIT = 56 * 1024 * 1024

BF16 = jnp.bfloat16
F32 = jnp.float32


def _cparams(sem):
    return pltpu.CompilerParams(dimension_semantics=sem, vmem_limit_bytes=VMEM_LIMIT)


def _dot(a, b):
    return jnp.dot(a, b, preferred_element_type=F32)


def _dot_nt(a, b, precision=None):
    return lax.dot_general(a, b, (((1,), (1,)), ((), ())), precision=precision,
                           preferred_element_type=F32)


def _top_mask(g, ids, k, axis=-1):
    sel = jnp.zeros(g.shape, F32)
    for _ in range(k):
        m = jnp.max(g, axis=axis, keepdims=True)
        idx = jnp.min(jnp.where(g == m, ids, jnp.float32(g.shape[axis])), axis=axis, keepdims=True)
        pick = ids == idx
        sel = jnp.where(pick, 1.0, sel)
        g = jnp.where(pick, -jnp.inf, g)
    return sel


def _in_proj_kernel(x_ref, gmix_ref, w_ref, gq_ref, gk_ref,
                    q_ref, k_ref, kb_ref, km_ref, v_ref, vb_ref, u_ref, hb_ref, *, tm):
    xf = x_ref[...]
    ms = jnp.mean(xf * xf, axis=-1, keepdims=True)
    hb_ref[...] = (xf * lax.rsqrt(ms + EPS) * gmix_ref[...]).astype(BF16)

    def head_norm(z, out_ref, g_ref):
        for h in range(N_HEADS):
            zh = z[:, h * HEAD_DIM:(h + 1) * HEAD_DIM]
            ms = jnp.mean(zh * zh, axis=-1, keepdims=True)
            out_ref[:, h * HEAD_DIM:(h + 1) * HEAD_DIM] = zh * lax.rsqrt(ms + EPS) * g_ref[...]

    head_norm(_dot(hb_ref[...], w_ref[:, 0:ATTN_WIDTH]), q_ref, gq_ref)
    head_norm(_dot(hb_ref[...], w_ref[:, ATTN_WIDTH:2 * ATTN_WIDTH]), k_ref, gk_ref)
    kb_ref[...] = k_ref[...].astype(BF16)
    for g in range(tm // MOBA_BLOCK):
        blk = k_ref[g * MOBA_BLOCK:(g + 1) * MOBA_BLOCK, :]
        km_ref[g] = jnp.mean(blk, axis=0, keepdims=True)
    zv = _dot(hb_ref[...], w_ref[:, 2 * ATTN_WIDTH:3 * ATTN_WIDTH])
    v_ref[...] = zv
    vb_ref[...] = zv.astype(BF16)
    u_ref[...] = _dot(hb_ref[...], w_ref[:, 3 * ATTN_WIDTH:])


def in_proj(x, g_mix, w_in_bf, g_q, g_k, *, tm):
    n = x.shape[0]
    wide = lambda i: (i, 0)
    const = lambda i: (0, 0)
    out_shape = (
        jax.ShapeDtypeStruct((n, ATTN_WIDTH), F32),
        jax.ShapeDtypeStruct((n, ATTN_WIDTH), F32),
        jax.ShapeDtypeStruct((n, ATTN_WIDTH), BF16),
        jax.ShapeDtypeStruct((n // MOBA_BLOCK, 1, ATTN_WIDTH), F32),
        jax.ShapeDtypeStruct((n, ATTN_WIDTH), F32),
        jax.ShapeDtypeStruct((n, ATTN_WIDTH), BF16),
        jax.ShapeDtypeStruct((n, POOL_WIDTH), F32),
    )
    blk = pl.BlockSpec((tm, ATTN_WIDTH), wide)
    return pl.pallas_call(
        functools.partial(_in_proj_kernel, tm=tm),
        grid=(n // tm,),
        in_specs=[
            pl.BlockSpec((tm, D_MODEL), wide),
            pl.BlockSpec((1, D_MODEL), const),
            pl.BlockSpec(w_in_bf.shape, const, pipeline_mode=pl.Buffered(1)),
            pl.BlockSpec((1, HEAD_DIM), const),
            pl.BlockSpec((1, HEAD_DIM), const),
        ],
        out_specs=(blk, blk, blk,
                   pl.BlockSpec((tm // MOBA_BLOCK, 1, ATTN_WIDTH), lambda i: (i, 0, 0)),
                   blk, blk, blk),
        out_shape=out_shape,
        scratch_shapes=[pltpu.VMEM((tm, D_MODEL), BF16)],
        compiler_params=_cparams(("arbitrary",)),
        name="in_proj",
    )(x, g_mix, w_in_bf, g_q, g_k)


LOG2E = 1.4426950408889634
ATT_HEADS = 2
ATT_GROUP = 4


def _moba_prompt_kernel(q_ref, k_ref, vt_ref, km_ref, o_ref, qs_ref, bias_ref, m_ref, l_ref, acc_ref,
                        s_ref, mb_ref, *, nb, hp, grp):
    own = pl.program_id(1)
    base = pl.multiple_of(own * MOBA_BLOCK, MOBA_BLOCK)
    blk_id = lax.broadcasted_iota(jnp.int32, (nb, MOBA_BLOCK), 0).astype(F32)
    valid = blk_id < own.astype(F32)
    key_i = lax.broadcasted_iota(jnp.int32, (MOBA_BLOCK, MOBA_BLOCK), 0)
    qry_i = lax.broadcasted_iota(jnp.int32, (MOBA_BLOCK, MOBA_BLOCK), 1)

    for hh in range(hp):
        cols = slice(hh * HEAD_DIM, (hh + 1) * HEAD_DIM)
        q = q_ref[:, cols]
        gate = _dot_nt(km_ref[:, cols], q, precision=lax.Precision.HIGHEST)
        sel = _top_mask(jnp.where(valid, gate, NEG), blk_id, MOBA_TOPK, axis=0)
        bias = jnp.where(jnp.logical_and(sel > 0.5, valid), 0.0, NEG)
        for n in range(nb):
            bias_ref[hh, n] = bias[n:n + 1, :]
        qs = (q * (HEAD_DIM ** -0.5 * LOG2E)).astype(BF16)
        qs_ref[hh] = qs
        s = _dot_nt(k_ref[pl.ds(base, MOBA_BLOCK), cols], qs)
        s = jnp.where(key_i <= qry_i, s, NEG)
        m0 = jnp.max(s, axis=0, keepdims=True)
        p = jnp.exp2(s - m0)
        m_ref[hh] = m0
        l_ref[hh] = jnp.sum(p, axis=0, keepdims=True)
        acc_ref[hh] = _dot(vt_ref[own, cols, :], p.astype(BF16))

    n_iter = (own + grp - 1) // grp

    def scores(i, slot):
        for hh in range(hp):
            cols = slice(hh * HEAD_DIM, (hh + 1) * HEAD_DIM)
            qs = qs_ref[hh]
            mb = None
            for gg in range(grp):
                n = jnp.minimum(i * grp + gg, nb - 1)
                off = pl.multiple_of(n * MOBA_BLOCK, MOBA_BLOCK)
                s = _dot_nt(k_ref[pl.ds(off, MOBA_BLOCK), cols], qs) + bias_ref[hh, n]
                s_ref[slot, hh, gg] = s
                smax = jnp.max(s, axis=0, keepdims=True)
                mb = smax if mb is None else jnp.maximum(mb, smax)
            mb_ref[slot, hh] = mb

    def softmax_pv(i, slot):
        for hh in range(hp):
            cols = slice(hh * HEAD_DIM, (hh + 1) * HEAD_DIM)
            m_prev = m_ref[hh]
            m_new = jnp.maximum(m_prev, mb_ref[slot, hh])
            alpha = jnp.exp2(m_prev - m_new)
            l_new = alpha * l_ref[hh]
            acc = alpha * acc_ref[hh]
            for gg in range(grp):
                p = jnp.exp2(s_ref[slot, hh, gg] - m_new)
                l_new = l_new + jnp.sum(p, axis=0, keepdims=True)
                acc = acc + _dot(vt_ref[i * grp + gg, cols, :], p.astype(BF16))
            m_ref[hh] = m_new
            l_ref[hh] = l_new
            acc_ref[hh] = acc

    scores(0, 0)

    def body(k, carry):
        scores(2 * k + 1, 1)
        softmax_pv(2 * k, 0)
        scores(2 * k + 2, 0)
        softmax_pv(2 * k + 1, 1)
        return carry

    lax.fori_loop(0, n_iter // 2, body, 0)

    @pl.when(n_iter % 2 == 1)
    def _():
        softmax_pv(n_iter - 1, 0)

    for hh in range(hp):
        o = acc_ref[hh] * (1.0 / l_ref[hh])
        o_ref[:, hh * HEAD_DIM:(hh + 1) * HEAD_DIM] = o.T.astype(o_ref.dtype)


def moba_prompt(q, k_bf, v_bf, kmean):
    s = q.shape[0]
    nb = s // MOBA_BLOCK
    hp = ATT_HEADS
    grp = ATT_GROUP if nb % ATT_GROUP == 0 else 1
    wide = hp * HEAD_DIM
    vt = v_bf.reshape(nb, MOBA_BLOCK, ATTN_WIDTH).transpose(0, 2, 1)
    return pl.pallas_call(
        functools.partial(_moba_prompt_kernel, nb=nb, hp=hp, grp=grp),
        grid=(N_HEADS // hp, nb),
        in_specs=[
            pl.BlockSpec((MOBA_BLOCK, wide), lambda h, j: (j, h)),
            pl.BlockSpec((s, wide), lambda h, j: (0, h)),
            pl.BlockSpec((nb, wide, MOBA_BLOCK), lambda h, j: (0, h, 0)),
            pl.BlockSpec((nb, wide), lambda h, j: (0, h)),
        ],
        out_specs=pl.BlockSpec((MOBA_BLOCK, wide), lambda h, j: (j, h)),
        out_shape=jax.ShapeDtypeStruct((s, ATTN_WIDTH), BF16),
        scratch_shapes=[
            pltpu.VMEM((hp, MOBA_BLOCK, HEAD_DIM), BF16),
            pltpu.VMEM((hp, nb, 1, MOBA_BLOCK), F32),
            pltpu.VMEM((hp, 1, MOBA_BLOCK), F32),
            pltpu.VMEM((hp, 1, MOBA_BLOCK), F32),
            pltpu.VMEM((hp, HEAD_DIM, MOBA_BLOCK), F32),
            pltpu.VMEM((2, hp, grp, MOBA_BLOCK, MOBA_BLOCK), F32),
            pltpu.VMEM((2, hp, 1, MOBA_BLOCK), F32),
        ],
        compiler_params=_cparams(("arbitrary", "arbitrary")),
        name="moba_prompt",
    )(q, k_bf, vt, kmean)


def _page_bf16(ref):
    page = ref.shape[0] // N_HEADS
    heads = [ref[pl.ds(h, page, stride=N_HEADS), :] for h in range(N_HEADS)]
    return jnp.concatenate(heads, axis=1).astype(BF16)


def _moba_sample_kernel(pt_ref, q_ref, kn_ref, vn_ref, *refs, n_pages, t_new):
    k_pages = refs[:n_pages]
    v_pages = refs[n_pages:2 * n_pages]
    o_ref = refs[2 * n_pages]
    s_ref = refs[2 * n_pages + 1]
    page = k_pages[0].shape[0] // N_HEADS
    rows = t_new * N_HEADS
    n_blocks = n_pages * page // MOBA_BLOCK
    per_blk = MOBA_BLOCK // page

    qrep = jnp.concatenate(
        [jnp.broadcast_to(q_ref[0, t:t + 1, :], (N_HEADS, ATTN_WIDTH)) for t in range(t_new)], axis=0)
    r_i = lax.broadcasted_iota(jnp.int32, (rows, ATTN_WIDTH), 0)
    c_i = lax.broadcasted_iota(jnp.int32, (rows, ATTN_WIDTH), 1)
    diag = (c_i // HEAD_DIM) == (r_i % N_HEADS)
    qbd = jnp.where(diag, qrep * (HEAD_DIM ** -0.5), 0.0).astype(BF16)

    gcol = lax.broadcasted_iota(jnp.int32, (rows, LANES), 1).astype(F32)
    gate = jnp.full((rows, LANES), NEG, F32)
    for n in range(n_blocks):
        tot = jnp.zeros((rows, 1), F32)
        for pp in range(per_blk):
            p_i = n * per_blk + pp
            sp = _dot_nt(qbd, _page_bf16(k_pages[p_i]))
            s_ref[:, p_i * page:(p_i + 1) * page] = sp
            tot = tot + jnp.sum(sp, axis=-1, keepdims=True)
        gate = jnp.where(gcol == float(n), tot * (1.0 / MOBA_BLOCK), gate)
    sel = _top_mask(gate, gcol, min(MOBA_TOPK, n_blocks))

    s_own = _dot_nt(qbd, kn_ref[0].astype(BF16))
    oc = lax.broadcasted_iota(jnp.int32, s_own.shape, 1)
    orow = lax.broadcasted_iota(jnp.int32, s_own.shape, 0)
    s_own = jnp.where(oc <= orow // N_HEADS, s_own, NEG)

    m = jnp.max(s_own, axis=-1, keepdims=True)
    for n in range(n_blocks):
        sb = s_ref[:, n * MOBA_BLOCK:(n + 1) * MOBA_BLOCK]
        sb = jnp.where(sel[:, n:n + 1] > 0.5, sb, NEG)
        m = jnp.maximum(m, jnp.max(sb, axis=-1, keepdims=True))

    p_own = jnp.exp(s_own - m)
    l = jnp.sum(p_own, axis=-1, keepdims=True)
    acc = _dot(p_own.astype(BF16), vn_ref[0].astype(BF16))
    for p_i in range(n_pages):
        n = p_i // per_blk
        sp = s_ref[:, p_i * page:(p_i + 1) * page]
        pp = jnp.where(sel[:, n:n + 1] > 0.5, jnp.exp(sp - m), 0.0)
        l = l + jnp.sum(pp, axis=-1, keepdims=True)
        acc = acc + _dot(pp.astype(BF16), _page_bf16(v_pages[p_i]))
    o = jnp.where(diag, acc * (1.0 / l), 0.0)
    for t in range(t_new):
        o_ref[0, t:t + 1, :] = jnp.sum(o[t * N_HEADS:(t + 1) * N_HEADS, :], axis=0, keepdims=True)


def moba_sample(q, k_new, v_new, cache_k, cache_v, page_table):
    db, t_new, _ = q.shape
    n_pages = page_table.shape[1]
    page = cache_k.shape[-3]
    ck = cache_k.reshape(-1, page * N_HEADS, HEAD_DIM)
    cv = cache_v.reshape(-1, page * N_HEADS, HEAD_DIM)
    tok = pl.BlockSpec((1, t_new, ATTN_WIDTH), lambda b, pt: (b, 0, 0))

    def page_spec(p):
        return pl.BlockSpec((None, page * N_HEADS, HEAD_DIM), lambda b, pt: (pt[b, p], 0, 0))

    grid_spec = pltpu.PrefetchScalarGridSpec(
        num_scalar_prefetch=1,
        grid=(db,),
        in_specs=[tok, tok, tok] + [page_spec(p) for p in range(n_pages)] * 2,
        out_specs=tok,
        scratch_shapes=[pltpu.VMEM((t_new * N_HEADS, n_pages * page), F32)],
    )
    return pl.pallas_call(
        functools.partial(_moba_sample_kernel, n_pages=n_pages, t_new=t_new),
        grid_spec=grid_spec,
        out_shape=jax.ShapeDtypeStruct((db, t_new, ATTN_WIDTH), F32),
        compiler_params=_cparams(("arbitrary",)),
        name="moba_sample",
    )(page_table, q, k_new, v_new, *([ck] * n_pages), *([cv] * n_pages))


def _pool_prompt_kernel(u_ref, prev_ref, d_ref, ext_ref, *, tm):
    i = pl.program_id(0)
    pad = prev_ref.shape[0]
    prev = jnp.where(i > 0, prev_ref[...], 0.0)
    ext_ref[0:pad, :] = prev
    ext_ref[pad:pad + tm, :] = u_ref[...]
    pos = (i * tm + lax.broadcasted_iota(jnp.int32, (tm, 1), 0)).astype(F32)
    for g, w in enumerate(POOL_WINDOWS):
        c0, c1 = g * POOL_GROUP_W, (g + 1) * POOL_GROUP_W
        wsum = ext_ref[pad:pad + tm, c0:c1]
        for jj in range(1, w):
            wsum = wsum + ext_ref[pad - jj:pad - jj + tm, c0:c1]
        inv = 1.0 / jnp.minimum(jnp.float32(w), pos + 1.0)
        d_ref[:, c0:c1] = (wsum * inv - u_ref[:, c0:c1]).astype(d_ref.dtype)


def pool_prompt(u, *, tm):
    n = u.shape[0]
    pad = 16
    return pl.pallas_call(
        functools.partial(_pool_prompt_kernel, tm=tm),
        grid=(n // tm,),
        in_specs=[
            pl.BlockSpec((tm, POOL_WIDTH), lambda i: (i, 0)),
            pl.BlockSpec((pad, POOL_WIDTH), lambda i: (jnp.maximum(i * (tm // pad) - 1, 0), 0)),
        ],
        out_specs=pl.BlockSpec((tm, POOL_WIDTH), lambda i: (i, 0)),
        out_shape=jax.ShapeDtypeStruct((n, POOL_WIDTH), BF16),
        scratch_shapes=[pltpu.VMEM((tm + pad, POOL_WIDTH), F32)],
        compiler_params=_cparams(("arbitrary",)),
        name="pool_prompt",
    )(u, u)


def _pool_sample_kernel(u_ref, st_ref, d_ref, new_ref, *, t_new, start_pos):
    ext = [st_ref[:, r, :] for r in range(POOL_STATE)] + [u_ref[:, t, :] for t in range(t_new)]
    for t in range(t_new):
        e = POOL_STATE + t
        parts = []
        for g, w in enumerate(POOL_WINDOWS):
            c0, c1 = g * POOL_GROUP_W, (g + 1) * POOL_GROUP_W
            wsum = ext[e][:, c0:c1]
            for jj in range(1, w):
                wsum = wsum + ext[e - jj][:, c0:c1]
            count = min(float(w), float(start_pos + t) + 1.0)
            parts.append(wsum * (1.0 / count) - ext[e][:, c0:c1])
        d_ref[:, t, :] = jnp.concatenate(parts, axis=-1).astype(d_ref.dtype)
    for r in range(POOL_STATE):
        new_ref[:, r, :] = ext[t_new + r]


def pool_sample(u, state, *, start_pos, bb=32):
    db, t_new, _ = u.shape
    return pl.pallas_call(
        functools.partial(_pool_sample_kernel, t_new=t_new, start_pos=start_pos),
        grid=(db // bb,),
        in_specs=[
            pl.BlockSpec((bb, t_new, POOL_WIDTH), lambda i: (i, 0, 0)),
            pl.BlockSpec((bb, POOL_STATE, POOL_WIDTH), lambda i: (i, 0, 0)),
        ],
        out_specs=(
            pl.BlockSpec((bb, t_new, POOL_WIDTH), lambda i: (i, 0, 0)),
            pl.BlockSpec((bb, POOL_STATE, POOL_WIDTH), lambda i: (i, 0, 0)),
        ),
        out_shape=(
            jax.ShapeDtypeStruct((db, t_new, POOL_WIDTH), F32),
            jax.ShapeDtypeStruct((db, POOL_STATE, POOL_WIDTH), F32),
        ),
        compiler_params=_cparams(("arbitrary",)),
        name="pool_sample",
    )(u, state)


def _out_proj_kernel(x_ref, oa_ref, d_ref, wp_ref, ps_ref, wo_ref, gf_ref, wr_ref, br_ref,
                     h2_ref, rt_ref, mix_ref):
    mix_ref[:, 0:ATTN_WIDTH] = oa_ref[...].astype(BF16)
    dd = d_ref[...].astype(BF16)
    for g in range(len(POOL_WINDOWS)):
        c0, c1 = g * POOL_GROUP_W, (g + 1) * POOL_GROUP_W
        yg = _dot(dd[:, c0:c1], wp_ref[g].astype(BF16)) * ps_ref[:, c0:c1]
        mix_ref[:, ATTN_WIDTH + c0:ATTN_WIDTH + c1] = yg.astype(BF16)
    h2 = x_ref[...] + _dot(mix_ref[...], wo_ref[...])
    h2_ref[...] = h2
    ms = jnp.mean(h2 * h2, axis=-1, keepdims=True)
    hn = h2 * lax.rsqrt(ms + EPS) * gf_ref[...]

    hn_hi = hn.astype(BF16)
    hn_lo = (hn - hn_hi.astype(F32)).astype(BF16)
    t = _dot(hn_hi, wr_ref[...])
    logits = t[:, :LANES] + t[:, LANES:] + _dot(hn_lo, wr_ref[:, :LANES]) + br_ref[...]
    lane = lax.broadcasted_iota(jnp.int32, logits.shape, 1).astype(F32)
    far = jnp.float32(LANES)
    is_g = lane < N_EXPERT_GROUPS
    gl = jnp.where(is_g, logits, -jnp.inf)
    g_max = jnp.max(gl, axis=-1, keepdims=True)
    g_top = jnp.min(jnp.where(gl == g_max, lane, far), axis=-1, keepdims=True)
    g_p = 1.0 / jnp.sum(jnp.where(is_g, jnp.exp(gl - g_max), 0.0), axis=-1, keepdims=True)
    lo = N_EXPERT_GROUPS + g_top * EXPERTS_PER_GROUP
    in_grp = jnp.logical_and(lane >= lo, lane < lo + EXPERTS_PER_GROUP)
    el = jnp.where(in_grp, logits, -jnp.inf)
    e1 = jnp.max(el, axis=-1, keepdims=True)
    i1 = jnp.min(jnp.where(el == e1, lane, far), axis=-1, keepdims=True)
    el2 = jnp.where(lane == i1, -jnp.inf, el)
    e2 = jnp.max(el2, axis=-1, keepdims=True)
    i2 = jnp.min(jnp.where(el2 == e2, lane, far), axis=-1, keepdims=True)
    ex2 = jnp.exp(e2 - e1)
    den = 1.0 + ex2
    w1 = (1.0 / den) * g_p
    w2 = (ex2 / den) * g_p
    rt_ref[...] = jnp.where(lane == 0.0, i1 - N_EXPERT_GROUPS,
                            jnp.where(lane == 1.0, i2 - N_EXPERT_GROUPS,
                                      jnp.where(lane == 2.0, w1, jnp.where(lane == 3.0, w2, 0.0))))


def out_proj(x, o_attn, d, w_pool, pool_scale, w_out_bf, g_ffn, w_router, b_router, *, tm):
    n = x.shape[0]
    row = lambda i: (i, 0)
    const = lambda i: (0, 0)
    return pl.pallas_call(
        _out_proj_kernel,
        grid=(n // tm,),
        in_specs=[
            pl.BlockSpec((tm, D_MODEL), row),
            pl.BlockSpec((tm, ATTN_WIDTH), row),
            pl.BlockSpec((tm, POOL_WIDTH), row),
            pl.BlockSpec(w_pool.shape, lambda i: (0, 0, 0)),
            pl.BlockSpec((1, POOL_WIDTH), const),
            pl.BlockSpec((D_MODEL, D_MODEL), const),
            pl.BlockSpec((1, D_MODEL), const),
            pl.BlockSpec((D_MODEL, 2 * LANES), const),
            pl.BlockSpec((1, LANES), const),
        ],
        out_specs=(
            pl.BlockSpec((tm, D_MODEL), row),
            pl.BlockSpec((tm, LANES), row),
        ),
        out_shape=(
            jax.ShapeDtypeStruct((n, D_MODEL), F32),
            jax.ShapeDtypeStruct((n, LANES), F32),
        ),
        scratch_shapes=[pltpu.VMEM((tm, D_MODEL), BF16)],
        compiler_params=_cparams(("arbitrary",)),
        name="out_proj",
    )(x, o_attn, d, w_pool, pool_scale, w_out_bf, g_ffn, w_router, b_router)


MOE_TILE = 256


def _route_kernel(rt_ref, pos_ref, meta_ref, *, n_tiles):
    lane = lax.broadcasted_iota(jnp.int32, (MOE_TILE, LANES), 1).astype(F32)
    r_i = lax.broadcasted_iota(jnp.int32, (MOE_TILE, MOE_TILE), 0)
    c_i = lax.broadcasted_iota(jnp.int32, (MOE_TILE, MOE_TILE), 1)
    tri = jnp.where(c_i < r_i, 1.0, 0.0).astype(BF16)

    def one_hot(t):
        rt = rt_ref[pl.ds(pl.multiple_of(t * MOE_TILE, MOE_TILE), MOE_TILE), :]
        e1, e2 = rt[:, 0:1], rt[:, 1:2]
        return e1, e2, jnp.where(jnp.logical_or(lane == e1, lane == e2), 1.0, 0.0)

    def count(t, cnt):
        return cnt + jnp.sum(one_hot(t)[2], axis=0, keepdims=True)

    cnt = lax.fori_loop(0, n_tiles, count, jnp.zeros((1, LANES), F32))
    tiles_per = jnp.floor((cnt + (MOE_TILE - 1)) * (1.0 / MOE_TILE))
    e_r = lax.broadcasted_iota(jnp.int32, (LANES, LANES), 0)
    e_c = lax.broadcasted_iota(jnp.int32, (LANES, LANES), 1)
    upper = jnp.where(e_r < e_c, 1.0, 0.0).astype(BF16)
    off_tiles = _dot(jnp.broadcast_to(tiles_per, (8, LANES)).astype(BF16), upper)[0:1]
    base = off_tiles * MOE_TILE

    def place(t, run):
        e1, e2, oh = one_hot(t)
        dest = base + run + _dot(tri, oh.astype(BF16))
        p1 = jnp.sum(jnp.where(lane == e1, dest, 0.0), axis=1, keepdims=True)
        p2 = jnp.sum(jnp.where(lane == e2, dest, 0.0), axis=1, keepdims=True)
        pos = jnp.where(lane == 0.0, p1, jnp.where(lane == 1.0, p2, 0.0))
        pos_ref[pl.ds(pl.multiple_of(t * MOE_TILE, MOE_TILE), MOE_TILE), :] = pos.astype(jnp.int32)
        return run + jnp.sum(oh, axis=0, keepdims=True)

    lax.fori_loop(0, n_tiles, place, jnp.zeros((1, LANES), F32))
    row = lax.broadcasted_iota(jnp.int32, (8, LANES), 0)
    meta_ref[...] = jnp.where(row == 0, tiles_per, jnp.where(row == 1, off_tiles, 0.0))


def route(rt):
    n = rt.shape[0]
    assert n % MOE_TILE == 0
    return pl.pallas_call(
        functools.partial(_route_kernel, n_tiles=n // MOE_TILE),
        out_shape=(
            jax.ShapeDtypeStruct((n, LANES), jnp.int32),
            jax.ShapeDtypeStruct((8, LANES), F32),
        ),
        compiler_params=pltpu.CompilerParams(vmem_limit_bytes=VMEM_LIMIT),
        name="route",
    )(rt)


ROW_UNROLL = 8


def _dispatch_kernel(p1_ref, p2_ref, zt_ref, xa_ref, xb_ref, xs_ref, zero_ref, sem, zsem, *, tm, tiles_a, n_zero):
    i = pl.program_id(0)

    @pl.when(i == 0)
    def _():
        zero_ref[...] = jnp.zeros(zero_ref.shape, zero_ref.dtype)

        def zcopy(z):
            row0 = pl.multiple_of(zt_ref[z] * tm, tm)
            return pltpu.make_async_copy(zero_ref, xs_ref.at[pl.ds(row0, tm)], zsem)

        for z in range(n_zero):
            @pl.when(zt_ref[z] >= 0)
            def _():
                zcopy(z).start()
        for z in range(n_zero):
            @pl.when(zt_ref[z] >= 0)
            def _():
                zcopy(z).wait()

    def scatter(x_ref):
        t0 = i * tm

        def issue(r, c):
            src = x_ref.at[pl.ds(r, 1)]
            pltpu.make_async_copy(src, xs_ref.at[pl.ds(p1_ref[t0 + r], 1)], sem).start()
            pltpu.make_async_copy(src, xs_ref.at[pl.ds(p2_ref[t0 + r], 1)], sem).start()
            return c

        lax.fori_loop(0, tm, issue, 0, unroll=ROW_UNROLL)
        for _ in range(2):
            pltpu.make_async_copy(x_ref, xs_ref.at[pl.ds(0, tm)], sem).wait()

    @pl.when(i < tiles_a)
    def _():
        scatter(xa_ref)

    @pl.when(i >= tiles_a)
    def _():
        scatter(xb_ref)


def dispatch(p1, p2, zero_tiles, xa, xb, rows, *, tm):
    tiles_a, tiles_b = xa.shape[0] // tm, xb.shape[0] // tm
    grid_spec = pltpu.PrefetchScalarGridSpec(
        num_scalar_prefetch=3,
        grid=(tiles_a + tiles_b,),
        in_specs=[
            pl.BlockSpec((tm, D_MODEL), lambda i, a, b, z: (jnp.minimum(i, tiles_a - 1), 0)),
            pl.BlockSpec((tm, D_MODEL), lambda i, a, b, z: (jnp.maximum(i - tiles_a, 0), 0)),
        ],
        out_specs=pl.BlockSpec(memory_space=pl.ANY),
        scratch_shapes=[pltpu.VMEM((tm, D_MODEL), F32), pltpu.SemaphoreType.DMA(()), pltpu.SemaphoreType.DMA(())],
    )
    return pl.pallas_call(
        functools.partial(_dispatch_kernel, tm=tm, tiles_a=tiles_a, n_zero=zero_tiles.shape[0]),
        grid_spec=grid_spec,
        out_shape=jax.ShapeDtypeStruct((rows, D_MODEL), F32),
        compiler_params=_cparams(("arbitrary",)),
        name="dispatch",
    )(p1, p2, zero_tiles, xa, xb)


def _experts_kernel(te_ref, ts_ref, nu_ref, x_ref, gf_ref, wg_ref, wu_ref, wd_ref, o_ref):
    @pl.when(pl.program_id(0) < nu_ref[0])
    def _():
        h = x_ref[...]
        ms = jnp.mean(h * h, axis=-1, keepdims=True)
        x = (h * lax.rsqrt(ms + EPS) * gf_ref[...]).astype(BF16).astype(F32)
        a = _dot(x, wg_ref[0])
        b = _dot(x, wu_ref[0])
        act = (a * (1.0 / (1.0 + jnp.exp(-a)))) * b
        o_ref[...] = _dot(act.astype(BF16).astype(F32), wd_ref[0])

    @pl.when(pl.program_id(0) >= nu_ref[0])
    def _():
        o_ref[...] = jnp.zeros(o_ref.shape, o_ref.dtype)


def experts(tile_expert, tile_src, n_used, xs, g_ffn, w_gate, w_up, w_down, *, layer):
    rows, w = xs.shape
    wspec = lambda shape: pl.BlockSpec((None, 1) + shape, lambda j, te, ts, nu: (layer, te[j], 0, 0))
    grid_spec = pltpu.PrefetchScalarGridSpec(
        num_scalar_prefetch=3,
        grid=(rows // MOE_TILE,),
        in_specs=[
            pl.BlockSpec((MOE_TILE, w), lambda j, te, ts, nu: (ts[j], 0)),
            pl.BlockSpec((1, D_MODEL), lambda j, te, ts, nu: (0, 0)),
            wspec((D_MODEL, D_EXPERT)), wspec((D_MODEL, D_EXPERT)), wspec((D_EXPERT, D_MODEL)),
        ],
        out_specs=pl.BlockSpec((MOE_TILE, D_MODEL), lambda j, te, ts, nu: (j, 0)),
    )
    return pl.pallas_call(
        _experts_kernel,
        grid_spec=grid_spec,
        out_shape=jax.ShapeDtypeStruct((rows, D_MODEL), F32),
        compiler_params=_cparams(("arbitrary",)),
        name="experts",
    )(tile_expert, tile_src, n_used, xs, g_ffn, w_gate, w_up, w_down)


def _combine_kernel(p1_ref, p2_ref, h2_ref, rt_ref, os_ref, y_ref, buf_ref, sem, *, tm):
    t0 = pl.program_id(0) * tm

    def issue(r, c):
        pltpu.make_async_copy(os_ref.at[pl.ds(p1_ref[t0 + r], 1)], buf_ref.at[0, pl.ds(r, 1)], sem).start()
        pltpu.make_async_copy(os_ref.at[pl.ds(p2_ref[t0 + r], 1)], buf_ref.at[1, pl.ds(r, 1)], sem).start()
        return c

    lax.fori_loop(0, tm, issue, 0, unroll=ROW_UNROLL)
    for s in range(2):
        pltpu.make_async_copy(os_ref.at[pl.ds(0, tm)], buf_ref.at[s], sem).wait()
    rt = rt_ref[...]
    y_ref[...] = h2_ref[...] + rt[:, 2:3] * buf_ref[0] + rt[:, 3:4] * buf_ref[1]


def combine(p1, p2, h2, rt, os, *, tm):
    n = h2.shape[0]
    row = lambda i, a, b: (i, 0)
    grid_spec = pltpu.PrefetchScalarGridSpec(
        num_scalar_prefetch=2,
        grid=(n // tm,),
        in_specs=[pl.BlockSpec((tm, D_MODEL), row), pl.BlockSpec((tm, LANES), row),
                  pl.BlockSpec(memory_space=pl.ANY)],
        out_specs=pl.BlockSpec((tm, D_MODEL), row),
        scratch_shapes=[pltpu.VMEM((2, tm, D_MODEL), F32), pltpu.SemaphoreType.DMA(())],
    )
    return pl.pallas_call(
        functools.partial(_combine_kernel, tm=tm),
        grid_spec=grid_spec,
        out_shape=jax.ShapeDtypeStruct((n, D_MODEL), F32),
        compiler_params=_cparams(("arbitrary",)),
        name="combine",
    )(p1, p2, h2, rt, os)


def kernel(x_prompt, x_sample, cache_k, cache_v, state_pool, page_table, g_mix, w_in, g_q, g_k, w_pool, pool_scale, w_out, g_ffn, w_group_router, b_group_router, w_expert_router, b_expert_router, w_gate, w_up, w_down):
    B, S, _ = x_prompt.shape
    DB, T, _ = x_sample.shape
    depth = w_in.shape[0]
    assert B == 1 and depth == 1
    past_len = page_table.shape[1] * cache_k.shape[2]
    l = 0

    w_in_bf = w_in[l].astype(BF16)
    w_out_bf = w_out[l].astype(BF16)
    gm, gq, gk, gf = g_mix[l][None], g_q[l][None], g_k[l][None], g_ffn[l][None]
    ps = pool_scale[l][None]
    n_r = N_EXPERT_GROUPS + N_EXPERTS
    w_router = jnp.concatenate([w_group_router[l], w_expert_router[l].reshape(D_MODEL, N_EXPERTS)], axis=1)
    w_router = jnp.pad(w_router, ((0, 0), (0, LANES - n_r)))
    w_router_hi = w_router.astype(BF16)
    w_router = jnp.concatenate([w_router_hi, (w_router - w_router_hi.astype(F32)).astype(BF16)], axis=1)
    b_router = jnp.concatenate([b_group_router[l], b_expert_router[l].reshape(N_EXPERTS)])
    b_router = jnp.pad(b_router, (0, LANES - n_r))[None]

    def mixer_tail(x2d, o_attn, d):
        return out_proj(x2d, o_attn, d, w_pool[l], ps, w_out_bf, gf, w_router, b_router, tm=512)

    xp = x_prompt.reshape(S, D_MODEL)
    q_p, k_p, kb_p, km_p, v_p, vb_p, u_p = in_proj(xp, gm, w_in_bf, gq, gk, tm=512)
    o_p = moba_prompt(q_p, kb_p, vb_p, km_p.reshape(S // MOBA_BLOCK, ATTN_WIDTH))
    d_p = pool_prompt(u_p, tm=512)
    h2_p, rt_p = mixer_tail(xp, o_p, d_p)

    n_s = DB * T
    xs = x_sample.reshape(n_s, D_MODEL)
    q_s, k_s, _, _, v_s, _, u_s = in_proj(xs, gm, w_in_bf, gq, gk, tm=n_s)
    r3 = lambda a: a.reshape(DB, T, ATTN_WIDTH)
    o_s = moba_sample(r3(q_s), r3(k_s), r3(v_s), cache_k, cache_v, page_table + l * cache_k.shape[1])
    d_s, pool_s = pool_sample(u_s.reshape(DB, T, POOL_WIDTH), state_pool[l], start_pos=past_len)
    h2_s, rt_s = mixer_tail(xs, o_s.reshape(n_s, ATTN_WIDTH), d_s.reshape(n_s, POOL_WIDTH))

    pos, meta = route(jnp.concatenate([rt_p, rt_s], axis=0))
    p1, p2 = pos[:, 0], pos[:, 1]
    max_tiles = 2 * (S + n_s) // MOE_TILE + N_EXPERTS
    tiles_per = meta[0, :N_EXPERTS].astype(jnp.int32)
    ends = tiles_per + meta[1, :N_EXPERTS].astype(jnp.int32)
    n_used = ends[N_EXPERTS - 1]
    tile_src = jnp.minimum(jnp.arange(max_tiles, dtype=jnp.int32), n_used - 1)
    tile_expert = jnp.minimum(jnp.sum(tile_src[:, None] >= ends[None, :], axis=1), N_EXPERTS - 1).astype(jnp.int32)
    tail = n_used + jnp.arange(N_EXPERTS, dtype=jnp.int32)
    zero_tiles = jnp.concatenate([jnp.where(tiles_per > 0, ends - 1, -1), jnp.where(tail < max_tiles, tail, -1)])
    x_sorted = dispatch(p1, p2, zero_tiles, h2_p, h2_s, max_tiles * MOE_TILE, tm=MOE_TILE)
    o_sorted = experts(tile_expert, tile_src, n_used[None], x_sorted, gf, w_gate, w_up, w_down, layer=l)
    y_prompt = combine(p1[:S], p2[:S], h2_p, rt_p, o_sorted, tm=MOE_TILE).reshape(B, S, D_MODEL)
    y_sample = combine(p1[S:], p2[S:], h2_s, rt_s, o_sorted, tm=MOE_TILE).reshape(DB, T, D_MODEL)

    hd = (N_HEADS, HEAD_DIM)
    return (
        y_prompt,
        y_sample,
        k_p.reshape(1, B, S, *hd),
        v_p.reshape(1, B, S, *hd),
        u_p[S - POOL_STATE:].reshape(1, B, POOL_STATE, POOL_WIDTH),
        k_s.reshape(1, DB, T, *hd),
        v_s.reshape(1, DB, T, *hd),
        pool_s.reshape(1, DB, POOL_STATE, POOL_WIDTH),
    )
```

```python
import functools

import jax
import jax.numpy as jnp
from jax import lax
from jax.experimental import pallas as pl
from jax.experimental.pallas import tpu as pltpu

D_MODEL = 2048
ATTN_WIDTH = 1024
POOL_WIDTH = 1024
HEAD_DIM = 128
N_HEADS = 8
POOL_WINDOWS = (2, 4, 8, 16)
POOL_GROUP_W = 256
POOL_STATE = 15
MOBA_BLOCK = 256
MOBA_TOPK = 3
N_EXPERT_GROUPS = 4
EXPERTS_PER_GROUP = 4
N_EXPERTS = 16
D_EXPERT = 768
EPS = 1e-6
NEG = -1e30
LANES = 128
VMEM_LIMIT = 56 * 1024 * 1024
ATTN_VMEM_LIMIT = 60 * 1024 * 1024
EXPERT_VMEM_LIMIT = 60 * 1024 * 1024

BF16 = jnp.bfloat16
F32 = jnp.float32


def _cparams(sem, vmem_limit=VMEM_LIMIT):
    return pltpu.CompilerParams(dimension_semantics=sem, vmem_limit_bytes=vmem_limit)


def _dot(a, b):
    return jnp.dot(a, b, preferred_element_type=F32)


def _dot_nt(a, b, precision=None):
    return lax.dot_general(a, b, (((1,), (1,)), ((), ())), precision=precision,
                           preferred_element_type=F32)


def _top_mask(g, ids, k, axis=-1):
    sel = jnp.zeros(g.shape, F32)
    for _ in range(k):
        m = jnp.max(g, axis=axis, keepdims=True)
        idx = jnp.min(jnp.where(g == m, ids, jnp.float32(g.shape[axis])), axis=axis, keepdims=True)
        pick = ids == idx
        sel = jnp.where(pick, 1.0, sel)
        g = jnp.where(pick, -jnp.inf, g)
    return sel


def _in_proj_kernel(x_ref, gmix_ref, w_ref, gq_ref, gk_ref,
                    q_ref, k_ref, kb_ref, km_ref, v_ref, vb_ref, u_ref, hb_ref, *, tm):
    xf = x_ref[...]
    ms = jnp.mean(xf * xf, axis=-1, keepdims=True)
    hb_ref[...] = (xf * lax.rsqrt(ms + EPS) * gmix_ref[...]).astype(BF16)

    def head_norm(z, out_ref, g_ref):
        for h in range(N_HEADS):
            zh = z[:, h * HEAD_DIM:(h + 1) * HEAD_DIM]
            ms = jnp.mean(zh * zh, axis=-1, keepdims=True)
            out_ref[:, h * HEAD_DIM:(h + 1) * HEAD_DIM] = zh * lax.rsqrt(ms + EPS) * g_ref[...]

    head_norm(_dot(hb_ref[...], w_ref[:, 0:ATTN_WIDTH]), q_ref, gq_ref)
    head_norm(_dot(hb_ref[...], w_ref[:, ATTN_WIDTH:2 * ATTN_WIDTH]), k_ref, gk_ref)
    kb_ref[...] = k_ref[...].astype(BF16)
    for g in range(tm // MOBA_BLOCK):
        blk = k_ref[g * MOBA_BLOCK:(g + 1) * MOBA_BLOCK, :]
        km_ref[g] = jnp.mean(blk, axis=0, keepdims=True)
    zv = _dot(hb_ref[...], w_ref[:, 2 * ATTN_WIDTH:3 * ATTN_WIDTH])
    v_ref[...] = zv
    vb_ref[...] = zv.astype(BF16)
    u_ref[...] = _dot(hb_ref[...], w_ref[:, 3 * ATTN_WIDTH:])


def in_proj(x, g_mix, w_in_bf, g_q, g_k, *, tm):
    n = x.shape[0]
    wide = lambda i: (i, 0)
    const = lambda i: (0, 0)
    out_shape = (
        jax.ShapeDtypeStruct((n, ATTN_WIDTH), F32),
        jax.ShapeDtypeStruct((n, ATTN_WIDTH), F32),
        jax.ShapeDtypeStruct((n, ATTN_WIDTH), BF16),
        jax.ShapeDtypeStruct((n // MOBA_BLOCK, 1, ATTN_WIDTH), F32),
        jax.ShapeDtypeStruct((n, ATTN_WIDTH), F32),
        jax.ShapeDtypeStruct((n, ATTN_WIDTH), BF16),
        jax.ShapeDtypeStruct((n, POOL_WIDTH), F32),
    )
    blk = pl.BlockSpec((tm, ATTN_WIDTH), wide)
    return pl.pallas_call(
        functools.partial(_in_proj_kernel, tm=tm),
        grid=(n // tm,),
        in_specs=[
            pl.BlockSpec((tm, D_MODEL), wide),
            pl.BlockSpec((1, D_MODEL), const),
            pl.BlockSpec(w_in_bf.shape, const, pipeline_mode=pl.Buffered(1)),
            pl.BlockSpec((1, HEAD_DIM), const),
            pl.BlockSpec((1, HEAD_DIM), const),
        ],
        out_specs=(blk, blk, blk,
                   pl.BlockSpec((tm // MOBA_BLOCK, 1, ATTN_WIDTH), lambda i: (i, 0, 0)),
                   blk, blk, blk),
        out_shape=out_shape,
        scratch_shapes=[pltpu.VMEM((tm, D_MODEL), BF16)],
        compiler_params=_cparams(("arbitrary",)),
        name="in_proj",
    )(x, g_mix, w_in_bf, g_q, g_k)


LOG2E = 1.4426950408889634
ATT_HEADS = 2
ATT_GROUP = 4


def _moba_prompt_body(own, q_ref, k_ref, vt_ref, km_ref, o_ref, qs_ref, bias_ref, m_ref, l_ref, acc_ref,
                      s_ref, mb_ref, *, nb, hp, grp):
    base = pl.multiple_of(own * MOBA_BLOCK, MOBA_BLOCK)
    blk_id = lax.broadcasted_iota(jnp.int32, (nb, MOBA_BLOCK), 0).astype(F32)
    valid = blk_id < own.astype(F32)
    key_i = lax.broadcasted_iota(jnp.int32, (MOBA_BLOCK, MOBA_BLOCK), 0)
    qry_i = lax.broadcasted_iota(jnp.int32, (MOBA_BLOCK, MOBA_BLOCK), 1)

    for hh in range(hp):
        cols = slice(hh * HEAD_DIM, (hh + 1) * HEAD_DIM)
        q = q_ref[:, cols]
        gate = _dot_nt(km_ref[:, cols], q, precision=lax.Precision.HIGHEST)
        sel = _top_mask(jnp.where(valid, gate, NEG), blk_id, MOBA_TOPK, axis=0)
        bias = jnp.where(jnp.logical_and(sel > 0.5, valid), 0.0, NEG)
        for n in range(nb):
            bias_ref[hh, n] = bias[n:n + 1, :]
        qs = (q * (HEAD_DIM ** -0.5 * LOG2E)).astype(BF16)
        qs_ref[hh] = qs
        s = _dot_nt(k_ref[pl.ds(base, MOBA_BLOCK), cols], qs)
        s = jnp.where(key_i <= qry_i, s, NEG)
        m0 = jnp.max(s, axis=0, keepdims=True)
        p = jnp.exp2(s - m0)
        m_ref[hh] = m0
        l_ref[hh] = jnp.sum(p, axis=0, keepdims=True)
        acc_ref[hh] = _dot(vt_ref[own, cols, :], p.astype(BF16))

    n_iter = (own + grp - 1) // grp

    def scores(i, slot):
        for hh in range(hp):
            cols = slice(hh * HEAD_DIM, (hh + 1) * HEAD_DIM)
            qs = qs_ref[hh]
            mb = None
            for gg in range(grp):
                n = jnp.minimum(i * grp + gg, nb - 1)
                off = pl.multiple_of(n * MOBA_BLOCK, MOBA_BLOCK)
                s = _dot_nt(k_ref[pl.ds(off, MOBA_BLOCK), cols], qs) + bias_ref[hh, n]
                s_ref[slot, hh, gg] = s
                smax = jnp.max(s, axis=0, keepdims=True)
                mb = smax if mb is None else jnp.maximum(mb, smax)
            mb_ref[slot, hh] = mb

    def softmax_pv(i, slot):
        for hh in range(hp):
            cols = slice(hh * HEAD_DIM, (hh + 1) * HEAD_DIM)
            m_prev = m_ref[hh]
            m_new = jnp.maximum(m_prev, mb_ref[slot, hh])
            alpha = jnp.exp2(m_prev - m_new)
            l_new = alpha * l_ref[hh]
            acc = alpha * acc_ref[hh]
            for gg in range(grp):
                p = jnp.exp2(s_ref[slot, hh, gg] - m_new)
                l_new = l_new + jnp.sum(p, axis=0, keepdims=True)
                acc = acc + _dot(vt_ref[i * grp + gg, cols, :], p.astype(BF16))
            m_ref[hh] = m_new
            l_ref[hh] = l_new
            acc_ref[hh] = acc

    scores(0, 0)

    def body(k, carry):
        scores(2 * k + 1, 1)
        softmax_pv(2 * k, 0)
        scores(2 * k + 2, 0)
        softmax_pv(2 * k + 1, 1)
        return carry

    lax.fori_loop(0, n_iter // 2, body, 0)

    @pl.when(n_iter % 2 == 1)
    def _():
        softmax_pv(n_iter - 1, 0)

    for hh in range(hp):
        o = acc_ref[hh] * (1.0 / l_ref[hh])
        o_ref[:, hh * HEAD_DIM:(hh + 1) * HEAD_DIM] = o.T.astype(o_ref.dtype)


def _prompt_scratch(nb, hp, grp):
    return [
        pltpu.VMEM((hp, MOBA_BLOCK, HEAD_DIM), BF16),
        pltpu.VMEM((hp, nb, 1, MOBA_BLOCK), F32),
        pltpu.VMEM((hp, 1, MOBA_BLOCK), F32),
        pltpu.VMEM((hp, 1, MOBA_BLOCK), F32),
        pltpu.VMEM((hp, HEAD_DIM, MOBA_BLOCK), F32),
        pltpu.VMEM((2, hp, grp, MOBA_BLOCK, MOBA_BLOCK), F32),
        pltpu.VMEM((2, hp, 1, MOBA_BLOCK), F32),
    ]


def _page_bf16(ref):
    page = ref.shape[0] // N_HEADS
    heads = [ref[pl.ds(h, page, stride=N_HEADS), :] for h in range(N_HEADS)]
    return jnp.concatenate(heads, axis=1).astype(BF16)


def _moba_sample_body(q_ref, kn_ref, vn_ref, k_pages, v_pages, o_ref, s_ref, *, t_new):
    n_pages = len(k_pages)
    page = k_pages[0].shape[0] // N_HEADS
    rows = t_new * N_HEADS
    n_blocks = n_pages * page // MOBA_BLOCK
    per_blk = MOBA_BLOCK // page

    qrep = jnp.concatenate(
        [jnp.broadcast_to(q_ref[0, t:t + 1, :], (N_HEADS, ATTN_WIDTH)) for t in range(t_new)], axis=0)
    r_i = lax.broadcasted_iota(jnp.int32, (rows, ATTN_WIDTH), 0)
    c_i = lax.broadcasted_iota(jnp.int32, (rows, ATTN_WIDTH), 1)
    diag = (c_i // HEAD_DIM) == (r_i % N_HEADS)
    qbd = jnp.where(diag, qrep * (HEAD_DIM ** -0.5), 0.0).astype(BF16)

    gcol = lax.broadcasted_iota(jnp.int32, (rows, LANES), 1).astype(F32)
    gate = jnp.full((rows, LANES), NEG, F32)
    for n in range(n_blocks):
        tot = jnp.zeros((rows, 1), F32)
        for pp in range(per_blk):
            p_i = n * per_blk + pp
            sp = _dot_nt(qbd, _page_bf16(k_pages[p_i]))
            s_ref[:, p_i * page:(p_i + 1) * page] = sp
            tot = tot + jnp.sum(sp, axis=-1, keepdims=True)
        gate = jnp.where(gcol == float(n), tot * (1.0 / MOBA_BLOCK), gate)
    sel = _top_mask(gate, gcol, min(MOBA_TOPK, n_blocks))

    s_own = _dot_nt(qbd, kn_ref[0].astype(BF16))
    oc = lax.broadcasted_iota(jnp.int32, s_own.shape, 1)
    orow = lax.broadcasted_iota(jnp.int32, s_own.shape, 0)
    s_own = jnp.where(oc <= orow // N_HEADS, s_own, NEG)

    m = jnp.max(s_own, axis=-1, keepdims=True)
    for n in range(n_blocks):
        sb = s_ref[:, n * MOBA_BLOCK:(n + 1) * MOBA_BLOCK]
        sb = jnp.where(sel[:, n:n + 1] > 0.5, sb, NEG)
        m = jnp.maximum(m, jnp.max(sb, axis=-1, keepdims=True))

    p_own = jnp.exp(s_own - m)
    l = jnp.sum(p_own, axis=-1, keepdims=True)
    acc = _dot(p_own.astype(BF16), vn_ref[0].astype(BF16))
    for p_i in range(n_pages):
        n = p_i // per_blk
        sp = s_ref[:, p_i * page:(p_i + 1) * page]
        pp = jnp.where(sel[:, n:n + 1] > 0.5, jnp.exp(sp - m), 0.0)
        l = l + jnp.sum(pp, axis=-1, keepdims=True)
        acc = acc + _dot(pp.astype(BF16), _page_bf16(v_pages[p_i]))
    o = jnp.where(diag, acc * (1.0 / l), 0.0)
    for t in range(t_new):
        o_ref[0, t:t + 1, :] = jnp.sum(o[t * N_HEADS:(t + 1) * N_HEADS, :], axis=0, keepdims=True)


def _moba_kernel(pt_ref, q_ref, k_ref, vt_ref, km_ref, qs_ref, kn_ref, vn_ref, *refs, nb, hp, grp, n_pages, t_new):
    k_pages = refs[:n_pages]
    v_pages = refs[n_pages:2 * n_pages]
    o_ref, os_ref = refs[2 * n_pages:2 * n_pages + 2]
    prompt_scratch = refs[2 * n_pages + 2:-1]
    _moba_prompt_body(pl.program_id(0) % nb, q_ref, k_ref, vt_ref, km_ref, o_ref, *prompt_scratch,
                      nb=nb, hp=hp, grp=grp)
    _moba_sample_body(qs_ref, kn_ref, vn_ref, k_pages, v_pages, os_ref, refs[-1], t_new=t_new)


def moba_attention(q, k_bf, v_bf, kmean, q_s, k_new, v_new, cache_k, cache_v, page_table):
    s = q.shape[0]
    nb = s // MOBA_BLOCK
    hp = ATT_HEADS
    grp = ATT_GROUP if nb % ATT_GROUP == 0 else 1
    wide = hp * HEAD_DIM
    vt = v_bf.reshape(nb, MOBA_BLOCK, ATTN_WIDTH).transpose(0, 2, 1)
    db, t_new, _ = q_s.shape
    assert db == (N_HEADS // hp) * nb, "one sample row per prompt tile step"
    n_pages = page_table.shape[1]
    page = cache_k.shape[-3]
    ck = cache_k.reshape(-1, page * N_HEADS, HEAD_DIM)
    cv = cache_v.reshape(-1, page * N_HEADS, HEAD_DIM)
    tok = pl.BlockSpec((1, t_new, ATTN_WIDTH), lambda i, pt: (i, 0, 0))
    tile = pl.BlockSpec((MOBA_BLOCK, wide), lambda i, pt: (i % nb, i // nb))

    def page_spec(p):
        return pl.BlockSpec((None, page * N_HEADS, HEAD_DIM), lambda i, pt: (pt[i, p], 0, 0))

    grid_spec = pltpu.PrefetchScalarGridSpec(
        num_scalar_prefetch=1,
        grid=(db,),
        in_specs=[
            tile,
            pl.BlockSpec((s, wide), lambda i, pt: (0, i // nb)),
            pl.BlockSpec((nb, wide, MOBA_BLOCK), lambda i, pt: (0, i // nb, 0)),
            pl.BlockSpec((nb, wide), lambda i, pt: (0, i // nb)),
            tok, tok, tok,
        ] + [page_spec(p) for p in range(n_pages)] * 2,
        out_specs=(tile, tok),
        scratch_shapes=_prompt_scratch(nb, hp, grp) + [pltpu.VMEM((t_new * N_HEADS, n_pages * page), F32)],
    )
    return pl.pallas_call(
        functools.partial(_moba_kernel, nb=nb, hp=hp, grp=grp, n_pages=n_pages, t_new=t_new),
        grid_spec=grid_spec,
        out_shape=(jax.ShapeDtypeStruct((s, ATTN_WIDTH), BF16),
                   jax.ShapeDtypeStruct((db, t_new, ATTN_WIDTH), F32)),
        compiler_params=_cparams(("arbitrary",), ATTN_VMEM_LIMIT),
        name="moba_attention",
    )(page_table, q, k_bf, vt, kmean, q_s, k_new, v_new, *([ck] * n_pages), *([cv] * n_pages))


def _pool_prompt_kernel(u_ref, prev_ref, d_ref, ext_ref, *, tm):
    i = pl.program_id(0)
    pad = prev_ref.shape[0]
    prev = jnp.where(i > 0, prev_ref[...], 0.0)
    ext_ref[0:pad, :] = prev
    ext_ref[pad:pad + tm, :] = u_ref[...]
    pos = (i * tm + lax.broadcasted_iota(jnp.int32, (tm, 1), 0)).astype(F32)
    for g, w in enumerate(POOL_WINDOWS):
        c0, c1 = g * POOL_GROUP_W, (g + 1) * POOL_GROUP_W
        wsum = ext_ref[pad:pad + tm, c0:c1]
        for jj in range(1, w):
            wsum = wsum + ext_ref[pad - jj:pad - jj + tm, c0:c1]
        inv = 1.0 / jnp.minimum(jnp.float32(w), pos + 1.0)
        d_ref[:, c0:c1] = (wsum * inv - u_ref[:, c0:c1]).astype(d_ref.dtype)


def pool_prompt(u, *, tm):
    n = u.shape[0]
    pad = 16
    return pl.pallas_call(
        functools.partial(_pool_prompt_kernel, tm=tm),
        grid=(n // tm,),
        in_specs=[
            pl.BlockSpec((tm, POOL_WIDTH), lambda i: (i, 0)),
            pl.BlockSpec((pad, POOL_WIDTH), lambda i: (jnp.maximum(i * (tm // pad) - 1, 0), 0)),
        ],
        out_specs=pl.BlockSpec((tm, POOL_WIDTH), lambda i: (i, 0)),
        out_shape=jax.ShapeDtypeStruct((n, POOL_WIDTH), BF16),
        scratch_shapes=[pltpu.VMEM((tm + pad, POOL_WIDTH), F32)],
        compiler_params=_cparams(("arbitrary",)),
        name="pool_prompt",
    )(u, u)


def _pool_sample_kernel(u_ref, st_ref, d_ref, new_ref, *, t_new, start_pos):
    ext = [st_ref[:, r, :] for r in range(POOL_STATE)] + [u_ref[:, t, :] for t in range(t_new)]
    for t in range(t_new):
        e = POOL_STATE + t
        parts = []
        for g, w in enumerate(POOL_WINDOWS):
            c0, c1 = g * POOL_GROUP_W, (g + 1) * POOL_GROUP_W
            wsum = ext[e][:, c0:c1]
            for jj in range(1, w):
                wsum = wsum + ext[e - jj][:, c0:c1]
            count = min(float(w), float(start_pos + t) + 1.0)
            parts.append(wsum * (1.0 / count) - ext[e][:, c0:c1])
        d_ref[:, t, :] = jnp.concatenate(parts, axis=-1).astype(d_ref.dtype)
    for r in range(POOL_STATE):
        new_ref[:, r, :] = ext[t_new + r]


def pool_sample(u, state, *, start_pos, bb=32):
    db, t_new, _ = u.shape
    return pl.pallas_call(
        functools.partial(_pool_sample_kernel, t_new=t_new, start_pos=start_pos),
        grid=(db // bb,),
        in_specs=[
            pl.BlockSpec((bb, t_new, POOL_WIDTH), lambda i: (i, 0, 0)),
            pl.BlockSpec((bb, POOL_STATE, POOL_WIDTH), lambda i: (i, 0, 0)),
        ],
        out_specs=(
            pl.BlockSpec((bb, t_new, POOL_WIDTH), lambda i: (i, 0, 0)),
            pl.BlockSpec((bb, POOL_STATE, POOL_WIDTH), lambda i: (i, 0, 0)),
        ),
        out_shape=(
            jax.ShapeDtypeStruct((db, t_new, POOL_WIDTH), F32),
            jax.ShapeDtypeStruct((db, POOL_STATE, POOL_WIDTH), F32),
        ),
        compiler_params=_cparams(("arbitrary",)),
        name="pool_sample",
    )(u, state)


def _out_proj_kernel(x_ref, oa_ref, d_ref, wp_ref, ps_ref, wo_ref, gf_ref, wr_ref, br_ref,
                     h2_ref, rt_ref, mix_ref):
    mix_ref[:, 0:ATTN_WIDTH] = oa_ref[...].astype(BF16)
    dd = d_ref[...].astype(BF16)
    for g in range(len(POOL_WINDOWS)):
        c0, c1 = g * POOL_GROUP_W, (g + 1) * POOL_GROUP_W
        yg = _dot(dd[:, c0:c1], wp_ref[g].astype(BF16)) * ps_ref[:, c0:c1]
        mix_ref[:, ATTN_WIDTH + c0:ATTN_WIDTH + c1] = yg.astype(BF16)
    h2 = x_ref[...] + _dot(mix_ref[...], wo_ref[...])
    h2_ref[...] = h2
    ms = jnp.mean(h2 * h2, axis=-1, keepdims=True)
    hn = h2 * lax.rsqrt(ms + EPS) * gf_ref[...]

    hn_hi = hn.astype(BF16)
    hn_lo = (hn - hn_hi.astype(F32)).astype(BF16)
    t = _dot(hn_hi, wr_ref[...])
    logits = t[:, :LANES] + t[:, LANES:] + _dot(hn_lo, wr_ref[:, :LANES]) + br_ref[...]
    lane = lax.broadcasted_iota(jnp.int32, logits.shape, 1).astype(F32)
    far = jnp.float32(LANES)
    is_g = lane < N_EXPERT_GROUPS
    gl = jnp.where(is_g, logits, -jnp.inf)
    g_max = jnp.max(gl, axis=-1, keepdims=True)
    g_top = jnp.min(jnp.where(gl == g_max, lane, far), axis=-1, keepdims=True)
    g_p = 1.0 / jnp.sum(jnp.where(is_g, jnp.exp(gl - g_max), 0.0), axis=-1, keepdims=True)
    lo = N_EXPERT_GROUPS + g_top * EXPERTS_PER_GROUP
    in_grp = jnp.logical_and(lane >= lo, lane < lo + EXPERTS_PER_GROUP)
    el = jnp.where(in_grp, logits, -jnp.inf)
    e1 = jnp.max(el, axis=-1, keepdims=True)
    i1 = jnp.min(jnp.where(el == e1, lane, far), axis=-1, keepdims=True)
    el2 = jnp.where(lane == i1, -jnp.inf, el)
    e2 = jnp.max(el2, axis=-1, keepdims=True)
    i2 = jnp.min(jnp.where(el2 == e2, lane, far), axis=-1, keepdims=True)
    ex2 = jnp.exp(e2 - e1)
    den = 1.0 + ex2
    w1 = (1.0 / den) * g_p
    w2 = (ex2 / den) * g_p
    rt_ref[...] = jnp.where(lane == 0.0, i1 - N_EXPERT_GROUPS,
                            jnp.where(lane == 1.0, i2 - N_EXPERT_GROUPS,
                                      jnp.where(lane == 2.0, w1, jnp.where(lane == 3.0, w2, 0.0))))


def out_proj(x, o_attn, d, w_pool, pool_scale, w_out_bf, g_ffn, w_router, b_router, *, tm):
    n = x.shape[0]
    row = lambda i: (i, 0)
    const = lambda i: (0, 0)
    return pl.pallas_call(
        _out_proj_kernel,
        grid=(n // tm,),
        in_specs=[
            pl.BlockSpec((tm, D_MODEL), row),
            pl.BlockSpec((tm, ATTN_WIDTH), row),
            pl.BlockSpec((tm, POOL_WIDTH), row),
            pl.BlockSpec(w_pool.shape, lambda i: (0, 0, 0)),
            pl.BlockSpec((1, POOL_WIDTH), const),
            pl.BlockSpec((D_MODEL, D_MODEL), const),
            pl.BlockSpec((1, D_MODEL), const),
            pl.BlockSpec((D_MODEL, 2 * LANES), const),
            pl.BlockSpec((1, LANES), const),
        ],
        out_specs=(
            pl.BlockSpec((tm, D_MODEL), row),
            pl.BlockSpec((tm, LANES), row),
        ),
        out_shape=(
            jax.ShapeDtypeStruct((n, D_MODEL), F32),
            jax.ShapeDtypeStruct((n, LANES), F32),
        ),
        scratch_shapes=[pltpu.VMEM((tm, D_MODEL), BF16)],
        compiler_params=_cparams(("arbitrary",)),
        name="out_proj",
    )(x, o_attn, d, w_pool, pool_scale, w_out_bf, g_ffn, w_router, b_router)


MOE_TILE = 512
ROW_TILE = 256


def _route_kernel(rt_ref, pos_ref, meta_ref, *, n_tiles):
    lane = lax.broadcasted_iota(jnp.int32, (ROW_TILE, LANES), 1).astype(F32)
    r_i = lax.broadcasted_iota(jnp.int32, (ROW_TILE, ROW_TILE), 0)
    c_i = lax.broadcasted_iota(jnp.int32, (ROW_TILE, ROW_TILE), 1)
    tri = jnp.where(c_i < r_i, 1.0, 0.0).astype(BF16)

    def one_hot(t):
        rt = rt_ref[pl.ds(pl.multiple_of(t * ROW_TILE, ROW_TILE), ROW_TILE), :]
        e1, e2 = rt[:, 0:1], rt[:, 1:2]
        return e1, e2, jnp.where(jnp.logical_or(lane == e1, lane == e2), 1.0, 0.0)

    def count(t, cnt):
        return cnt + jnp.sum(one_hot(t)[2], axis=0, keepdims=True)

    cnt = lax.fori_loop(0, n_tiles, count, jnp.zeros((1, LANES), F32))
    tiles_per = jnp.floor((cnt + (MOE_TILE - 1)) * (1.0 / MOE_TILE))
    e_r = lax.broadcasted_iota(jnp.int32, (LANES, LANES), 0)
    e_c = lax.broadcasted_iota(jnp.int32, (LANES, LANES), 1)
    upper = jnp.where(e_r < e_c, 1.0, 0.0).astype(BF16)
    off_tiles = _dot(jnp.broadcast_to(tiles_per, (8, LANES)).astype(BF16), upper)[0:1]
    base = off_tiles * MOE_TILE

    def place(t, run):
        e1, e2, oh = one_hot(t)
        dest = base + run + _dot(tri, oh.astype(BF16))
        p1 = jnp.sum(jnp.where(lane == e1, dest, 0.0), axis=1, keepdims=True)
        p2 = jnp.sum(jnp.where(lane == e2, dest, 0.0), axis=1, keepdims=True)
        pos = jnp.where(lane == 0.0, p1, jnp.where(lane == 1.0, p2, 0.0))
        pos_ref[pl.ds(pl.multiple_of(t * ROW_TILE, ROW_TILE), ROW_TILE), :] = pos.astype(jnp.int32)
        return run + jnp.sum(oh, axis=0, keepdims=True)

    lax.fori_loop(0, n_tiles, place, jnp.zeros((1, LANES), F32))
    row = lax.broadcasted_iota(jnp.int32, (8, LANES), 0)
    meta_ref[...] = jnp.where(row == 0, tiles_per, jnp.where(row == 1, off_tiles, 0.0))


def route(rt):
    n = rt.shape[0]
    assert n % ROW_TILE == 0
    return pl.pallas_call(
        functools.partial(_route_kernel, n_tiles=n // ROW_TILE),
        out_shape=(
            jax.ShapeDtypeStruct((n, LANES), jnp.int32),
            jax.ShapeDtypeStruct((8, LANES), F32),
        ),
        compiler_params=pltpu.CompilerParams(vmem_limit_bytes=VMEM_LIMIT),
        name="route",
    )(rt)


ROW_UNROLL = 8


def _dispatch_kernel(p1_ref, p2_ref, zt_ref, xa_ref, xb_ref, xs_ref, zero_ref, sem, zsem, *, tm, tiles_a, n_zero):
    i = pl.program_id(0)

    @pl.when(i == 0)
    def _():
        zero_ref[...] = jnp.zeros(zero_ref.shape, zero_ref.dtype)

        def zcopy(z):
            row0 = pl.multiple_of(zt_ref[z] * MOE_TILE, MOE_TILE)
            return pltpu.make_async_copy(zero_ref, xs_ref.at[pl.ds(row0, MOE_TILE)], zsem)

        for z in range(n_zero):
            @pl.when(zt_ref[z] >= 0)
            def _():
                zcopy(z).start()
        for z in range(n_zero):
            @pl.when(zt_ref[z] >= 0)
            def _():
                zcopy(z).wait()

    def scatter(x_ref):
        t0 = i * tm

        def issue(r, c):
            src = x_ref.at[pl.ds(r, 1)]
            pltpu.make_async_copy(src, xs_ref.at[pl.ds(p1_ref[t0 + r], 1)], sem).start()
            pltpu.make_async_copy(src, xs_ref.at[pl.ds(p2_ref[t0 + r], 1)], sem).start()
            return c

        lax.fori_loop(0, tm, issue, 0, unroll=ROW_UNROLL)
        for _ in range(2):
            pltpu.make_async_copy(x_ref, xs_ref.at[pl.ds(0, tm)], sem).wait()

    @pl.when(i < tiles_a)
    def _():
        scatter(xa_ref)

    @pl.when(i >= tiles_a)
    def _():
        scatter(xb_ref)


def dispatch(p1, p2, zero_tiles, xa, xb, rows, *, tm):
    tiles_a, tiles_b = xa.shape[0] // tm, xb.shape[0] // tm
    grid_spec = pltpu.PrefetchScalarGridSpec(
        num_scalar_prefetch=3,
        grid=(tiles_a + tiles_b,),
        in_specs=[
            pl.BlockSpec((tm, D_MODEL), lambda i, a, b, z: (jnp.minimum(i, tiles_a - 1), 0)),
            pl.BlockSpec((tm, D_MODEL), lambda i, a, b, z: (jnp.maximum(i - tiles_a, 0), 0)),
        ],
        out_specs=pl.BlockSpec(memory_space=pl.ANY),
        scratch_shapes=[pltpu.VMEM((MOE_TILE, D_MODEL), F32), pltpu.SemaphoreType.DMA(()), pltpu.SemaphoreType.DMA(())],
    )
    return pl.pallas_call(
        functools.partial(_dispatch_kernel, tm=tm, tiles_a=tiles_a, n_zero=zero_tiles.shape[0]),
        grid_spec=grid_spec,
        out_shape=jax.ShapeDtypeStruct((rows, D_MODEL), F32),
        compiler_params=_cparams(("arbitrary",)),
        name="dispatch",
    )(p1, p2, zero_tiles, xa, xb)


def _experts_kernel(te_ref, ts_ref, nu_ref, x_ref, gf_ref, wg_ref, wu_ref, wd_ref, o_ref):
    @pl.when(pl.program_id(0) < nu_ref[0])
    def _():
        h = x_ref[...]
        ms = jnp.mean(h * h, axis=-1, keepdims=True)
        x = (h * lax.rsqrt(ms + EPS) * gf_ref[...]).astype(BF16).astype(F32)
        a = _dot(x, wg_ref[0])
        b = _dot(x, wu_ref[0])
        act = (a * (1.0 / (1.0 + jnp.exp(-a)))) * b
        o_ref[...] = _dot(act.astype(BF16).astype(F32), wd_ref[0])

    @pl.when(pl.program_id(0) >= nu_ref[0])
    def _():
        o_ref[...] = jnp.zeros(o_ref.shape, o_ref.dtype)


def experts(tile_expert, tile_src, n_used, xs, g_ffn, w_gate, w_up, w_down, *, layer):
    rows, w = xs.shape
    wspec = lambda shape: pl.BlockSpec((None, 1) + shape, lambda j, te, ts, nu: (layer, te[j], 0, 0))
    grid_spec = pltpu.PrefetchScalarGridSpec(
        num_scalar_prefetch=3,
        grid=(rows // MOE_TILE,),
        in_specs=[
            pl.BlockSpec((MOE_TILE, w), lambda j, te, ts, nu: (ts[j], 0)),
            pl.BlockSpec((1, D_MODEL), lambda j, te, ts, nu: (0, 0)),
            wspec((D_MODEL, D_EXPERT)), wspec((D_MODEL, D_EXPERT)), wspec((D_EXPERT, D_MODEL)),
        ],
        out_specs=pl.BlockSpec((MOE_TILE, D_MODEL), lambda j, te, ts, nu: (j, 0)),
    )
    return pl.pallas_call(
        _experts_kernel,
        grid_spec=grid_spec,
        out_shape=jax.ShapeDtypeStruct((rows, D_MODEL), F32),
        compiler_params=_cparams(("arbitrary",), EXPERT_VMEM_LIMIT),
        name="experts",
    )(tile_expert, tile_src, n_used, xs, g_ffn, w_gate, w_up, w_down)


def _combine_kernel(p1_ref, p2_ref, h2_ref, rt_ref, os_ref, y_ref, buf_ref, sem, *, tm):
    t0 = pl.program_id(0) * tm

    def issue(r, c):
        pltpu.make_async_copy(os_ref.at[pl.ds(p1_ref[t0 + r], 1)], buf_ref.at[0, pl.ds(r, 1)], sem).start()
        pltpu.make_async_copy(os_ref.at[pl.ds(p2_ref[t0 + r], 1)], buf_ref.at[1, pl.ds(r, 1)], sem).start()
        return c

    lax.fori_loop(0, tm, issue, 0, unroll=ROW_UNROLL)
    for s in range(2):
        pltpu.make_async_copy(os_ref.at[pl.ds(0, tm)], buf_ref.at[s], sem).wait()
    rt = rt_ref[...]
    y_ref[...] = h2_ref[...] + rt[:, 2:3] * buf_ref[0] + rt[:, 3:4] * buf_ref[1]


def combine(p1, p2, h2, rt, os, *, tm):
    n = h2.shape[0]
    row = lambda i, a, b: (i, 0)
    grid_spec = pltpu.PrefetchScalarGridSpec(
        num_scalar_prefetch=2,
        grid=(n // tm,),
        in_specs=[pl.BlockSpec((tm, D_MODEL), row), pl.BlockSpec((tm, LANES), row),
                  pl.BlockSpec(memory_space=pl.ANY)],
        out_specs=pl.BlockSpec((tm, D_MODEL), row),
        scratch_shapes=[pltpu.VMEM((2, tm, D_MODEL), F32), pltpu.SemaphoreType.DMA(())],
    )
    return pl.pallas_call(
        functools.partial(_combine_kernel, tm=tm),
        grid_spec=grid_spec,
        out_shape=jax.ShapeDtypeStruct((n, D_MODEL), F32),
        compiler_params=_cparams(("arbitrary",)),
        name="combine",
    )(p1, p2, h2, rt, os)


def kernel(x_prompt, x_sample, cache_k, cache_v, state_pool, page_table, g_mix, w_in, g_q, g_k, w_pool, pool_scale, w_out, g_ffn, w_group_router, b_group_router, w_expert_router, b_expert_router, w_gate, w_up, w_down):
    B, S, _ = x_prompt.shape
    DB, T, _ = x_sample.shape
    depth = w_in.shape[0]
    assert B == 1 and depth == 1
    past_len = page_table.shape[1] * cache_k.shape[2]
    l = 0

    w_in_bf = w_in[l].astype(BF16)
    w_out_bf = w_out[l].astype(BF16)
    gm, gq, gk, gf = g_mix[l][None], g_q[l][None], g_k[l][None], g_ffn[l][None]
    ps = pool_scale[l][None]
    n_r = N_EXPERT_GROUPS + N_EXPERTS
    w_router = jnp.concatenate([w_group_router[l], w_expert_router[l].reshape(D_MODEL, N_EXPERTS)], axis=1)
    w_router = jnp.pad(w_router, ((0, 0), (0, LANES - n_r)))
    w_router_hi = w_router.astype(BF16)
    w_router = jnp.concatenate([w_router_hi, (w_router - w_router_hi.astype(F32)).astype(BF16)], axis=1)
    b_router = jnp.concatenate([b_group_router[l], b_expert_router[l].reshape(N_EXPERTS)])
    b_router = jnp.pad(b_router, (0, LANES - n_r))[None]

    def mixer_tail(x2d, o_attn, d):
        return out_proj(x2d, o_attn, d, w_pool[l], ps, w_out_bf, gf, w_router, b_router, tm=512)

    xp = x_prompt.reshape(S, D_MODEL)
    n_s = DB * T
    xs = x_sample.reshape(n_s, D_MODEL)
    q_p, k_p, kb_p, km_p, v_p, vb_p, u_p = in_proj(xp, gm, w_in_bf, gq, gk, tm=512)
    q_s, k_s, _, _, v_s, _, u_s = in_proj(xs, gm, w_in_bf, gq, gk, tm=n_s)
    r3 = lambda a: a.reshape(DB, T, ATTN_WIDTH)
    o_p, o_s = moba_attention(q_p, kb_p, vb_p, km_p.reshape(S // MOBA_BLOCK, ATTN_WIDTH),
                              r3(q_s), r3(k_s), r3(v_s), cache_k, cache_v, page_table + l * cache_k.shape[1])
    d_p = pool_prompt(u_p, tm=512)
    d_s, pool_s = pool_sample(u_s.reshape(DB, T, POOL_WIDTH), state_pool[l], start_pos=past_len)
    h2_p, rt_p = mixer_tail(xp, o_p, d_p)
    h2_s, rt_s = mixer_tail(xs, o_s.reshape(n_s, ATTN_WIDTH), d_s.reshape(n_s, POOL_WIDTH))

    pos, meta = route(jnp.concatenate([rt_p, rt_s], axis=0))
    p1, p2 = pos[:, 0], pos[:, 1]
    max_tiles = -(-2 * (S + n_s) // MOE_TILE) + N_EXPERTS
    tiles_per = meta[0, :N_EXPERTS].astype(jnp.int32)
    ends = tiles_per + meta[1, :N_EXPERTS].astype(jnp.int32)
    n_used = ends[N_EXPERTS - 1]
    tile_src = jnp.minimum(jnp.arange(max_tiles, dtype=jnp.int32), n_used - 1)
    tile_expert = jnp.minimum(jnp.sum(tile_src[:, None] >= ends[None, :], axis=1), N_EXPERTS - 1).astype(jnp.int32)
    tail = n_used + jnp.arange(N_EXPERTS, dtype=jnp.int32)
    zero_tiles = jnp.concatenate([jnp.where(tiles_per > 0, ends - 1, -1), jnp.where(tail < max_tiles, tail, -1)])
    x_sorted = dispatch(p1, p2, zero_tiles, h2_p, h2_s, max_tiles * MOE_TILE, tm=ROW_TILE)
    o_sorted = experts(tile_expert, tile_src, n_used[None], x_sorted, gf, w_gate, w_up, w_down, layer=l)
    y_prompt = combine(p1[:S], p2[:S], h2_p, rt_p, o_sorted, tm=ROW_TILE).reshape(B, S, D_MODEL)
    y_sample = combine(p1[S:], p2[S:], h2_s, rt_s, o_sorted, tm=ROW_TILE).reshape(DB, T, D_MODEL)

    hd = (N_HEADS, HEAD_DIM)
    return (
        y_prompt,
        y_sample,
        k_p.reshape(1, B, S, *hd),
        v_p.reshape(1, B, S, *hd),
        u_p[S - POOL_STATE:].reshape(1, B, POOL_STATE, POOL_WIDTH),
        k_s.reshape(1, DB, T, *hd),
        v_s.reshape(1, DB, T, *hd),
        pool_s.reshape(1, DB, POOL_STATE, POOL_WIDTH),
    )
```

```python
import functools

import jax
import jax.numpy as jnp
from jax import lax
from jax.experimental import pallas as pl
from jax.experimental.pallas import tpu as pltpu

D_MODEL = 2048
ATTN_WIDTH = 1024
POOL_WIDTH = 1024
HEAD_DIM = 128
N_HEADS = 8
POOL_WINDOWS = (2, 4, 8, 16)
POOL_GROUP_W = 256
POOL_STATE = 15
MOBA_BLOCK = 256
MOBA_TOPK = 3
N_EXPERT_GROUPS = 4
EXPERTS_PER_GROUP = 4
N_EXPERTS = 16
D_EXPERT = 768
EPS = 1e-6
NEG = -1e30
LANES = 128
VMEM_LIMIT = 56 * 1024 * 1024
ATTN_VMEM_LIMIT = 60 * 1024 * 1024
EXPERT_VMEM_LIMIT = 60 * 1024 * 1024

BF16 = jnp.bfloat16
F32 = jnp.float32


def _cparams(sem, vmem_limit=VMEM_LIMIT):
    return pltpu.CompilerParams(dimension_semantics=sem, vmem_limit_bytes=vmem_limit)


def _dot(a, b):
    return jnp.dot(a, b, preferred_element_type=F32)


def _dot_nt(a, b, precision=None):
    return lax.dot_general(a, b, (((1,), (1,)), ((), ())), precision=precision,
                           preferred_element_type=F32)


def _top_mask(g, ids, k, axis=-1):
    sel = jnp.zeros(g.shape, F32)
    for _ in range(k):
        m = jnp.max(g, axis=axis, keepdims=True)
        idx = jnp.min(jnp.where(g == m, ids, jnp.float32(g.shape[axis])), axis=axis, keepdims=True)
        pick = ids == idx
        sel = jnp.where(pick, 1.0, sel)
        g = jnp.where(pick, -jnp.inf, g)
    return sel


def _in_proj_kernel(x_ref, gmix_ref, w_ref, gq_ref, gk_ref,
                    q_ref, k_ref, kb_ref, km_ref, v_ref, vb_ref, u_ref, hb_ref, *, tm):
    xf = x_ref[...]
    ms = jnp.mean(xf * xf, axis=-1, keepdims=True)
    hb_ref[...] = (xf * lax.rsqrt(ms + EPS) * gmix_ref[...]).astype(BF16)

    def head_norm(z, out_ref, g_ref):
        for h in range(N_HEADS):
            zh = z[:, h * HEAD_DIM:(h + 1) * HEAD_DIM]
            ms = jnp.mean(zh * zh, axis=-1, keepdims=True)
            out_ref[:, h * HEAD_DIM:(h + 1) * HEAD_DIM] = zh * lax.rsqrt(ms + EPS) * g_ref[...]

    head_norm(_dot(hb_ref[...], w_ref[:, 0:ATTN_WIDTH]), q_ref, gq_ref)
    head_norm(_dot(hb_ref[...], w_ref[:, ATTN_WIDTH:2 * ATTN_WIDTH]), k_ref, gk_ref)
    kb_ref[...] = k_ref[...].astype(BF16)
    for g in range(tm // MOBA_BLOCK):
        blk = k_ref[g * MOBA_BLOCK:(g + 1) * MOBA_BLOCK, :]
        km_ref[g] = jnp.mean(blk, axis=0, keepdims=True)
    zv = _dot(hb_ref[...], w_ref[:, 2 * ATTN_WIDTH:3 * ATTN_WIDTH])
    v_ref[...] = zv
    vb_ref[...] = zv.astype(BF16)
    u_ref[...] = _dot(hb_ref[...], w_ref[:, 3 * ATTN_WIDTH:])


def in_proj(x, g_mix, w_in_bf, g_q, g_k, *, tm):
    n = x.shape[0]
    wide = lambda i: (i, 0)
    const = lambda i: (0, 0)
    out_shape = (
        jax.ShapeDtypeStruct((n, ATTN_WIDTH), F32),
        jax.ShapeDtypeStruct((n, ATTN_WIDTH), F32),
        jax.ShapeDtypeStruct((n, ATTN_WIDTH), BF16),
        jax.ShapeDtypeStruct((n // MOBA_BLOCK, 1, ATTN_WIDTH), F32),
        jax.ShapeDtypeStruct((n, ATTN_WIDTH), F32),
        jax.ShapeDtypeStruct((n, ATTN_WIDTH), BF16),
        jax.ShapeDtypeStruct((n, POOL_WIDTH), F32),
    )
    blk = pl.BlockSpec((tm, ATTN_WIDTH), wide)
    return pl.pallas_call(
        functools.partial(_in_proj_kernel, tm=tm),
        grid=(n // tm,),
        in_specs=[
            pl.BlockSpec((tm, D_MODEL), wide),
            pl.BlockSpec((1, D_MODEL), const),
            pl.BlockSpec(w_in_bf.shape, const, pipeline_mode=pl.Buffered(1)),
            pl.BlockSpec((1, HEAD_DIM), const),
            pl.BlockSpec((1, HEAD_DIM), const),
        ],
        out_specs=(blk, blk, blk,
                   pl.BlockSpec((tm // MOBA_BLOCK, 1, ATTN_WIDTH), lambda i: (i, 0, 0)),
                   blk, blk, blk),
        out_shape=out_shape,
        scratch_shapes=[pltpu.VMEM((tm, D_MODEL), BF16)],
        compiler_params=_cparams(("arbitrary",)),
        name="in_proj",
    )(x, g_mix, w_in_bf, g_q, g_k)


LOG2E = 1.4426950408889634
ATT_HEADS = 2
ATT_GROUP = 4


def _moba_prompt_prologue(own, q_ref, k_ref, vt_ref, km_ref, qs_ref, bias_ref, m_ref, l_ref, acc_ref, *, nb, hp):
    base = pl.multiple_of(own * MOBA_BLOCK, MOBA_BLOCK)
    blk_id = lax.broadcasted_iota(jnp.int32, (nb, MOBA_BLOCK), 0).astype(F32)
    valid = blk_id < own.astype(F32)
    key_i = lax.broadcasted_iota(jnp.int32, (MOBA_BLOCK, MOBA_BLOCK), 0)
    qry_i = lax.broadcasted_iota(jnp.int32, (MOBA_BLOCK, MOBA_BLOCK), 1)

    for hh in range(hp):
        cols = slice(hh * HEAD_DIM, (hh + 1) * HEAD_DIM)
        q = q_ref[:, cols]
        gate = _dot_nt(km_ref[:, cols], q, precision=lax.Precision.HIGHEST)
        sel = _top_mask(jnp.where(valid, gate, NEG), blk_id, MOBA_TOPK, axis=0)
        bias = jnp.where(jnp.logical_and(sel > 0.5, valid), 0.0, NEG)
        for n in range(nb):
            bias_ref[hh, n] = bias[n:n + 1, :]
        qs = (q * (HEAD_DIM ** -0.5 * LOG2E)).astype(BF16)
        qs_ref[hh] = qs
        s = _dot_nt(k_ref[pl.ds(base, MOBA_BLOCK), cols], qs)
        s = jnp.where(key_i <= qry_i, s, NEG)
        m0 = jnp.max(s, axis=0, keepdims=True)
        p = jnp.exp2(s - m0)
        m_ref[hh] = m0
        l_ref[hh] = jnp.sum(p, axis=0, keepdims=True)
        acc_ref[hh] = _dot(vt_ref[own, cols, :], p.astype(BF16))


def _moba_prompt_main(own, k_ref, vt_ref, o_ref, qs_ref, bias_ref, m_ref, l_ref, acc_ref, s_ref, mb_ref,
                      *, nb, hp, grp):
    n_iter = (own + grp - 1) // grp

    def scores(i, slot):
        for hh in range(hp):
            cols = slice(hh * HEAD_DIM, (hh + 1) * HEAD_DIM)
            qs = qs_ref[hh]
            mb = None
            for gg in range(grp):
                n = jnp.minimum(i * grp + gg, nb - 1)
                off = pl.multiple_of(n * MOBA_BLOCK, MOBA_BLOCK)
                s = _dot_nt(k_ref[pl.ds(off, MOBA_BLOCK), cols], qs) + bias_ref[hh, n]
                s_ref[slot, hh, gg] = s
                smax = jnp.max(s, axis=0, keepdims=True)
                mb = smax if mb is None else jnp.maximum(mb, smax)
            mb_ref[slot, hh] = mb

    def softmax_pv(i, slot):
        for hh in range(hp):
            cols = slice(hh * HEAD_DIM, (hh + 1) * HEAD_DIM)
            m_prev = m_ref[hh]
            m_new = jnp.maximum(m_prev, mb_ref[slot, hh])
            alpha = jnp.exp2(m_prev - m_new)
            l_new = alpha * l_ref[hh]
            acc = alpha * acc_ref[hh]
            for gg in range(grp):
                p = jnp.exp2(s_ref[slot, hh, gg] - m_new)
                l_new = l_new + jnp.sum(p, axis=0, keepdims=True)
                acc = acc + _dot(vt_ref[i * grp + gg, cols, :], p.astype(BF16))
            m_ref[hh] = m_new
            l_ref[hh] = l_new
            acc_ref[hh] = acc

    scores(0, 0)

    def body(k, carry):
        scores(2 * k + 1, 1)
        softmax_pv(2 * k, 0)
        scores(2 * k + 2, 0)
        softmax_pv(2 * k + 1, 1)
        return carry

    lax.fori_loop(0, n_iter // 2, body, 0)

    @pl.when(n_iter % 2 == 1)
    def _():
        softmax_pv(n_iter - 1, 0)

    for hh in range(hp):
        o = acc_ref[hh] * (1.0 / l_ref[hh])
        o_ref[:, hh * HEAD_DIM:(hh + 1) * HEAD_DIM] = o.T.astype(o_ref.dtype)


def _prompt_scratch(nb, hp, grp):
    return [
        pltpu.VMEM((hp, MOBA_BLOCK, HEAD_DIM), BF16),
        pltpu.VMEM((hp, nb, 1, MOBA_BLOCK), F32),
        pltpu.VMEM((hp, 1, MOBA_BLOCK), F32),
        pltpu.VMEM((hp, 1, MOBA_BLOCK), F32),
        pltpu.VMEM((hp, HEAD_DIM, MOBA_BLOCK), F32),
        pltpu.VMEM((2, hp, grp, MOBA_BLOCK, MOBA_BLOCK), F32),
        pltpu.VMEM((2, hp, 1, MOBA_BLOCK), F32),
    ]


def _moba_sample_body(q_ref, kn_ref, vn_ref, k_pages, v_pages, o_ref, s_ref, *, t_new, between):
    n_pages = len(k_pages)
    pw = k_pages[0].shape[0]
    page = pw // N_HEADS
    rows = t_new * N_HEADS
    per_blk = MOBA_BLOCK // page
    n_blocks = n_pages // per_blk

    qa = (q_ref[0] * (HEAD_DIM ** -0.5)).astype(BF16)
    r_i = lax.broadcasted_iota(jnp.int32, (rows, pw), 0)
    c_i = lax.broadcasted_iota(jnp.int32, (rows, pw), 1)
    same_head = (c_i % N_HEADS) == (r_i % N_HEADS)

    gcol = lax.broadcasted_iota(jnp.int32, (rows, LANES), 1).astype(F32)
    gate = jnp.full((rows, LANES), NEG, F32)
    bmax = jnp.full((rows, LANES), NEG, F32)
    for n in range(n_blocks):
        ssum = jnp.zeros((rows, pw), F32)
        smax = jnp.full((rows, pw), NEG, F32)
        for pp in range(per_blk):
            p_i = n * per_blk + pp
            sp = _dot_nt(qa, k_pages[p_i][...].astype(BF16))
            s_ref[:, p_i * pw:(p_i + 1) * pw] = sp
            ssum = ssum + jnp.where(same_head, sp, 0.0)
            smax = jnp.maximum(smax, jnp.where(same_head, sp, NEG))
        tot = jnp.sum(ssum, axis=-1, keepdims=True) * (1.0 / MOBA_BLOCK)
        gate = jnp.where(gcol == float(n), tot, gate)
        bmax = jnp.where(gcol == float(n), jnp.max(smax, axis=-1, keepdims=True), bmax)
    sel = _top_mask(gate, gcol, min(MOBA_TOPK, n_blocks))
    between()

    s_own = _dot_nt(qa, kn_ref[0].astype(BF16))
    orow = lax.broadcasted_iota(jnp.int32, s_own.shape, 0)
    ocol = lax.broadcasted_iota(jnp.int32, s_own.shape, 1)
    own_ok = jnp.logical_and(ocol % N_HEADS == orow % N_HEADS, ocol // N_HEADS <= orow // N_HEADS)
    s_own = jnp.where(own_ok, s_own, NEG)

    m = jnp.maximum(jnp.max(s_own, axis=-1, keepdims=True),
                    jnp.max(jnp.where(sel > 0.5, bmax, NEG), axis=-1, keepdims=True))
    p_own = jnp.exp(s_own - m)
    l = jnp.sum(p_own, axis=-1, keepdims=True)
    acc = _dot(p_own.astype(BF16), vn_ref[0].astype(BF16))
    lsum = jnp.zeros((rows, pw), F32)
    for p_i in range(n_pages):
        n = p_i // per_blk
        sp = s_ref[:, p_i * pw:(p_i + 1) * pw]
        keep = jnp.logical_and(same_head, sel[:, n:n + 1] > 0.5)
        pp = jnp.where(keep, jnp.exp(sp - m), 0.0)
        lsum = lsum + pp
        acc = acc + _dot(pp.astype(BF16), v_pages[p_i][...].astype(BF16))
    l = l + jnp.sum(lsum, axis=-1, keepdims=True)
    o_ref[0] = acc * (1.0 / l)


def _moba_kernel(pt_ref, q_ref, k_ref, vt_ref, km_ref, qs_ref, kn_ref, vn_ref, *refs, nb, hp, grp, n_pages, t_new):
    k_pages = refs[:n_pages]
    v_pages = refs[n_pages:2 * n_pages]
    o_ref, os_ref = refs[2 * n_pages:2 * n_pages + 2]
    qsc_ref, bias_ref, m_ref, l_ref, acc_ref, s_ref, mb_ref = refs[2 * n_pages + 2:-1]
    own = pl.program_id(0) % nb

    def prompt_prologue():
        _moba_prompt_prologue(own, q_ref, k_ref, vt_ref, km_ref, qsc_ref, bias_ref, m_ref, l_ref, acc_ref,
                              nb=nb, hp=hp)

    _moba_sample_body(qs_ref, kn_ref, vn_ref, k_pages, v_pages, os_ref, refs[-1], t_new=t_new,
                      between=prompt_prologue)
    _moba_prompt_main(own, k_ref, vt_ref, o_ref, qsc_ref, bias_ref, m_ref, l_ref, acc_ref, s_ref, mb_ref,
                      nb=nb, hp=hp, grp=grp)


def moba_attention(q, k_bf, v_bf, kmean, q_s, k_new, v_new, cache_k, cache_v, page_table):
    s = q.shape[0]
    nb = s // MOBA_BLOCK
    hp = ATT_HEADS
    grp = ATT_GROUP if nb % ATT_GROUP == 0 else 1
    wide = hp * HEAD_DIM
    vt = v_bf.reshape(nb, MOBA_BLOCK, ATTN_WIDTH).transpose(0, 2, 1)
    db, t_new, _ = q_s.shape
    th = t_new * N_HEADS
    q_s, k_new, v_new = (a.reshape(db, th, HEAD_DIM) for a in (q_s, k_new, v_new))
    assert db == (N_HEADS // hp) * nb, "one sample row per prompt tile step"
    n_pages = page_table.shape[1]
    page = cache_k.shape[-3]
    ck = cache_k.reshape(-1, page * N_HEADS, HEAD_DIM)
    cv = cache_v.reshape(-1, page * N_HEADS, HEAD_DIM)
    tok = pl.BlockSpec((1, th, HEAD_DIM), lambda i, pt: (i, 0, 0))
    tile = pl.BlockSpec((MOBA_BLOCK, wide), lambda i, pt: (i % nb, i // nb))

    def page_spec(p):
        return pl.BlockSpec((None, page * N_HEADS, HEAD_DIM), lambda i, pt: (pt[i, p], 0, 0))

    grid_spec = pltpu.PrefetchScalarGridSpec(
        num_scalar_prefetch=1,
        grid=(db,),
        in_specs=[
            tile,
            pl.BlockSpec((s, wide), lambda i, pt: (0, i // nb)),
            pl.BlockSpec((nb, wide, MOBA_BLOCK), lambda i, pt: (0, i // nb, 0)),
            pl.BlockSpec((nb, wide), lambda i, pt: (0, i // nb)),
            tok, tok, tok,
        ] + [page_spec(p) for p in range(n_pages)] * 2,
        out_specs=(tile, tok),
        scratch_shapes=_prompt_scratch(nb, hp, grp) + [pltpu.VMEM((th, n_pages * page * N_HEADS), F32)],
    )
    o_p, o_s = pl.pallas_call(
        functools.partial(_moba_kernel, nb=nb, hp=hp, grp=grp, n_pages=n_pages, t_new=t_new),
        grid_spec=grid_spec,
        out_shape=(jax.ShapeDtypeStruct((s, ATTN_WIDTH), BF16),
                   jax.ShapeDtypeStruct((db, th, HEAD_DIM), F32)),
        compiler_params=_cparams(("arbitrary",), ATTN_VMEM_LIMIT),
        name="moba_attention",
    )(page_table, q, k_bf, vt, kmean, q_s, k_new, v_new, *([ck] * n_pages), *([cv] * n_pages))
    return o_p, o_s.reshape(db, t_new, ATTN_WIDTH)


def _pool_prompt_kernel(u_ref, prev_ref, d_ref, ext_ref, *, tm):
    i = pl.program_id(0)
    pad = prev_ref.shape[0]
    prev = jnp.where(i > 0, prev_ref[...], 0.0)
    ext_ref[0:pad, :] = prev
    ext_ref[pad:pad + tm, :] = u_ref[...]
    pos = (i * tm + lax.broadcasted_iota(jnp.int32, (tm, 1), 0)).astype(F32)
    for g, w in enumerate(POOL_WINDOWS):
        c0, c1 = g * POOL_GROUP_W, (g + 1) * POOL_GROUP_W
        wsum = ext_ref[pad:pad + tm, c0:c1]
        for jj in range(1, w):
            wsum = wsum + ext_ref[pad - jj:pad - jj + tm, c0:c1]
        inv = 1.0 / jnp.minimum(jnp.float32(w), pos + 1.0)
        d_ref[:, c0:c1] = (wsum * inv - u_ref[:, c0:c1]).astype(d_ref.dtype)


def pool_prompt(u, *, tm):
    n = u.shape[0]
    pad = 16
    return pl.pallas_call(
        functools.partial(_pool_prompt_kernel, tm=tm),
        grid=(n // tm,),
        in_specs=[
            pl.BlockSpec((tm, POOL_WIDTH), lambda i: (i, 0)),
            pl.BlockSpec((pad, POOL_WIDTH), lambda i: (jnp.maximum(i * (tm // pad) - 1, 0), 0)),
        ],
        out_specs=pl.BlockSpec((tm, POOL_WIDTH), lambda i: (i, 0)),
        out_shape=jax.ShapeDtypeStruct((n, POOL_WIDTH), BF16),
        scratch_shapes=[pltpu.VMEM((tm + pad, POOL_WIDTH), F32)],
        compiler_params=_cparams(("arbitrary",)),
        name="pool_prompt",
    )(u, u)


def _pool_sample_kernel(u_ref, st_ref, d_ref, new_ref, *, t_new, start_pos):
    ext = [st_ref[:, r, :] for r in range(POOL_STATE)] + [u_ref[:, t, :] for t in range(t_new)]
    for t in range(t_new):
        e = POOL_STATE + t
        parts = []
        for g, w in enumerate(POOL_WINDOWS):
            c0, c1 = g * POOL_GROUP_W, (g + 1) * POOL_GROUP_W
            wsum = ext[e][:, c0:c1]
            for jj in range(1, w):
                wsum = wsum + ext[e - jj][:, c0:c1]
            count = min(float(w), float(start_pos + t) + 1.0)
            parts.append(wsum * (1.0 / count) - ext[e][:, c0:c1])
        d_ref[:, t, :] = jnp.concatenate(parts, axis=-1).astype(d_ref.dtype)
    for r in range(POOL_STATE):
        new_ref[:, r, :] = ext[t_new + r]


def pool_sample(u, state, *, start_pos, bb=32):
    db, t_new, _ = u.shape
    return pl.pallas_call(
        functools.partial(_pool_sample_kernel, t_new=t_new, start_pos=start_pos),
        grid=(db // bb,),
        in_specs=[
            pl.BlockSpec((bb, t_new, POOL_WIDTH), lambda i: (i, 0, 0)),
            pl.BlockSpec((bb, POOL_STATE, POOL_WIDTH), lambda i: (i, 0, 0)),
        ],
        out_specs=(
            pl.BlockSpec((bb, t_new, POOL_WIDTH), lambda i: (i, 0, 0)),
            pl.BlockSpec((bb, POOL_STATE, POOL_WIDTH), lambda i: (i, 0, 0)),
        ),
        out_shape=(
            jax.ShapeDtypeStruct((db, t_new, POOL_WIDTH), F32),
            jax.ShapeDtypeStruct((db, POOL_STATE, POOL_WIDTH), F32),
        ),
        compiler_params=_cparams(("arbitrary",)),
        name="pool_sample",
    )(u, state)


def _out_proj_kernel(x_ref, oa_ref, d_ref, wp_ref, ps_ref, wo_ref, gf_ref, wr_ref, br_ref,
                     h2_ref, rt_ref, mix_ref):
    mix_ref[:, 0:ATTN_WIDTH] = oa_ref[...].astype(BF16)
    dd = d_ref[...].astype(BF16)
    for g in range(len(POOL_WINDOWS)):
        c0, c1 = g * POOL_GROUP_W, (g + 1) * POOL_GROUP_W
        yg = _dot(dd[:, c0:c1], wp_ref[g].astype(BF16)) * ps_ref[:, c0:c1]
        mix_ref[:, ATTN_WIDTH + c0:ATTN_WIDTH + c1] = yg.astype(BF16)
    h2 = x_ref[...] + _dot(mix_ref[...], wo_ref[...])
    h2_ref[...] = h2
    ms = jnp.mean(h2 * h2, axis=-1, keepdims=True)
    hn = h2 * lax.rsqrt(ms + EPS) * gf_ref[...]

    hn_hi = hn.astype(BF16)
    hn_lo = (hn - hn_hi.astype(F32)).astype(BF16)
    t = _dot(hn_hi, wr_ref[...])
    logits = t[:, :LANES] + t[:, LANES:] + _dot(hn_lo, wr_ref[:, :LANES]) + br_ref[...]
    lane = lax.broadcasted_iota(jnp.int32, logits.shape, 1).astype(F32)
    far = jnp.float32(LANES)
    is_g = lane < N_EXPERT_GROUPS
    gl = jnp.where(is_g, logits, -jnp.inf)
    g_max = jnp.max(gl, axis=-1, keepdims=True)
    g_top = jnp.min(jnp.where(gl == g_max, lane, far), axis=-1, keepdims=True)
    g_p = 1.0 / jnp.sum(jnp.where(is_g, jnp.exp(gl - g_max), 0.0), axis=-1, keepdims=True)
    lo = N_EXPERT_GROUPS + g_top * EXPERTS_PER_GROUP
    in_grp = jnp.logical_and(lane >= lo, lane < lo + EXPERTS_PER_GROUP)
    el = jnp.where(in_grp, logits, -jnp.inf)
    e1 = jnp.max(el, axis=-1, keepdims=True)
    i1 = jnp.min(jnp.where(el == e1, lane, far), axis=-1, keepdims=True)
    el2 = jnp.where(lane == i1, -jnp.inf, el)
    e2 = jnp.max(el2, axis=-1, keepdims=True)
    i2 = jnp.min(jnp.where(el2 == e2, lane, far), axis=-1, keepdims=True)
    ex2 = jnp.exp(e2 - e1)
    den = 1.0 + ex2
    w1 = (1.0 / den) * g_p
    w2 = (ex2 / den) * g_p
    rt_ref[...] = jnp.where(lane == 0.0, i1 - N_EXPERT_GROUPS,
                            jnp.where(lane == 1.0, i2 - N_EXPERT_GROUPS,
                                      jnp.where(lane == 2.0, w1, jnp.where(lane == 3.0, w2, 0.0))))


def out_proj(x, o_attn, d, w_pool, pool_scale, w_out_bf, g_ffn, w_router, b_router, *, tm):
    n = x.shape[0]
    row = lambda i: (i, 0)
    const = lambda i: (0, 0)
    return pl.pallas_call(
        _out_proj_kernel,
        grid=(n // tm,),
        in_specs=[
            pl.BlockSpec((tm, D_MODEL), row),
            pl.BlockSpec((tm, ATTN_WIDTH), row),
            pl.BlockSpec((tm, POOL_WIDTH), row),
            pl.BlockSpec(w_pool.shape, lambda i: (0, 0, 0)),
            pl.BlockSpec((1, POOL_WIDTH), const),
            pl.BlockSpec((D_MODEL, D_MODEL), const),
            pl.BlockSpec((1, D_MODEL), const),
            pl.BlockSpec((D_MODEL, 2 * LANES), const),
            pl.BlockSpec((1, LANES), const),
        ],
        out_specs=(
            pl.BlockSpec((tm, D_MODEL), row),
            pl.BlockSpec((tm, LANES), row),
        ),
        out_shape=(
            jax.ShapeDtypeStruct((n, D_MODEL), F32),
            jax.ShapeDtypeStruct((n, LANES), F32),
        ),
        scratch_shapes=[pltpu.VMEM((tm, D_MODEL), BF16)],
        compiler_params=_cparams(("arbitrary",)),
        name="out_proj",
    )(x, o_attn, d, w_pool, pool_scale, w_out_bf, g_ffn, w_router, b_router)


MOE_TILE = 256
ROW_TILE = 256


def _route_kernel(rt_ref, pos_ref, meta_ref, *, n_tiles):
    lane = lax.broadcasted_iota(jnp.int32, (ROW_TILE, LANES), 1).astype(F32)
    r_i = lax.broadcasted_iota(jnp.int32, (ROW_TILE, ROW_TILE), 0)
    c_i = lax.broadcasted_iota(jnp.int32, (ROW_TILE, ROW_TILE), 1)
    tri = jnp.where(c_i < r_i, 1.0, 0.0).astype(BF16)

    def one_hot(t):
        rt = rt_ref[pl.ds(pl.multiple_of(t * ROW_TILE, ROW_TILE), ROW_TILE), :]
        e1, e2 = rt[:, 0:1], rt[:, 1:2]
        return e1, e2, jnp.where(jnp.logical_or(lane == e1, lane == e2), 1.0, 0.0)

    def count(t, cnt):
        return cnt + jnp.sum(one_hot(t)[2], axis=0, keepdims=True)

    cnt = lax.fori_loop(0, n_tiles, count, jnp.zeros((1, LANES), F32))
    tiles_per = jnp.floor((cnt + (MOE_TILE - 1)) * (1.0 / MOE_TILE))
    e_r = lax.broadcasted_iota(jnp.int32, (LANES, LANES), 0)
    e_c = lax.broadcasted_iota(jnp.int32, (LANES, LANES), 1)
    upper = jnp.where(e_r < e_c, 1.0, 0.0).astype(BF16)
    off_tiles = _dot(jnp.broadcast_to(tiles_per, (8, LANES)).astype(BF16), upper)[0:1]
    base = off_tiles * MOE_TILE

    def place(t, run):
        e1, e2, oh = one_hot(t)
        dest = base + run + _dot(tri, oh.astype(BF16))
        p1 = jnp.sum(jnp.where(lane == e1, dest, 0.0), axis=1, keepdims=True)
        p2 = jnp.sum(jnp.where(lane == e2, dest, 0.0), axis=1, keepdims=True)
        pos = jnp.where(lane == 0.0, p1, jnp.where(lane == 1.0, p2, 0.0))
        pos_ref[pl.ds(pl.multiple_of(t * ROW_TILE, ROW_TILE), ROW_TILE), :] = pos.astype(jnp.int32)
        return run + jnp.sum(oh, axis=0, keepdims=True)

    lax.fori_loop(0, n_tiles, place, jnp.zeros((1, LANES), F32))
    row = lax.broadcasted_iota(jnp.int32, (8, LANES), 0)
    meta_ref[...] = jnp.where(row == 0, tiles_per, jnp.where(row == 1, off_tiles, 0.0))


def route(rt):
    n = rt.shape[0]
    assert n % ROW_TILE == 0
    return pl.pallas_call(
        functools.partial(_route_kernel, n_tiles=n // ROW_TILE),
        out_shape=(
            jax.ShapeDtypeStruct((n, LANES), jnp.int32),
            jax.ShapeDtypeStruct((8, LANES), F32),
        ),
        compiler_params=pltpu.CompilerParams(vmem_limit_bytes=VMEM_LIMIT),
        name="route",
    )(rt)


ROW_UNROLL = 8


def _dispatch_kernel(p1_ref, p2_ref, zt_ref, xa_ref, xb_ref, xs_ref, zero_ref, sem, zsem, *, tm, tiles_a, n_zero):
    i = pl.program_id(0)

    @pl.when(i == 0)
    def _():
        zero_ref[...] = jnp.zeros(zero_ref.shape, zero_ref.dtype)

        def zcopy(z):
            row0 = pl.multiple_of(zt_ref[z] * MOE_TILE, MOE_TILE)
            return pltpu.make_async_copy(zero_ref, xs_ref.at[pl.ds(row0, MOE_TILE)], zsem)

        for z in range(n_zero):
            @pl.when(zt_ref[z] >= 0)
            def _():
                zcopy(z).start()
        for z in range(n_zero):
            @pl.when(zt_ref[z] >= 0)
            def _():
                zcopy(z).wait()

    def scatter(x_ref):
        t0 = i * tm

        def issue(r, c):
            src = x_ref.at[pl.ds(r, 1)]
            pltpu.make_async_copy(src, xs_ref.at[pl.ds(p1_ref[t0 + r], 1)], sem).start()
            pltpu.make_async_copy(src, xs_ref.at[pl.ds(p2_ref[t0 + r], 1)], sem).start()
            return c

        lax.fori_loop(0, tm, issue, 0, unroll=ROW_UNROLL)
        for _ in range(2):
            pltpu.make_async_copy(x_ref, xs_ref.at[pl.ds(0, tm)], sem).wait()

    @pl.when(i < tiles_a)
    def _():
        scatter(xa_ref)

    @pl.when(i >= tiles_a)
    def _():
        scatter(xb_ref)


def dispatch(p1, p2, zero_tiles, xa, xb, rows, *, tm):
    tiles_a, tiles_b = xa.shape[0] // tm, xb.shape[0] // tm
    grid_spec = pltpu.PrefetchScalarGridSpec(
        num_scalar_prefetch=3,
        grid=(tiles_a + tiles_b,),
        in_specs=[
            pl.BlockSpec((tm, D_MODEL), lambda i, a, b, z: (jnp.minimum(i, tiles_a - 1), 0)),
            pl.BlockSpec((tm, D_MODEL), lambda i, a, b, z: (jnp.maximum(i - tiles_a, 0), 0)),
        ],
        out_specs=pl.BlockSpec(memory_space=pl.ANY),
        scratch_shapes=[pltpu.VMEM((MOE_TILE, D_MODEL), F32), pltpu.SemaphoreType.DMA(()), pltpu.SemaphoreType.DMA(())],
    )
    return pl.pallas_call(
        functools.partial(_dispatch_kernel, tm=tm, tiles_a=tiles_a, n_zero=zero_tiles.shape[0]),
        grid_spec=grid_spec,
        out_shape=jax.ShapeDtypeStruct((rows, D_MODEL), F32),
        compiler_params=_cparams(("arbitrary",)),
        name="dispatch",
    )(p1, p2, zero_tiles, xa, xb)


def _experts_kernel(te_ref, ts_ref, nu_ref, x_ref, gf_ref, wg_ref, wu_ref, wd_ref, o_ref):
    @pl.when(pl.program_id(0) < nu_ref[0])
    def _():
        h = x_ref[...]
        ms = jnp.mean(h * h, axis=-1, keepdims=True)
        x = (h * lax.rsqrt(ms + EPS) * gf_ref[...]).astype(BF16).astype(F32)
        a = _dot(x, wg_ref[0])
        b = _dot(x, wu_ref[0])
        act = (a * (1.0 / (1.0 + jnp.exp(-a)))) * b
        o_ref[...] = _dot(act.astype(BF16).astype(F32), wd_ref[0])

    @pl.when(pl.program_id(0) >= nu_ref[0])
    def _():
        o_ref[...] = jnp.zeros(o_ref.shape, o_ref.dtype)


def experts(tile_expert, tile_src, n_used, xs, g_ffn, w_gate, w_up, w_down, *, layer):
    rows, w = xs.shape
    wspec = lambda shape: pl.BlockSpec((None, 1) + shape, lambda j, te, ts, nu: (layer, te[j], 0, 0))
    grid_spec = pltpu.PrefetchScalarGridSpec(
        num_scalar_prefetch=3,
        grid=(rows // MOE_TILE,),
        in_specs=[
            pl.BlockSpec((MOE_TILE, w), lambda j, te, ts, nu: (ts[j], 0)),
            pl.BlockSpec((1, D_MODEL), lambda j, te, ts, nu: (0, 0)),
            wspec((D_MODEL, D_EXPERT)), wspec((D_MODEL, D_EXPERT)), wspec((D_EXPERT, D_MODEL)),
        ],
        out_specs=pl.BlockSpec((MOE_TILE, D_MODEL), lambda j, te, ts, nu: (j, 0)),
    )
    return pl.pallas_call(
        _experts_kernel,
        grid_spec=grid_spec,
        out_shape=jax.ShapeDtypeStruct((rows, D_MODEL), F32),
        compiler_params=_cparams(("arbitrary",), EXPERT_VMEM_LIMIT),
        name="experts",
    )(tile_expert, tile_src, n_used, xs, g_ffn, w_gate, w_up, w_down)


def _combine_kernel(p1_ref, p2_ref, h2_ref, rt_ref, os_ref, y_ref, buf_ref, sem, *, tm):
    t0 = pl.program_id(0) * tm

    def issue(r, c):
        pltpu.make_async_copy(os_ref.at[pl.ds(p1_ref[t0 + r], 1)], buf_ref.at[0, pl.ds(r, 1)], sem).start()
        pltpu.make_async_copy(os_ref.at[pl.ds(p2_ref[t0 + r], 1)], buf_ref.at[1, pl.ds(r, 1)], sem).start()
        return c

    lax.fori_loop(0, tm, issue, 0, unroll=ROW_UNROLL)
    for s in range(2):
        pltpu.make_async_copy(os_ref.at[pl.ds(0, tm)], buf_ref.at[s], sem).wait()
    rt = rt_ref[...]
    y_ref[...] = h2_ref[...] + rt[:, 2:3] * buf_ref[0] + rt[:, 3:4] * buf_ref[1]


def combine(p1, p2, h2, rt, os, *, tm):
    n = h2.shape[0]
    row = lambda i, a, b: (i, 0)
    grid_spec = pltpu.PrefetchScalarGridSpec(
        num_scalar_prefetch=2,
        grid=(n // tm,),
        in_specs=[pl.BlockSpec((tm, D_MODEL), row), pl.BlockSpec((tm, LANES), row),
                  pl.BlockSpec(memory_space=pl.ANY)],
        out_specs=pl.BlockSpec((tm, D_MODEL), row),
        scratch_shapes=[pltpu.VMEM((2, tm, D_MODEL), F32), pltpu.SemaphoreType.DMA(())],
    )
    return pl.pallas_call(
        functools.partial(_combine_kernel, tm=tm),
        grid_spec=grid_spec,
        out_shape=jax.ShapeDtypeStruct((n, D_MODEL), F32),
        compiler_params=_cparams(("arbitrary",)),
        name="combine",
    )(p1, p2, h2, rt, os)


def kernel(x_prompt, x_sample, cache_k, cache_v, state_pool, page_table, g_mix, w_in, g_q, g_k, w_pool, pool_scale, w_out, g_ffn, w_group_router, b_group_router, w_expert_router, b_expert_router, w_gate, w_up, w_down):
    B, S, _ = x_prompt.shape
    DB, T, _ = x_sample.shape
    depth = w_in.shape[0]
    assert B == 1 and depth == 1
    past_len = page_table.shape[1] * cache_k.shape[2]
    l = 0

    w_in_bf = w_in[l].astype(BF16)
    w_out_bf = w_out[l].astype(BF16)
    gm, gq, gk, gf = g_mix[l][None], g_q[l][None], g_k[l][None], g_ffn[l][None]
    ps = pool_scale[l][None]
    n_r = N_EXPERT_GROUPS + N_EXPERTS
    w_router = jnp.concatenate([w_group_router[l], w_expert_router[l].reshape(D_MODEL, N_EXPERTS)], axis=1)
    w_router = jnp.pad(w_router, ((0, 0), (0, LANES - n_r)))
    w_router_hi = w_router.astype(BF16)
    w_router = jnp.concatenate([w_router_hi, (w_router - w_router_hi.astype(F32)).astype(BF16)], axis=1)
    b_router = jnp.concatenate([b_group_router[l], b_expert_router[l].reshape(N_EXPERTS)])
    b_router = jnp.pad(b_router, (0, LANES - n_r))[None]

    def mixer_tail(x2d, o_attn, d):
        return out_proj(x2d, o_attn, d, w_pool[l], ps, w_out_bf, gf, w_router, b_router, tm=512)

    xp = x_prompt.reshape(S, D_MODEL)
    n_s = DB * T
    xs = x_sample.reshape(n_s, D_MODEL)
    q_p, k_p, kb_p, km_p, v_p, vb_p, u_p = in_proj(xp, gm, w_in_bf, gq, gk, tm=512)
    q_s, k_s, _, _, v_s, _, u_s = in_proj(xs, gm, w_in_bf, gq, gk, tm=n_s)
    r3 = lambda a: a.reshape(DB, T, ATTN_WIDTH)
    o_p, o_s = moba_attention(q_p, kb_p, vb_p, km_p.reshape(S // MOBA_BLOCK, ATTN_WIDTH),
                              r3(q_s), r3(k_s), r3(v_s), cache_k, cache_v, page_table + l * cache_k.shape[1])
    d_p = pool_prompt(u_p, tm=512)
    d_s, pool_s = pool_sample(u_s.reshape(DB, T, POOL_WIDTH), state_pool[l], start_pos=past_len)
    h2_p, rt_p = mixer_tail(xp, o_p, d_p)
    h2_s, rt_s = mixer_tail(xs, o_s.reshape(n_s, ATTN_WIDTH), d_s.reshape(n_s, POOL_WIDTH))

    pos, meta = route(jnp.concatenate([rt_p, rt_s], axis=0))
    p1, p2 = pos[:, 0], pos[:, 1]
    max_tiles = -(-2 * (S + n_s) // MOE_TILE) + N_EXPERTS
    tiles_per = meta[0, :N_EXPERTS].astype(jnp.int32)
    ends = tiles_per + meta[1, :N_EXPERTS].astype(jnp.int32)
    n_used = ends[N_EXPERTS - 1]
    tile_src = jnp.minimum(jnp.arange(max_tiles, dtype=jnp.int32), n_used - 1)
    tile_expert = jnp.minimum(jnp.sum(tile_src[:, None] >= ends[None, :], axis=1), N_EXPERTS - 1).astype(jnp.int32)
    tail = n_used + jnp.arange(N_EXPERTS, dtype=jnp.int32)
    zero_tiles = jnp.concatenate([jnp.where(tiles_per > 0, ends - 1, -1), jnp.where(tail < max_tiles, tail, -1)])
    x_sorted = dispatch(p1, p2, zero_tiles, h2_p, h2_s, max_tiles * MOE_TILE, tm=ROW_TILE)
    o_sorted = experts(tile_expert, tile_src, n_used[None], x_sorted, gf, w_gate, w_up, w_down, layer=l)
    y_prompt = combine(p1[:S], p2[:S], h2_p, rt_p, o_sorted, tm=ROW_TILE).reshape(B, S, D_MODEL)
    y_sample = combine(p1[S:], p2[S:], h2_s, rt_s, o_sorted, tm=ROW_TILE).reshape(DB, T, D_MODEL)

    hd = (N_HEADS, HEAD_DIM)
    return (
        y_prompt,
        y_sample,
        k_p.reshape(1, B, S, *hd),
        v_p.reshape(1, B, S, *hd),
        u_p[S - POOL_STATE:].reshape(1, B, POOL_STATE, POOL_WIDTH),
        k_s.reshape(1, DB, T, *hd),
        v_s.reshape(1, DB, T, *hd),
        pool_s.reshape(1, DB, POOL_STATE, POOL_WIDTH),
    )
```

```python
import functools

import jax
import jax.numpy as jnp
from jax import lax
from jax.experimental import pallas as pl
from jax.experimental.pallas import tpu as pltpu

D_MODEL = 2048
ATTN_WIDTH = 1024
POOL_WIDTH = 1024
HEAD_DIM = 128
N_HEADS = 8
POOL_WINDOWS = (2, 4, 8, 16)
POOL_GROUP_W = 256
POOL_STATE = 15
MOBA_BLOCK = 256
MOBA_TOPK = 3
N_EXPERT_GROUPS = 4
EXPERTS_PER_GROUP = 4
N_EXPERTS = 16
D_EXPERT = 768
EPS = 1e-6
NEG = -1e30
LANES = 128
VMEM_LIMIT = 56 * 1024 * 1024
ATTN_VMEM_LIMIT = 60 * 1024 * 1024
EXPERT_VMEM_LIMIT = 60 * 1024 * 1024

BF16 = jnp.bfloat16
F32 = jnp.float32


def _cparams(sem, vmem_limit=VMEM_LIMIT):
    return pltpu.CompilerParams(dimension_semantics=sem, vmem_limit_bytes=vmem_limit)


def _dot(a, b):
    return jnp.dot(a, b, preferred_element_type=F32)


def _dot_nt(a, b, precision=None):
    return lax.dot_general(a, b, (((1,), (1,)), ((), ())), precision=precision,
                           preferred_element_type=F32)


def _top_mask(g, ids, k, axis=-1):
    sel = jnp.zeros(g.shape, F32)
    for _ in range(k):
        m = jnp.max(g, axis=axis, keepdims=True)
        idx = jnp.min(jnp.where(g == m, ids, jnp.float32(g.shape[axis])), axis=axis, keepdims=True)
        pick = ids == idx
        sel = jnp.where(pick, 1.0, sel)
        g = jnp.where(pick, -jnp.inf, g)
    return sel


def _in_proj_kernel(x_ref, gmix_ref, w_ref, gq_ref, gk_ref,
                    q_ref, k_ref, kb_ref, km_ref, v_ref, vb_ref, u_ref, *rest, tm, with_pool):
    if with_pool:
        d_ref, hb_ref, ext_ref = rest
    else:
        (hb_ref,) = rest
    xf = x_ref[...]
    ms = jnp.mean(xf * xf, axis=-1, keepdims=True)
    hb_ref[...] = (xf * lax.rsqrt(ms + EPS) * gmix_ref[...]).astype(BF16)

    def head_norm(z, out_ref, g_ref):
        for h in range(N_HEADS):
            zh = z[:, h * HEAD_DIM:(h + 1) * HEAD_DIM]
            ms = jnp.mean(zh * zh, axis=-1, keepdims=True)
            out_ref[:, h * HEAD_DIM:(h + 1) * HEAD_DIM] = zh * lax.rsqrt(ms + EPS) * g_ref[...]

    head_norm(_dot(hb_ref[...], w_ref[:, 0:ATTN_WIDTH]), q_ref, gq_ref)
    head_norm(_dot(hb_ref[...], w_ref[:, ATTN_WIDTH:2 * ATTN_WIDTH]), k_ref, gk_ref)
    kb_ref[...] = k_ref[...].astype(BF16)
    for g in range(tm // MOBA_BLOCK):
        blk = k_ref[g * MOBA_BLOCK:(g + 1) * MOBA_BLOCK, :]
        km_ref[g] = jnp.mean(blk, axis=0, keepdims=True)
    zv = _dot(hb_ref[...], w_ref[:, 2 * ATTN_WIDTH:3 * ATTN_WIDTH])
    v_ref[...] = zv
    vb_ref[...] = zv.astype(BF16)
    zu = _dot(hb_ref[...], w_ref[:, 3 * ATTN_WIDTH:])
    u_ref[...] = zu
    if with_pool:
        _pool_differences(zu, d_ref, ext_ref, tm=tm)


POOL_PAD = 16


def _pool_differences(u, d_ref, ext_ref, *, tm):
    i = pl.program_id(0)

    @pl.when(i == 0)
    def _():
        ext_ref[0:POOL_PAD, :] = jnp.zeros((POOL_PAD, POOL_WIDTH), F32)

    ext_ref[POOL_PAD:POOL_PAD + tm, :] = u
    pos = (i * tm + lax.broadcasted_iota(jnp.int32, (tm, 1), 0)).astype(F32)
    for g, w in enumerate(POOL_WINDOWS):
        c0, c1 = g * POOL_GROUP_W, (g + 1) * POOL_GROUP_W
        wsum = u[:, c0:c1]
        for jj in range(1, w):
            wsum = wsum + ext_ref[POOL_PAD - jj:POOL_PAD - jj + tm, c0:c1]
        inv = 1.0 / jnp.minimum(jnp.float32(w), pos + 1.0)
        d_ref[:, c0:c1] = (wsum * inv - u[:, c0:c1]).astype(d_ref.dtype)
    ext_ref[0:POOL_PAD, :] = ext_ref[tm:tm + POOL_PAD, :]


def in_proj(x, g_mix, w_in_bf, g_q, g_k, *, tm, with_pool):
    n = x.shape[0]
    wide = lambda i: (i, 0)
    const = lambda i: (0, 0)
    out_shape = (
        jax.ShapeDtypeStruct((n, ATTN_WIDTH), F32),
        jax.ShapeDtypeStruct((n, ATTN_WIDTH), F32),
        jax.ShapeDtypeStruct((n, ATTN_WIDTH), BF16),
        jax.ShapeDtypeStruct((n // MOBA_BLOCK, 1, ATTN_WIDTH), F32),
        jax.ShapeDtypeStruct((n, ATTN_WIDTH), F32),
        jax.ShapeDtypeStruct((n, ATTN_WIDTH), BF16),
        jax.ShapeDtypeStruct((n, POOL_WIDTH), F32),
    )
    blk = pl.BlockSpec((tm, ATTN_WIDTH), wide)
    pool_out = (jax.ShapeDtypeStruct((n, POOL_WIDTH), BF16),) if with_pool else ()
    pool_scratch = [pltpu.VMEM((POOL_PAD + tm, POOL_WIDTH), F32)] if with_pool else []
    return pl.pallas_call(
        functools.partial(_in_proj_kernel, tm=tm, with_pool=with_pool),
        grid=(n // tm,),
        in_specs=[
            pl.BlockSpec((tm, D_MODEL), wide),
            pl.BlockSpec((1, D_MODEL), const),
            pl.BlockSpec(w_in_bf.shape, const, pipeline_mode=pl.Buffered(1)),
            pl.BlockSpec((1, HEAD_DIM), const),
            pl.BlockSpec((1, HEAD_DIM), const),
        ],
        out_specs=(blk, blk, blk,
                   pl.BlockSpec((tm // MOBA_BLOCK, 1, ATTN_WIDTH), lambda i: (i, 0, 0)),
                   blk, blk, blk) + (blk,) * len(pool_out),
        out_shape=out_shape + pool_out,
        scratch_shapes=[pltpu.VMEM((tm, D_MODEL), BF16)] + pool_scratch,
        compiler_params=_cparams(("arbitrary",)),
        name="in_proj",
    )(x, g_mix, w_in_bf, g_q, g_k)


LOG2E = 1.4426950408889634
ATT_HEADS = 2
ATT_GROUP = 4


def _moba_prompt_prologue(own, q_ref, k_ref, vt_ref, km_ref, qs_ref, bias_ref, m_ref, l_ref, acc_ref, *, nb, hp):
    base = pl.multiple_of(own * MOBA_BLOCK, MOBA_BLOCK)
    blk_id = lax.broadcasted_iota(jnp.int32, (nb, MOBA_BLOCK), 0).astype(F32)
    valid = blk_id < own.astype(F32)
    key_i = lax.broadcasted_iota(jnp.int32, (MOBA_BLOCK, MOBA_BLOCK), 0)
    qry_i = lax.broadcasted_iota(jnp.int32, (MOBA_BLOCK, MOBA_BLOCK), 1)

    for hh in range(hp):
        cols = slice(hh * HEAD_DIM, (hh + 1) * HEAD_DIM)
        q = q_ref[:, cols]
        gate = _dot_nt(km_ref[:, cols], q, precision=lax.Precision.HIGHEST)
        sel = _top_mask(jnp.where(valid, gate, NEG), blk_id, MOBA_TOPK, axis=0)
        bias = jnp.where(jnp.logical_and(sel > 0.5, valid), 0.0, NEG)
        for n in range(nb):
            bias_ref[hh, n] = bias[n:n + 1, :]
        qs = (q * (HEAD_DIM ** -0.5 * LOG2E)).astype(BF16)
        qs_ref[hh] = qs
        s = _dot_nt(k_ref[pl.ds(base, MOBA_BLOCK), cols], qs)
        s = jnp.where(key_i <= qry_i, s, NEG)
        m0 = jnp.max(s, axis=0, keepdims=True)
        p = jnp.exp2(s - m0)
        m_ref[hh] = m0
        l_ref[hh] = jnp.sum(p, axis=0, keepdims=True)
        acc_ref[hh] = _dot(vt_ref[own, cols, :], p.astype(BF16))


def _moba_prompt_main(own, k_ref, vt_ref, o_ref, qs_ref, bias_ref, m_ref, l_ref, acc_ref, s_ref, mb_ref,
                      *, nb, hp, grp):
    n_iter = (own + grp - 1) // grp

    def scores(i, slot):
        for hh in range(hp):
            cols = slice(hh * HEAD_DIM, (hh + 1) * HEAD_DIM)
            qs = qs_ref[hh]
            mb = None
            for gg in range(grp):
                n = jnp.minimum(i * grp + gg, nb - 1)
                off = pl.multiple_of(n * MOBA_BLOCK, MOBA_BLOCK)
                s = _dot_nt(k_ref[pl.ds(off, MOBA_BLOCK), cols], qs) + bias_ref[hh, n]
                s_ref[slot, hh, gg] = s
                smax = jnp.max(s, axis=0, keepdims=True)
                mb = smax if mb is None else jnp.maximum(mb, smax)
            mb_ref[slot, hh] = mb

    def softmax_pv(i, slot):
        for hh in range(hp):
            cols = slice(hh * HEAD_DIM, (hh + 1) * HEAD_DIM)
            m_prev = m_ref[hh]
            m_new = jnp.maximum(m_prev, mb_ref[slot, hh])
            alpha = jnp.exp2(m_prev - m_new)
            l_new = alpha * l_ref[hh]
            acc = alpha * acc_ref[hh]
            for gg in range(grp):
                p = jnp.exp2(s_ref[slot, hh, gg] - m_new)
                l_new = l_new + jnp.sum(p, axis=0, keepdims=True)
                acc = acc + _dot(vt_ref[i * grp + gg, cols, :], p.astype(BF16))
            m_ref[hh] = m_new
            l_ref[hh] = l_new
            acc_ref[hh] = acc

    scores(0, 0)

    def body(k, carry):
        scores(2 * k + 1, 1)
        softmax_pv(2 * k, 0)
        scores(2 * k + 2, 0)
        softmax_pv(2 * k + 1, 1)
        return carry

    lax.fori_loop(0, n_iter // 2, body, 0)

    @pl.when(n_iter % 2 == 1)
    def _():
        softmax_pv(n_iter - 1, 0)

    for hh in range(hp):
        o = acc_ref[hh] * (1.0 / l_ref[hh])
        o_ref[:, hh * HEAD_DIM:(hh + 1) * HEAD_DIM] = o.T.astype(o_ref.dtype)


def _prompt_scratch(nb, hp, grp):
    return [
        pltpu.VMEM((hp, MOBA_BLOCK, HEAD_DIM), BF16),
        pltpu.VMEM((hp, nb, 1, MOBA_BLOCK), F32),
        pltpu.VMEM((hp, 1, MOBA_BLOCK), F32),
        pltpu.VMEM((hp, 1, MOBA_BLOCK), F32),
        pltpu.VMEM((hp, HEAD_DIM, MOBA_BLOCK), F32),
        pltpu.VMEM((2, hp, grp, MOBA_BLOCK, MOBA_BLOCK), F32),
        pltpu.VMEM((2, hp, 1, MOBA_BLOCK), F32),
    ]


def _moba_sample_body(q_ref, kn_ref, vn_ref, k_pages, v_pages, o_ref, s_ref, *, t_new, between):
    n_pages = len(k_pages)
    pw = k_pages[0].shape[0]
    page = pw // N_HEADS
    rows = t_new * N_HEADS
    per_blk = MOBA_BLOCK // page
    n_blocks = n_pages // per_blk

    qa = (q_ref[0] * (HEAD_DIM ** -0.5)).astype(BF16)
    r_i = lax.broadcasted_iota(jnp.int32, (rows, pw), 0)
    c_i = lax.broadcasted_iota(jnp.int32, (rows, pw), 1)
    same_head = (c_i % N_HEADS) == (r_i % N_HEADS)

    gcol = lax.broadcasted_iota(jnp.int32, (rows, LANES), 1).astype(F32)
    gate = jnp.full((rows, LANES), NEG, F32)
    bmax = jnp.full((rows, LANES), NEG, F32)
    for n in range(n_blocks):
        ssum = jnp.zeros((rows, pw), F32)
        smax = jnp.full((rows, pw), NEG, F32)
        for pp in range(per_blk):
            p_i = n * per_blk + pp
            sp = _dot_nt(qa, k_pages[p_i][...].astype(BF16))
            s_ref[:, p_i * pw:(p_i + 1) * pw] = sp
            ssum = ssum + jnp.where(same_head, sp, 0.0)
            smax = jnp.maximum(smax, jnp.where(same_head, sp, NEG))
        tot = jnp.sum(ssum, axis=-1, keepdims=True) * (1.0 / MOBA_BLOCK)
        gate = jnp.where(gcol == float(n), tot, gate)
        bmax = jnp.where(gcol == float(n), jnp.max(smax, axis=-1, keepdims=True), bmax)
    sel = _top_mask(gate, gcol, min(MOBA_TOPK, n_blocks))
    between()

    s_own = _dot_nt(qa, kn_ref[0].astype(BF16))
    orow = lax.broadcasted_iota(jnp.int32, s_own.shape, 0)
    ocol = lax.broadcasted_iota(jnp.int32, s_own.shape, 1)
    own_ok = jnp.logical_and(ocol % N_HEADS == orow % N_HEADS, ocol // N_HEADS <= orow // N_HEADS)
    s_own = jnp.where(own_ok, s_own, NEG)

    m = jnp.maximum(jnp.max(s_own, axis=-1, keepdims=True),
                    jnp.max(jnp.where(sel > 0.5, bmax, NEG), axis=-1, keepdims=True))
    p_own = jnp.exp(s_own - m)
    l = jnp.sum(p_own, axis=-1, keepdims=True)
    acc = _dot(p_own.astype(BF16), vn_ref[0].astype(BF16))
    lsum = jnp.zeros((rows, pw), F32)
    for p_i in range(n_pages):
        n = p_i // per_blk
        sp = s_ref[:, p_i * pw:(p_i + 1) * pw]
        keep = jnp.logical_and(same_head, sel[:, n:n + 1] > 0.5)
        pp = jnp.where(keep, jnp.exp(sp - m), 0.0)
        lsum = lsum + pp
        acc = acc + _dot(pp.astype(BF16), v_pages[p_i][...].astype(BF16))
    l = l + jnp.sum(lsum, axis=-1, keepdims=True)
    o_ref[0] = acc * (1.0 / l)


def _moba_kernel(pt_ref, q_ref, k_ref, vt_ref, km_ref, qs_ref, kn_ref, vn_ref, ck_ref, cv_ref, o_ref, os_ref,
                 qsc_ref, bias_ref, m_ref, l_ref, acc_ref, s_ref, mb_ref, sc_ref, kbuf_ref, vbuf_ref, page_sem,
                 *, nb, hp, grp, n_pages, t_new):
    i = pl.program_id(0)
    own = i % nb
    slot = i % 2

    def fetch(row, sl):
        for p in range(n_pages):
            pg = pt_ref[row, p]
            pltpu.make_async_copy(ck_ref.at[pg], kbuf_ref.at[sl, p], page_sem.at[sl]).start()
            pltpu.make_async_copy(cv_ref.at[pg], vbuf_ref.at[sl, p], page_sem.at[sl]).start()

    @pl.when(i == 0)
    def _():
        fetch(0, 0)

    @pl.when(i + 1 < pl.num_programs(0))
    def _():
        fetch(i + 1, 1 - slot)

    for src, buf in ((ck_ref, kbuf_ref), (cv_ref, vbuf_ref)):
        pltpu.make_async_copy(src.at[pl.ds(0, n_pages)], buf.at[slot], page_sem.at[slot]).wait()
    k_pages = [kbuf_ref.at[slot, p] for p in range(n_pages)]
    v_pages = [vbuf_ref.at[slot, p] for p in range(n_pages)]

    def prompt_prologue():
        _moba_prompt_prologue(own, q_ref, k_ref, vt_ref, km_ref, qsc_ref, bias_ref, m_ref, l_ref, acc_ref,
                              nb=nb, hp=hp)

    _moba_sample_body(qs_ref, kn_ref, vn_ref, k_pages, v_pages, os_ref, sc_ref, t_new=t_new,
                      between=prompt_prologue)
    _moba_prompt_main(own, k_ref, vt_ref, o_ref, qsc_ref, bias_ref, m_ref, l_ref, acc_ref, s_ref, mb_ref,
                      nb=nb, hp=hp, grp=grp)


def moba_attention(q, k_bf, v_bf, kmean, q_s, k_new, v_new, cache_k, cache_v, page_table):
    s = q.shape[0]
    nb = s // MOBA_BLOCK
    hp = ATT_HEADS
    grp = ATT_GROUP if nb % ATT_GROUP == 0 else 1
    wide = hp * HEAD_DIM
    vt = v_bf.reshape(nb, MOBA_BLOCK, ATTN_WIDTH).transpose(0, 2, 1)
    db, t_new, _ = q_s.shape
    th = t_new * N_HEADS
    q_s, k_new, v_new = (a.reshape(db, th, HEAD_DIM) for a in (q_s, k_new, v_new))
    assert db == (N_HEADS // hp) * nb, "one sample row per prompt tile step"
    n_pages = page_table.shape[1]
    page = cache_k.shape[-3]
    ck = cache_k.reshape(-1, page * N_HEADS, HEAD_DIM)
    cv = cache_v.reshape(-1, page * N_HEADS, HEAD_DIM)
    tok = pl.BlockSpec((1, th, HEAD_DIM), lambda i, pt: (i, 0, 0))
    tile = pl.BlockSpec((MOBA_BLOCK, wide), lambda i, pt: (i % nb, i // nb))

    page_rows = page * N_HEADS
    grid_spec = pltpu.PrefetchScalarGridSpec(
        num_scalar_prefetch=1,
        grid=(db,),
        in_specs=[
            tile,
            pl.BlockSpec((s, wide), lambda i, pt: (0, i // nb)),
            pl.BlockSpec((nb, wide, MOBA_BLOCK), lambda i, pt: (0, i // nb, 0)),
            pl.BlockSpec((nb, wide), lambda i, pt: (0, i // nb)),
            tok, tok, tok,
            pl.BlockSpec(memory_space=pl.ANY), pl.BlockSpec(memory_space=pl.ANY),
        ],
        out_specs=(tile, tok),
        scratch_shapes=_prompt_scratch(nb, hp, grp) + [
            pltpu.VMEM((th, n_pages * page_rows), F32),
            pltpu.VMEM((2, n_pages, page_rows, HEAD_DIM), F32),
            pltpu.VMEM((2, n_pages, page_rows, HEAD_DIM), F32),
            pltpu.SemaphoreType.DMA((2,)),
        ],
    )
    o_p, o_s = pl.pallas_call(
        functools.partial(_moba_kernel, nb=nb, hp=hp, grp=grp, n_pages=n_pages, t_new=t_new),
        grid_spec=grid_spec,
        out_shape=(jax.ShapeDtypeStruct((s, ATTN_WIDTH), BF16),
                   jax.ShapeDtypeStruct((db, th, HEAD_DIM), F32)),
        compiler_params=_cparams(("arbitrary",), ATTN_VMEM_LIMIT),
        name="moba_attention",
    )(page_table, q, k_bf, vt, kmean, q_s, k_new, v_new, ck, cv)
    return o_p, o_s.reshape(db, t_new, ATTN_WIDTH)


def _pool_sample_kernel(u_ref, st_ref, d_ref, new_ref, *, t_new, start_pos):
    ext = [st_ref[:, r, :] for r in range(POOL_STATE)] + [u_ref[:, t, :] for t in range(t_new)]
    for t in range(t_new):
        e = POOL_STATE + t
        parts = []
        for g, w in enumerate(POOL_WINDOWS):
            c0, c1 = g * POOL_GROUP_W, (g + 1) * POOL_GROUP_W
            wsum = ext[e][:, c0:c1]
            for jj in range(1, w):
                wsum = wsum + ext[e - jj][:, c0:c1]
            count = min(float(w), float(start_pos + t) + 1.0)
            parts.append(wsum * (1.0 / count) - ext[e][:, c0:c1])
        d_ref[:, t, :] = jnp.concatenate(parts, axis=-1).astype(d_ref.dtype)
    for r in range(POOL_STATE):
        new_ref[:, r, :] = ext[t_new + r]


def pool_sample(u, state, *, start_pos, bb=32):
    db, t_new, _ = u.shape
    return pl.pallas_call(
        functools.partial(_pool_sample_kernel, t_new=t_new, start_pos=start_pos),
        grid=(db // bb,),
        in_specs=[
            pl.BlockSpec((bb, t_new, POOL_WIDTH), lambda i: (i, 0, 0)),
            pl.BlockSpec((bb, POOL_STATE, POOL_WIDTH), lambda i: (i, 0, 0)),
        ],
        out_specs=(
            pl.BlockSpec((bb, t_new, POOL_WIDTH), lambda i: (i, 0, 0)),
            pl.BlockSpec((bb, POOL_STATE, POOL_WIDTH), lambda i: (i, 0, 0)),
        ),
        out_shape=(
            jax.ShapeDtypeStruct((db, t_new, POOL_WIDTH), F32),
            jax.ShapeDtypeStruct((db, POOL_STATE, POOL_WIDTH), F32),
        ),
        compiler_params=_cparams(("arbitrary",)),
        name="pool_sample",
    )(u, state)


def _out_proj_kernel(x_ref, oa_ref, d_ref, wp_ref, ps_ref, wo_ref, gf_ref, wr_ref, br_ref,
                     h2_ref, rt_ref, mix_ref):
    mix_ref[:, 0:ATTN_WIDTH] = oa_ref[...].astype(BF16)
    dd = d_ref[...].astype(BF16)
    for g in range(len(POOL_WINDOWS)):
        c0, c1 = g * POOL_GROUP_W, (g + 1) * POOL_GROUP_W
        yg = _dot(dd[:, c0:c1], wp_ref[g].astype(BF16)) * ps_ref[:, c0:c1]
        mix_ref[:, ATTN_WIDTH + c0:ATTN_WIDTH + c1] = yg.astype(BF16)
    h2 = x_ref[...] + _dot(mix_ref[...], wo_ref[...])
    h2_ref[...] = h2
    ms = jnp.mean(h2 * h2, axis=-1, keepdims=True)
    hn = h2 * lax.rsqrt(ms + EPS) * gf_ref[...]

    hn_hi = hn.astype(BF16)
    hn_lo = (hn - hn_hi.astype(F32)).astype(BF16)
    t = _dot(hn_hi, wr_ref[...])
    logits = t[:, :LANES] + t[:, LANES:] + _dot(hn_lo, wr_ref[:, :LANES]) + br_ref[...]
    lane = lax.broadcasted_iota(jnp.int32, logits.shape, 1).astype(F32)
    far = jnp.float32(LANES)
    is_g = lane < N_EXPERT_GROUPS
    gl = jnp.where(is_g, logits, -jnp.inf)
    g_max = jnp.max(gl, axis=-1, keepdims=True)
    g_top = jnp.min(jnp.where(gl == g_max, lane, far), axis=-1, keepdims=True)
    g_p = 1.0 / jnp.sum(jnp.where(is_g, jnp.exp(gl - g_max), 0.0), axis=-1, keepdims=True)
    lo = N_EXPERT_GROUPS + g_top * EXPERTS_PER_GROUP
    in_grp = jnp.logical_and(lane >= lo, lane < lo + EXPERTS_PER_GROUP)
    el = jnp.where(in_grp, logits, -jnp.inf)
    e1 = jnp.max(el, axis=-1, keepdims=True)
    i1 = jnp.min(jnp.where(el == e1, lane, far), axis=-1, keepdims=True)
    el2 = jnp.where(lane == i1, -jnp.inf, el)
    e2 = jnp.max(el2, axis=-1, keepdims=True)
    i2 = jnp.min(jnp.where(el2 == e2, lane, far), axis=-1, keepdims=True)
    ex2 = jnp.exp(e2 - e1)
    den = 1.0 + ex2
    w1 = (1.0 / den) * g_p
    w2 = (ex2 / den) * g_p
    rt_ref[...] = jnp.where(lane == 0.0, i1 - N_EXPERT_GROUPS,
                            jnp.where(lane == 1.0, i2 - N_EXPERT_GROUPS,
                                      jnp.where(lane == 2.0, w1, jnp.where(lane == 3.0, w2, 0.0))))


def out_proj(x, o_attn, d, w_pool, pool_scale, w_out_bf, g_ffn, w_router, b_router, *, tm):
    n = x.shape[0]
    row = lambda i: (i, 0)
    const = lambda i: (0, 0)
    return pl.pallas_call(
        _out_proj_kernel,
        grid=(n // tm,),
        in_specs=[
            pl.BlockSpec((tm, D_MODEL), row),
            pl.BlockSpec((tm, ATTN_WIDTH), row),
            pl.BlockSpec((tm, POOL_WIDTH), row),
            pl.BlockSpec(w_pool.shape, lambda i: (0, 0, 0)),
            pl.BlockSpec((1, POOL_WIDTH), const),
            pl.BlockSpec((D_MODEL, D_MODEL), const),
            pl.BlockSpec((1, D_MODEL), const),
            pl.BlockSpec((D_MODEL, 2 * LANES), const),
            pl.BlockSpec((1, LANES), const),
        ],
        out_specs=(
            pl.BlockSpec((tm, D_MODEL), row),
            pl.BlockSpec((tm, LANES), row),
        ),
        out_shape=(
            jax.ShapeDtypeStruct((n, D_MODEL), F32),
            jax.ShapeDtypeStruct((n, LANES), F32),
        ),
        scratch_shapes=[pltpu.VMEM((tm, D_MODEL), BF16)],
        compiler_params=_cparams(("arbitrary",)),
        name="out_proj",
    )(x, o_attn, d, w_pool, pool_scale, w_out_bf, g_ffn, w_router, b_router)


MOE_TILE = 256
ROW_TILE = 256


def _route_kernel(rt_ref, pos_ref, meta_ref, *, n_tiles):
    lane = lax.broadcasted_iota(jnp.int32, (ROW_TILE, LANES), 1).astype(F32)
    r_i = lax.broadcasted_iota(jnp.int32, (ROW_TILE, ROW_TILE), 0)
    c_i = lax.broadcasted_iota(jnp.int32, (ROW_TILE, ROW_TILE), 1)
    tri = jnp.where(c_i < r_i, 1.0, 0.0).astype(BF16)

    def one_hot(t):
        rt = rt_ref[pl.ds(pl.multiple_of(t * ROW_TILE, ROW_TILE), ROW_TILE), :]
        e1, e2 = rt[:, 0:1], rt[:, 1:2]
        return e1, e2, jnp.where(jnp.logical_or(lane == e1, lane == e2), 1.0, 0.0)

    def count(t, cnt):
        return cnt + jnp.sum(one_hot(t)[2], axis=0, keepdims=True)

    cnt = lax.fori_loop(0, n_tiles, count, jnp.zeros((1, LANES), F32))
    tiles_per = jnp.floor((cnt + (MOE_TILE - 1)) * (1.0 / MOE_TILE))
    e_r = lax.broadcasted_iota(jnp.int32, (LANES, LANES), 0)
    e_c = lax.broadcasted_iota(jnp.int32, (LANES, LANES), 1)
    upper = jnp.where(e_r < e_c, 1.0, 0.0).astype(BF16)
    off_tiles = _dot(jnp.broadcast_to(tiles_per, (8, LANES)).astype(BF16), upper)[0:1]
    base = off_tiles * MOE_TILE

    def place(t, run):
        e1, e2, oh = one_hot(t)
        dest = base + run + _dot(tri, oh.astype(BF16))
        p1 = jnp.sum(jnp.where(lane == e1, dest, 0.0), axis=1, keepdims=True)
        p2 = jnp.sum(jnp.where(lane == e2, dest, 0.0), axis=1, keepdims=True)
        pos = jnp.where(lane == 0.0, p1, jnp.where(lane == 1.0, p2, 0.0))
        pos_ref[pl.ds(pl.multiple_of(t * ROW_TILE, ROW_TILE), ROW_TILE), :] = pos.astype(jnp.int32)
        return run + jnp.sum(oh, axis=0, keepdims=True)

    lax.fori_loop(0, n_tiles, place, jnp.zeros((1, LANES), F32))
    row = lax.broadcasted_iota(jnp.int32, (8, LANES), 0)
    meta_ref[...] = jnp.where(row == 0, tiles_per, jnp.where(row == 1, off_tiles, 0.0))


def route(rt):
    n = rt.shape[0]
    assert n % ROW_TILE == 0
    return pl.pallas_call(
        functools.partial(_route_kernel, n_tiles=n // ROW_TILE),
        out_shape=(
            jax.ShapeDtypeStruct((n, LANES), jnp.int32),
            jax.ShapeDtypeStruct((8, LANES), F32),
        ),
        compiler_params=pltpu.CompilerParams(vmem_limit_bytes=VMEM_LIMIT),
        name="route",
    )(rt)


ROW_UNROLL = 8


def _dispatch_kernel(p1_ref, p2_ref, zt_ref, xa_ref, xb_ref, xs_ref, zero_ref, sem, zsem, *, tm, tiles_a, n_zero):
    i = pl.program_id(0)

    @pl.when(i == 0)
    def _():
        zero_ref[...] = jnp.zeros(zero_ref.shape, zero_ref.dtype)

        def zcopy(z):
            row0 = pl.multiple_of(zt_ref[z] * MOE_TILE, MOE_TILE)
            return pltpu.make_async_copy(zero_ref, xs_ref.at[pl.ds(row0, MOE_TILE)], zsem)

        for z in range(n_zero):
            @pl.when(zt_ref[z] >= 0)
            def _():
                zcopy(z).start()
        for z in range(n_zero):
            @pl.when(zt_ref[z] >= 0)
            def _():
                zcopy(z).wait()

    def scatter(x_ref):
        t0 = i * tm

        def issue(r, c):
            src = x_ref.at[pl.ds(r, 1)]
            pltpu.make_async_copy(src, xs_ref.at[pl.ds(p1_ref[t0 + r], 1)], sem).start()
            pltpu.make_async_copy(src, xs_ref.at[pl.ds(p2_ref[t0 + r], 1)], sem).start()
            return c

        lax.fori_loop(0, tm, issue, 0, unroll=ROW_UNROLL)
        for _ in range(2):
            pltpu.make_async_copy(x_ref, xs_ref.at[pl.ds(0, tm)], sem).wait()

    @pl.when(i < tiles_a)
    def _():
        scatter(xa_ref)

    @pl.when(i >= tiles_a)
    def _():
        scatter(xb_ref)


def dispatch(p1, p2, zero_tiles, xa, xb, rows, *, tm):
    tiles_a, tiles_b = xa.shape[0] // tm, xb.shape[0] // tm
    grid_spec = pltpu.PrefetchScalarGridSpec(
        num_scalar_prefetch=3,
        grid=(tiles_a + tiles_b,),
        in_specs=[
            pl.BlockSpec((tm, D_MODEL), lambda i, a, b, z: (jnp.minimum(i, tiles_a - 1), 0)),
            pl.BlockSpec((tm, D_MODEL), lambda i, a, b, z: (jnp.maximum(i - tiles_a, 0), 0)),
        ],
        out_specs=pl.BlockSpec(memory_space=pl.ANY),
        scratch_shapes=[pltpu.VMEM((MOE_TILE, D_MODEL), F32), pltpu.SemaphoreType.DMA(()), pltpu.SemaphoreType.DMA(())],
    )
    return pl.pallas_call(
        functools.partial(_dispatch_kernel, tm=tm, tiles_a=tiles_a, n_zero=zero_tiles.shape[0]),
        grid_spec=grid_spec,
        out_shape=jax.ShapeDtypeStruct((rows, D_MODEL), F32),
        compiler_params=_cparams(("arbitrary",)),
        name="dispatch",
    )(p1, p2, zero_tiles, xa, xb)


def _experts_kernel(te_ref, ts_ref, nu_ref, x_ref, gf_ref, wg_ref, wu_ref, wd_ref, o_ref):
    @pl.when(pl.program_id(0) < nu_ref[0])
    def _():
        h = x_ref[...]
        ms = jnp.mean(h * h, axis=-1, keepdims=True)
        x = (h * lax.rsqrt(ms + EPS) * gf_ref[...]).astype(BF16).astype(F32)
        a = _dot(x, wg_ref[0])
        b = _dot(x, wu_ref[0])
        act = (a * (1.0 / (1.0 + jnp.exp(-a)))) * b
        o_ref[...] = _dot(act.astype(BF16).astype(F32), wd_ref[0])

    @pl.when(pl.program_id(0) >= nu_ref[0])
    def _():
        o_ref[...] = jnp.zeros(o_ref.shape, o_ref.dtype)


def experts(tile_expert, tile_src, n_used, xs, g_ffn, w_gate, w_up, w_down, *, layer):
    rows, w = xs.shape
    wspec = lambda shape: pl.BlockSpec((None, 1) + shape, lambda j, te, ts, nu: (layer, te[j], 0, 0))
    grid_spec = pltpu.PrefetchScalarGridSpec(
        num_scalar_prefetch=3,
        grid=(rows // MOE_TILE,),
        in_specs=[
            pl.BlockSpec((MOE_TILE, w), lambda j, te, ts, nu: (ts[j], 0)),
            pl.BlockSpec((1, D_MODEL), lambda j, te, ts, nu: (0, 0)),
            wspec((D_MODEL, D_EXPERT)), wspec((D_MODEL, D_EXPERT)), wspec((D_EXPERT, D_MODEL)),
        ],
        out_specs=pl.BlockSpec((MOE_TILE, D_MODEL), lambda j, te, ts, nu: (j, 0)),
    )
    return pl.pallas_call(
        _experts_kernel,
        grid_spec=grid_spec,
        out_shape=jax.ShapeDtypeStruct((rows, D_MODEL), F32),
        compiler_params=_cparams(("arbitrary",), EXPERT_VMEM_LIMIT),
        name="experts",
    )(tile_expert, tile_src, n_used, xs, g_ffn, w_gate, w_up, w_down)


def _combine_kernel(p1_ref, p2_ref, h2_ref, rt_ref, os_ref, y_ref, buf_ref, sem, *, tm):
    i = pl.program_id(0)
    slot = i % 2

    def gather(step, sl):
        t0 = step * tm

        def issue(r, c):
            pltpu.make_async_copy(os_ref.at[pl.ds(p1_ref[t0 + r], 1)], buf_ref.at[sl, 0, pl.ds(r, 1)],
                                  sem.at[sl]).start()
            pltpu.make_async_copy(os_ref.at[pl.ds(p2_ref[t0 + r], 1)], buf_ref.at[sl, 1, pl.ds(r, 1)],
                                  sem.at[sl]).start()
            return c

        lax.fori_loop(0, tm, issue, 0, unroll=ROW_UNROLL)

    @pl.when(i == 0)
    def _():
        gather(0, 0)

    @pl.when(i + 1 < pl.num_programs(0))
    def _():
        gather(i + 1, 1 - slot)

    for s in range(2):
        pltpu.make_async_copy(os_ref.at[pl.ds(0, tm)], buf_ref.at[slot, s], sem.at[slot]).wait()
    rt = rt_ref[...]
    y_ref[...] = h2_ref[...] + rt[:, 2:3] * buf_ref[slot, 0] + rt[:, 3:4] * buf_ref[slot, 1]


def combine(p1, p2, h2, rt, os, *, tm):
    n = h2.shape[0]
    row = lambda i, a, b: (i, 0)
    grid_spec = pltpu.PrefetchScalarGridSpec(
        num_scalar_prefetch=2,
        grid=(n // tm,),
        in_specs=[pl.BlockSpec((tm, D_MODEL), row), pl.BlockSpec((tm, LANES), row),
                  pl.BlockSpec(memory_space=pl.ANY)],
        out_specs=pl.BlockSpec((tm, D_MODEL), row),
        scratch_shapes=[pltpu.VMEM((2, 2, tm, D_MODEL), F32), pltpu.SemaphoreType.DMA((2,))],
    )
    return pl.pallas_call(
        functools.partial(_combine_kernel, tm=tm),
        grid_spec=grid_spec,
        out_shape=jax.ShapeDtypeStruct((n, D_MODEL), F32),
        compiler_params=_cparams(("arbitrary",)),
        name="combine",
    )(p1, p2, h2, rt, os)


def kernel(x_prompt, x_sample, cache_k, cache_v, state_pool, page_table, g_mix, w_in, g_q, g_k, w_pool, pool_scale, w_out, g_ffn, w_group_router, b_group_router, w_expert_router, b_expert_router, w_gate, w_up, w_down):
    B, S, _ = x_prompt.shape
    DB, T, _ = x_sample.shape
    depth = w_in.shape[0]
    assert B == 1 and depth == 1
    past_len = page_table.shape[1] * cache_k.shape[2]
    l = 0

    w_in_bf = w_in[l].astype(BF16)
    w_out_bf = w_out[l].astype(BF16)
    gm, gq, gk, gf = g_mix[l][None], g_q[l][None], g_k[l][None], g_ffn[l][None]
    ps = pool_scale[l][None]
    n_r = N_EXPERT_GROUPS + N_EXPERTS
    w_router = jnp.concatenate([w_group_router[l], w_expert_router[l].reshape(D_MODEL, N_EXPERTS)], axis=1)
    w_router = jnp.pad(w_router, ((0, 0), (0, LANES - n_r)))
    w_router_hi = w_router.astype(BF16)
    w_router = jnp.concatenate([w_router_hi, (w_router - w_router_hi.astype(F32)).astype(BF16)], axis=1)
    b_router = jnp.concatenate([b_group_router[l], b_expert_router[l].reshape(N_EXPERTS)])
    b_router = jnp.pad(b_router, (0, LANES - n_r))[None]

    def mixer_tail(x2d, o_attn, d):
        return out_proj(x2d, o_attn, d, w_pool[l], ps, w_out_bf, gf, w_router, b_router, tm=512)

    xp = x_prompt.reshape(S, D_MODEL)
    n_s = DB * T
    xs = x_sample.reshape(n_s, D_MODEL)
    q_p, k_p, kb_p, km_p, v_p, vb_p, u_p, d_p = in_proj(xp, gm, w_in_bf, gq, gk, tm=512, with_pool=True)
    q_s, k_s, _, _, v_s, _, u_s = in_proj(xs, gm, w_in_bf, gq, gk, tm=n_s, with_pool=False)
    r3 = lambda a: a.reshape(DB, T, ATTN_WIDTH)
    o_p, o_s = moba_attention(q_p, kb_p, vb_p, km_p.reshape(S // MOBA_BLOCK, ATTN_WIDTH),
                              r3(q_s), r3(k_s), r3(v_s), cache_k, cache_v, page_table + l * cache_k.shape[1])
    d_s, pool_s = pool_sample(u_s.reshape(DB, T, POOL_WIDTH), state_pool[l], start_pos=past_len)
    h2_p, rt_p = mixer_tail(xp, o_p, d_p)
    h2_s, rt_s = mixer_tail(xs, o_s.reshape(n_s, ATTN_WIDTH), d_s.reshape(n_s, POOL_WIDTH))

    pos, meta = route(jnp.concatenate([rt_p, rt_s], axis=0))
    p1, p2 = pos[:, 0], pos[:, 1]
    max_tiles = -(-2 * (S + n_s) // MOE_TILE) + N_EXPERTS
    tiles_per = meta[0, :N_EXPERTS].astype(jnp.int32)
    ends = tiles_per + meta[1, :N_EXPERTS].astype(jnp.int32)
    n_used = ends[N_EXPERTS - 1]
    tile_src = jnp.minimum(jnp.arange(max_tiles, dtype=jnp.int32), n_used - 1)
    tile_expert = jnp.minimum(jnp.sum(tile_src[:, None] >= ends[None, :], axis=1), N_EXPERTS - 1).astype(jnp.int32)
    tail = n_used + jnp.arange(N_EXPERTS, dtype=jnp.int32)
    zero_tiles = jnp.concatenate([jnp.where(tiles_per > 0, ends - 1, -1), jnp.where(tail < max_tiles, tail, -1)])
    x_sorted = dispatch(p1, p2, zero_tiles, h2_p, h2_s, max_tiles * MOE_TILE, tm=ROW_TILE)
    o_sorted = experts(tile_expert, tile_src, n_used[None], x_sorted, gf, w_gate, w_up, w_down, layer=l)
    y_prompt = combine(p1[:S], p2[:S], h2_p, rt_p, o_sorted, tm=ROW_TILE).reshape(B, S, D_MODEL)
    y_sample = combine(p1[S:], p2[S:], h2_s, rt_s, o_sorted, tm=ROW_TILE).reshape(DB, T, D_MODEL)

    hd = (N_HEADS, HEAD_DIM)
    return (
        y_prompt,
        y_sample,
        k_p.reshape(1, B, S, *hd),
        v_p.reshape(1, B, S, *hd),
        u_p[S - POOL_STATE:].reshape(1, B, POOL_STATE, POOL_WIDTH),
        k_s.reshape(1, DB, T, *hd),
        v_s.reshape(1, DB, T, *hd),
        pool_s.reshape(1, DB, POOL_STATE, POOL_WIDTH),
    )
```

```python
import functools

import jax
import jax.numpy as jnp
from jax import lax
from jax.experimental import pallas as pl
from jax.experimental.pallas import tpu as pltpu

D_MODEL = 2048
ATTN_WIDTH = 1024
POOL_WIDTH = 1024
HEAD_DIM = 128
N_HEADS = 8
POOL_WINDOWS = (2, 4, 8, 16)
POOL_GROUP_W = 256
POOL_STATE = 15
MOBA_BLOCK = 256
MOBA_TOPK = 3
N_EXPERT_GROUPS = 4
EXPERTS_PER_GROUP = 4
N_EXPERTS = 16
D_EXPERT = 768
EPS = 1e-6
NEG = -1e30
LANES = 128
VMEM_LIMIT = 56 * 1024 * 1024
ATTN_VMEM_LIMIT = 60 * 1024 * 1024
EXPERT_VMEM_LIMIT = 60 * 1024 * 1024

BF16 = jnp.bfloat16
F32 = jnp.float32


def _cparams(sem, vmem_limit=VMEM_LIMIT):
    return pltpu.CompilerParams(dimension_semantics=sem, vmem_limit_bytes=vmem_limit)


def _dot(a, b):
    return jnp.dot(a, b, preferred_element_type=F32)


def _dot_nt(a, b, precision=None):
    return lax.dot_general(a, b, (((1,), (1,)), ((), ())), precision=precision,
                           preferred_element_type=F32)


def _top_mask(g, ids, k, axis=-1):
    sel = jnp.zeros(g.shape, F32)
    for _ in range(k):
        m = jnp.max(g, axis=axis, keepdims=True)
        idx = jnp.min(jnp.where(g == m, ids, jnp.float32(g.shape[axis])), axis=axis, keepdims=True)
        pick = ids == idx
        sel = jnp.where(pick, 1.0, sel)
        g = jnp.where(pick, -jnp.inf, g)
    return sel


def _in_proj_kernel(x_ref, gmix_ref, w_ref, gq_ref, gk_ref,
                    q_ref, k_ref, kb_ref, km_ref, v_ref, vb_ref, u_ref, *rest, tm, with_pool):
    if with_pool:
        d_ref, hb_ref, ext_ref = rest
    else:
        (hb_ref,) = rest
    xf = x_ref[...]
    ms = jnp.mean(xf * xf, axis=-1, keepdims=True)
    hb_ref[...] = (xf * lax.rsqrt(ms + EPS) * gmix_ref[...]).astype(BF16)

    def head_norm(z, out_ref, g_ref):
        for h in range(N_HEADS):
            zh = z[:, h * HEAD_DIM:(h + 1) * HEAD_DIM]
            ms = jnp.mean(zh * zh, axis=-1, keepdims=True)
            out_ref[:, h * HEAD_DIM:(h + 1) * HEAD_DIM] = zh * lax.rsqrt(ms + EPS) * g_ref[...]

    head_norm(_dot(hb_ref[...], w_ref[:, 0:ATTN_WIDTH]), q_ref, gq_ref)
    head_norm(_dot(hb_ref[...], w_ref[:, ATTN_WIDTH:2 * ATTN_WIDTH]), k_ref, gk_ref)
    kb_ref[...] = k_ref[...].astype(BF16)
    for g in range(tm // MOBA_BLOCK):
        blk = k_ref[g * MOBA_BLOCK:(g + 1) * MOBA_BLOCK, :]
        km_ref[g] = jnp.mean(blk, axis=0, keepdims=True)
    zv = _dot(hb_ref[...], w_ref[:, 2 * ATTN_WIDTH:3 * ATTN_WIDTH])
    v_ref[...] = zv
    vb_ref[...] = zv.astype(BF16)
    zu = _dot(hb_ref[...], w_ref[:, 3 * ATTN_WIDTH:])
    u_ref[...] = zu
    if with_pool:
        _pool_differences(zu, d_ref, ext_ref, tm=tm)


POOL_PAD = 16


def _pool_differences(u, d_ref, ext_ref, *, tm):
    i = pl.program_id(0)

    @pl.when(i == 0)
    def _():
        ext_ref[0:POOL_PAD, :] = jnp.zeros((POOL_PAD, POOL_WIDTH), F32)

    ext_ref[POOL_PAD:POOL_PAD + tm, :] = u
    pos = (i * tm + lax.broadcasted_iota(jnp.int32, (tm, 1), 0)).astype(F32)
    for g, w in enumerate(POOL_WINDOWS):
        c0, c1 = g * POOL_GROUP_W, (g + 1) * POOL_GROUP_W
        wsum = u[:, c0:c1]
        for jj in range(1, w):
            wsum = wsum + ext_ref[POOL_PAD - jj:POOL_PAD - jj + tm, c0:c1]
        inv = 1.0 / jnp.minimum(jnp.float32(w), pos + 1.0)
        d_ref[:, c0:c1] = (wsum * inv - u[:, c0:c1]).astype(d_ref.dtype)
    ext_ref[0:POOL_PAD, :] = ext_ref[tm:tm + POOL_PAD, :]


def in_proj(x, g_mix, w_in_bf, g_q, g_k, *, tm, with_pool):
    n = x.shape[0]
    wide = lambda i: (i, 0)
    const = lambda i: (0, 0)
    out_shape = (
        jax.ShapeDtypeStruct((n, ATTN_WIDTH), F32),
        jax.ShapeDtypeStruct((n, ATTN_WIDTH), F32),
        jax.ShapeDtypeStruct((n, ATTN_WIDTH), BF16),
        jax.ShapeDtypeStruct((n // MOBA_BLOCK, 1, ATTN_WIDTH), F32),
        jax.ShapeDtypeStruct((n, ATTN_WIDTH), F32),
        jax.ShapeDtypeStruct((n, ATTN_WIDTH), BF16),
        jax.ShapeDtypeStruct((n, POOL_WIDTH), F32),
    )
    blk = pl.BlockSpec((tm, ATTN_WIDTH), wide)
    pool_out = (jax.ShapeDtypeStruct((n, POOL_WIDTH), BF16),) if with_pool else ()
    pool_scratch = [pltpu.VMEM((POOL_PAD + tm, POOL_WIDTH), F32)] if with_pool else []
    return pl.pallas_call(
        functools.partial(_in_proj_kernel, tm=tm, with_pool=with_pool),
        grid=(n // tm,),
        in_specs=[
            pl.BlockSpec((tm, D_MODEL), wide),
            pl.BlockSpec((1, D_MODEL), const),
            pl.BlockSpec(w_in_bf.shape, const, pipeline_mode=pl.Buffered(1)),
            pl.BlockSpec((1, HEAD_DIM), const),
            pl.BlockSpec((1, HEAD_DIM), const),
        ],
        out_specs=(blk, blk, blk,
                   pl.BlockSpec((tm // MOBA_BLOCK, 1, ATTN_WIDTH), lambda i: (i, 0, 0)),
                   blk, blk, blk) + (blk,) * len(pool_out),
        out_shape=out_shape + pool_out,
        scratch_shapes=[pltpu.VMEM((tm, D_MODEL), BF16)] + pool_scratch,
        compiler_params=_cparams(("arbitrary",)),
        name="in_proj",
    )(x, g_mix, w_in_bf, g_q, g_k)


LOG2E = 1.4426950408889634
ATT_HEADS = 2
ATT_GROUP = 4


def _moba_prompt_prologue(own, q_ref, k_ref, vt_ref, km_ref, qs_ref, bias_ref, m_ref, l_ref, acc_ref, *, nb, hp):
    base = pl.multiple_of(own * MOBA_BLOCK, MOBA_BLOCK)
    blk_id = lax.broadcasted_iota(jnp.int32, (nb, MOBA_BLOCK), 0).astype(F32)
    valid = blk_id < own.astype(F32)
    key_i = lax.broadcasted_iota(jnp.int32, (MOBA_BLOCK, MOBA_BLOCK), 0)
    qry_i = lax.broadcasted_iota(jnp.int32, (MOBA_BLOCK, MOBA_BLOCK), 1)

    for hh in range(hp):
        cols = slice(hh * HEAD_DIM, (hh + 1) * HEAD_DIM)
        q = q_ref[:, cols]
        gate = _dot_nt(km_ref[:, cols], q, precision=lax.Precision.HIGHEST)
        sel = _top_mask(jnp.where(valid, gate, NEG), blk_id, MOBA_TOPK, axis=0)
        bias = jnp.where(jnp.logical_and(sel > 0.5, valid), 0.0, NEG)
        for n in range(nb):
            bias_ref[hh, n] = bias[n:n + 1, :]
        qs = (q * (HEAD_DIM ** -0.5 * LOG2E)).astype(BF16)
        qs_ref[hh] = qs
        s = _dot_nt(k_ref[pl.ds(base, MOBA_BLOCK), cols], qs)
        s = jnp.where(key_i <= qry_i, s, NEG)
        m0 = jnp.max(s, axis=0, keepdims=True)
        p = jnp.exp2(s - m0)
        m_ref[hh] = m0
        l_ref[hh] = jnp.sum(p, axis=0, keepdims=True)
        acc_ref[hh] = _dot(vt_ref[own, cols, :], p.astype(BF16))


def _moba_prompt_main(own, k_ref, vt_ref, o_ref, qs_ref, bias_ref, m_ref, l_ref, acc_ref, s_ref, mb_ref,
                      *, nb, hp, grp):
    n_iter = (own + grp - 1) // grp

    def scores(i, slot):
        for hh in range(hp):
            cols = slice(hh * HEAD_DIM, (hh + 1) * HEAD_DIM)
            qs = qs_ref[hh]
            mb = None
            for gg in range(grp):
                n = jnp.minimum(i * grp + gg, nb - 1)
                off = pl.multiple_of(n * MOBA_BLOCK, MOBA_BLOCK)
                s = _dot_nt(k_ref[pl.ds(off, MOBA_BLOCK), cols], qs) + bias_ref[hh, n]
                s_ref[slot, hh, gg] = s
                smax = jnp.max(s, axis=0, keepdims=True)
                mb = smax if mb is None else jnp.maximum(mb, smax)
            mb_ref[slot, hh] = mb

    def softmax_pv(i, slot):
        for hh in range(hp):
            cols = slice(hh * HEAD_DIM, (hh + 1) * HEAD_DIM)
            m_prev = m_ref[hh]
            m_new = jnp.maximum(m_prev, mb_ref[slot, hh])
            alpha = jnp.exp2(m_prev - m_new)
            l_new = alpha * l_ref[hh]
            acc = alpha * acc_ref[hh]
            for gg in range(grp):
                p = jnp.exp2(s_ref[slot, hh, gg] - m_new)
                l_new = l_new + jnp.sum(p, axis=0, keepdims=True)
                acc = acc + _dot(vt_ref[i * grp + gg, cols, :], p.astype(BF16))
            m_ref[hh] = m_new
            l_ref[hh] = l_new
            acc_ref[hh] = acc

    scores(0, 0)

    def body(k, carry):
        scores(2 * k + 1, 1)
        softmax_pv(2 * k, 0)
        scores(2 * k + 2, 0)
        softmax_pv(2 * k + 1, 1)
        return carry

    lax.fori_loop(0, n_iter // 2, body, 0)

    @pl.when(n_iter % 2 == 1)
    def _():
        softmax_pv(n_iter - 1, 0)

    for hh in range(hp):
        o = acc_ref[hh] * (1.0 / l_ref[hh])
        o_ref[:, hh * HEAD_DIM:(hh + 1) * HEAD_DIM] = o.T.astype(o_ref.dtype)


def _prompt_scratch(nb, hp, grp):
    return [
        pltpu.VMEM((hp, MOBA_BLOCK, HEAD_DIM), BF16),
        pltpu.VMEM((hp, nb, 1, MOBA_BLOCK), F32),
        pltpu.VMEM((hp, 1, MOBA_BLOCK), F32),
        pltpu.VMEM((hp, 1, MOBA_BLOCK), F32),
        pltpu.VMEM((hp, HEAD_DIM, MOBA_BLOCK), F32),
        pltpu.VMEM((2, hp, grp, MOBA_BLOCK, MOBA_BLOCK), F32),
        pltpu.VMEM((2, hp, 1, MOBA_BLOCK), F32),
    ]


def _moba_sample_body(q_ref, kn_ref, vn_ref, k_pages, v_pages, o_ref, s_ref, *, t_new, between):
    n_pages = len(k_pages)
    pw = k_pages[0].shape[0]
    page = pw // N_HEADS
    rows = t_new * N_HEADS
    per_blk = MOBA_BLOCK // page
    n_blocks = n_pages // per_blk

    qa = (q_ref[0] * (HEAD_DIM ** -0.5)).astype(BF16)
    r_i = lax.broadcasted_iota(jnp.int32, (rows, pw), 0)
    c_i = lax.broadcasted_iota(jnp.int32, (rows, pw), 1)
    same_head = (c_i % N_HEADS) == (r_i % N_HEADS)

    gcol = lax.broadcasted_iota(jnp.int32, (rows, LANES), 1).astype(F32)
    gate = jnp.full((rows, LANES), NEG, F32)
    bmax = jnp.full((rows, LANES), NEG, F32)
    for n in range(n_blocks):
        ssum = jnp.zeros((rows, pw), F32)
        smax = jnp.full((rows, pw), NEG, F32)
        for pp in range(per_blk):
            p_i = n * per_blk + pp
            sp = _dot_nt(qa, k_pages[p_i][...].astype(BF16))
            s_ref[:, p_i * pw:(p_i + 1) * pw] = sp
            ssum = ssum + jnp.where(same_head, sp, 0.0)
            smax = jnp.maximum(smax, jnp.where(same_head, sp, NEG))
        tot = jnp.sum(ssum, axis=-1, keepdims=True) * (1.0 / MOBA_BLOCK)
        gate = jnp.where(gcol == float(n), tot, gate)
        bmax = jnp.where(gcol == float(n), jnp.max(smax, axis=-1, keepdims=True), bmax)
    sel = _top_mask(gate, gcol, min(MOBA_TOPK, n_blocks))
    between()

    s_own = _dot_nt(qa, kn_ref[0].astype(BF16))
    orow = lax.broadcasted_iota(jnp.int32, s_own.shape, 0)
    ocol = lax.broadcasted_iota(jnp.int32, s_own.shape, 1)
    own_ok = jnp.logical_and(ocol % N_HEADS == orow % N_HEADS, ocol // N_HEADS <= orow // N_HEADS)
    s_own = jnp.where(own_ok, s_own, NEG)

    m = jnp.maximum(jnp.max(s_own, axis=-1, keepdims=True),
                    jnp.max(jnp.where(sel > 0.5, bmax, NEG), axis=-1, keepdims=True))
    p_own = jnp.exp(s_own - m)
    l = jnp.sum(p_own, axis=-1, keepdims=True)
    acc = _dot(p_own.astype(BF16), vn_ref[0].astype(BF16))
    lsum = jnp.zeros((rows, pw), F32)
    for p_i in range(n_pages):
        n = p_i // per_blk
        sp = s_ref[:, p_i * pw:(p_i + 1) * pw]
        keep = jnp.logical_and(same_head, sel[:, n:n + 1] > 0.5)
        pp = jnp.where(keep, jnp.exp(sp - m), 0.0)
        lsum = lsum + pp
        acc = acc + _dot(pp.astype(BF16), v_pages[p_i][...].astype(BF16))
    l = l + jnp.sum(lsum, axis=-1, keepdims=True)
    o_ref[0] = acc * (1.0 / l)


def _moba_kernel(pt_ref, q_ref, k_ref, vt_ref, km_ref, qs_ref, kn_ref, vn_ref, ck_ref, cv_ref, o_ref, os_ref,
                 qsc_ref, bias_ref, m_ref, l_ref, acc_ref, s_ref, mb_ref, sc_ref, kbuf_ref, vbuf_ref, page_sem,
                 *, nb, hp, grp, n_pages, t_new):
    i = pl.program_id(0)
    own = i % nb
    slot = i % 2

    def fetch(row, sl):
        for p in range(n_pages):
            pg = pt_ref[row, p]
            pltpu.make_async_copy(ck_ref.at[pg], kbuf_ref.at[sl, p], page_sem.at[sl]).start()
            pltpu.make_async_copy(cv_ref.at[pg], vbuf_ref.at[sl, p], page_sem.at[sl]).start()

    @pl.when(i == 0)
    def _():
        fetch(0, 0)

    @pl.when(i + 1 < pl.num_programs(0))
    def _():
        fetch(i + 1, 1 - slot)

    for src, buf in ((ck_ref, kbuf_ref), (cv_ref, vbuf_ref)):
        pltpu.make_async_copy(src.at[pl.ds(0, n_pages)], buf.at[slot], page_sem.at[slot]).wait()
    k_pages = [kbuf_ref.at[slot, p] for p in range(n_pages)]
    v_pages = [vbuf_ref.at[slot, p] for p in range(n_pages)]

    def prompt_prologue():
        _moba_prompt_prologue(own, q_ref, k_ref, vt_ref, km_ref, qsc_ref, bias_ref, m_ref, l_ref, acc_ref,
                              nb=nb, hp=hp)

    _moba_sample_body(qs_ref, kn_ref, vn_ref, k_pages, v_pages, os_ref, sc_ref, t_new=t_new,
                      between=prompt_prologue)
    _moba_prompt_main(own, k_ref, vt_ref, o_ref, qsc_ref, bias_ref, m_ref, l_ref, acc_ref, s_ref, mb_ref,
                      nb=nb, hp=hp, grp=grp)


def moba_attention(q, k_bf, v_bf, kmean, q_s, k_new, v_new, cache_k, cache_v, page_table):
    s = q.shape[0]
    nb = s // MOBA_BLOCK
    hp = ATT_HEADS
    grp = ATT_GROUP if nb % ATT_GROUP == 0 else 1
    wide = hp * HEAD_DIM
    vt = v_bf.reshape(nb, MOBA_BLOCK, ATTN_WIDTH).transpose(0, 2, 1)
    db, t_new, _ = q_s.shape
    th = t_new * N_HEADS
    q_s, k_new, v_new = (a.reshape(db, th, HEAD_DIM) for a in (q_s, k_new, v_new))
    assert db == (N_HEADS // hp) * nb, "one sample row per prompt tile step"
    n_pages = page_table.shape[1]
    page = cache_k.shape[-3]
    ck = cache_k.reshape(-1, page * N_HEADS, HEAD_DIM)
    cv = cache_v.reshape(-1, page * N_HEADS, HEAD_DIM)
    tok = pl.BlockSpec((1, th, HEAD_DIM), lambda i, pt: (i, 0, 0))
    tile = pl.BlockSpec((MOBA_BLOCK, wide), lambda i, pt: (i % nb, i // nb))

    page_rows = page * N_HEADS
    grid_spec = pltpu.PrefetchScalarGridSpec(
        num_scalar_prefetch=1,
        grid=(db,),
        in_specs=[
            tile,
            pl.BlockSpec((s, wide), lambda i, pt: (0, i // nb)),
            pl.BlockSpec((nb, wide, MOBA_BLOCK), lambda i, pt: (0, i // nb, 0)),
            pl.BlockSpec((nb, wide), lambda i, pt: (0, i // nb)),
            tok, tok, tok,
            pl.BlockSpec(memory_space=pl.ANY), pl.BlockSpec(memory_space=pl.ANY),
        ],
        out_specs=(tile, tok),
        scratch_shapes=_prompt_scratch(nb, hp, grp) + [
            pltpu.VMEM((th, n_pages * page_rows), F32),
            pltpu.VMEM((2, n_pages, page_rows, HEAD_DIM), F32),
            pltpu.VMEM((2, n_pages, page_rows, HEAD_DIM), F32),
            pltpu.SemaphoreType.DMA((2,)),
        ],
    )
    o_p, o_s = pl.pallas_call(
        functools.partial(_moba_kernel, nb=nb, hp=hp, grp=grp, n_pages=n_pages, t_new=t_new),
        grid_spec=grid_spec,
        out_shape=(jax.ShapeDtypeStruct((s, ATTN_WIDTH), BF16),
                   jax.ShapeDtypeStruct((db, th, HEAD_DIM), F32)),
        compiler_params=_cparams(("arbitrary",), ATTN_VMEM_LIMIT),
        name="moba_attention",
    )(page_table, q, k_bf, vt, kmean, q_s, k_new, v_new, ck, cv)
    return o_p, o_s.reshape(db, t_new, ATTN_WIDTH)


def _pool_sample_kernel(u_ref, st_ref, d_ref, new_ref, *, t_new, start_pos):
    ext = [st_ref[:, r, :] for r in range(POOL_STATE)] + [u_ref[:, t, :] for t in range(t_new)]
    for t in range(t_new):
        e = POOL_STATE + t
        parts = []
        for g, w in enumerate(POOL_WINDOWS):
            c0, c1 = g * POOL_GROUP_W, (g + 1) * POOL_GROUP_W
            wsum = ext[e][:, c0:c1]
            for jj in range(1, w):
                wsum = wsum + ext[e - jj][:, c0:c1]
            count = min(float(w), float(start_pos + t) + 1.0)
            parts.append(wsum * (1.0 / count) - ext[e][:, c0:c1])
        d_ref[:, t, :] = jnp.concatenate(parts, axis=-1).astype(d_ref.dtype)
    for r in range(POOL_STATE):
        new_ref[:, r, :] = ext[t_new + r]


def pool_sample(u, state, *, start_pos, bb=32):
    db, t_new, _ = u.shape
    return pl.pallas_call(
        functools.partial(_pool_sample_kernel, t_new=t_new, start_pos=start_pos),
        grid=(db // bb,),
        in_specs=[
            pl.BlockSpec((bb, t_new, POOL_WIDTH), lambda i: (i, 0, 0)),
            pl.BlockSpec((bb, POOL_STATE, POOL_WIDTH), lambda i: (i, 0, 0)),
        ],
        out_specs=(
            pl.BlockSpec((bb, t_new, POOL_WIDTH), lambda i: (i, 0, 0)),
            pl.BlockSpec((bb, POOL_STATE, POOL_WIDTH), lambda i: (i, 0, 0)),
        ),
        out_shape=(
            jax.ShapeDtypeStruct((db, t_new, POOL_WIDTH), F32),
            jax.ShapeDtypeStruct((db, POOL_STATE, POOL_WIDTH), F32),
        ),
        compiler_params=_cparams(("arbitrary",)),
        name="pool_sample",
    )(u, state)


def _out_proj_kernel(x_ref, oa_ref, d_ref, wp_ref, ps_ref, wo_ref, gf_ref, wr_ref, br_ref,
                     h2_ref, rt_ref, mix_ref):
    mix_ref[:, 0:ATTN_WIDTH] = oa_ref[...].astype(BF16)
    dd = d_ref[...].astype(BF16)
    for g in range(len(POOL_WINDOWS)):
        c0, c1 = g * POOL_GROUP_W, (g + 1) * POOL_GROUP_W
        yg = _dot(dd[:, c0:c1], wp_ref[g].astype(BF16)) * ps_ref[:, c0:c1]
        mix_ref[:, ATTN_WIDTH + c0:ATTN_WIDTH + c1] = yg.astype(BF16)
    h2 = x_ref[...] + _dot(mix_ref[...], wo_ref[...])
    h2_ref[...] = h2
    ms = jnp.mean(h2 * h2, axis=-1, keepdims=True)
    hn = h2 * lax.rsqrt(ms + EPS) * gf_ref[...]

    hn_hi = hn.astype(BF16)
    hn_lo = (hn - hn_hi.astype(F32)).astype(BF16)
    t = _dot(hn_hi, wr_ref[...])
    logits = t[:, :LANES] + t[:, LANES:] + _dot(hn_lo, wr_ref[:, :LANES]) + br_ref[...]
    lane = lax.broadcasted_iota(jnp.int32, logits.shape, 1).astype(F32)
    far = jnp.float32(LANES)
    is_g = lane < N_EXPERT_GROUPS
    gl = jnp.where(is_g, logits, -jnp.inf)
    g_max = jnp.max(gl, axis=-1, keepdims=True)
    g_top = jnp.min(jnp.where(gl == g_max, lane, far), axis=-1, keepdims=True)
    g_p = 1.0 / jnp.sum(jnp.where(is_g, jnp.exp(gl - g_max), 0.0), axis=-1, keepdims=True)
    lo = N_EXPERT_GROUPS + g_top * EXPERTS_PER_GROUP
    in_grp = jnp.logical_and(lane >= lo, lane < lo + EXPERTS_PER_GROUP)
    el = jnp.where(in_grp, logits, -jnp.inf)
    e1 = jnp.max(el, axis=-1, keepdims=True)
    i1 = jnp.min(jnp.where(el == e1, lane, far), axis=-1, keepdims=True)
    el2 = jnp.where(lane == i1, -jnp.inf, el)
    e2 = jnp.max(el2, axis=-1, keepdims=True)
    i2 = jnp.min(jnp.where(el2 == e2, lane, far), axis=-1, keepdims=True)
    ex2 = jnp.exp(e2 - e1)
    den = 1.0 + ex2
    w1 = (1.0 / den) * g_p
    w2 = (ex2 / den) * g_p
    rt_ref[...] = jnp.where(lane == 0.0, i1 - N_EXPERT_GROUPS,
                            jnp.where(lane == 1.0, i2 - N_EXPERT_GROUPS,
                                      jnp.where(lane == 2.0, w1, jnp.where(lane == 3.0, w2, 0.0))))


def out_proj(x, o_attn, d, w_pool, pool_scale, w_out_bf, g_ffn, w_router, b_router, *, tm):
    n = x.shape[0]
    row = lambda i: (i, 0)
    const = lambda i: (0, 0)
    return pl.pallas_call(
        _out_proj_kernel,
        grid=(n // tm,),
        in_specs=[
            pl.BlockSpec((tm, D_MODEL), row),
            pl.BlockSpec((tm, ATTN_WIDTH), row),
            pl.BlockSpec((tm, POOL_WIDTH), row),
            pl.BlockSpec(w_pool.shape, lambda i: (0, 0, 0)),
            pl.BlockSpec((1, POOL_WIDTH), const),
            pl.BlockSpec((D_MODEL, D_MODEL), const),
            pl.BlockSpec((1, D_MODEL), const),
            pl.BlockSpec((D_MODEL, 2 * LANES), const),
            pl.BlockSpec((1, LANES), const),
        ],
        out_specs=(
            pl.BlockSpec((tm, D_MODEL), row),
            pl.BlockSpec((tm, LANES), row),
        ),
        out_shape=(
            jax.ShapeDtypeStruct((n, D_MODEL), F32),
            jax.ShapeDtypeStruct((n, LANES), F32),
        ),
        scratch_shapes=[pltpu.VMEM((tm, D_MODEL), BF16)],
        compiler_params=_cparams(("arbitrary",)),
        name="out_proj",
    )(x, o_attn, d, w_pool, pool_scale, w_out_bf, g_ffn, w_router, b_router)


MOE_TILE = 256
ROW_TILE = 256


def _route_kernel(rt_ref, pos_ref, meta_ref, *, n_tiles):
    lane = lax.broadcasted_iota(jnp.int32, (ROW_TILE, LANES), 1).astype(F32)
    r_i = lax.broadcasted_iota(jnp.int32, (ROW_TILE, ROW_TILE), 0)
    c_i = lax.broadcasted_iota(jnp.int32, (ROW_TILE, ROW_TILE), 1)
    tri = jnp.where(c_i < r_i, 1.0, 0.0).astype(BF16)

    def one_hot(t):
        rt = rt_ref[pl.ds(pl.multiple_of(t * ROW_TILE, ROW_TILE), ROW_TILE), :]
        e1, e2 = rt[:, 0:1], rt[:, 1:2]
        return e1, e2, jnp.where(jnp.logical_or(lane == e1, lane == e2), 1.0, 0.0)

    def count(t, cnt):
        return cnt + jnp.sum(one_hot(t)[2], axis=0, keepdims=True)

    cnt = lax.fori_loop(0, n_tiles, count, jnp.zeros((1, LANES), F32))
    tiles_per = jnp.floor((cnt + (MOE_TILE - 1)) * (1.0 / MOE_TILE))
    e_r = lax.broadcasted_iota(jnp.int32, (LANES, LANES), 0)
    e_c = lax.broadcasted_iota(jnp.int32, (LANES, LANES), 1)
    upper = jnp.where(e_r < e_c, 1.0, 0.0).astype(BF16)
    off_tiles = _dot(jnp.broadcast_to(tiles_per, (8, LANES)).astype(BF16), upper)[0:1]
    base = off_tiles * MOE_TILE

    def place(t, run):
        e1, e2, oh = one_hot(t)
        dest = base + run + _dot(tri, oh.astype(BF16))
        p1 = jnp.sum(jnp.where(lane == e1, dest, 0.0), axis=1, keepdims=True)
        p2 = jnp.sum(jnp.where(lane == e2, dest, 0.0), axis=1, keepdims=True)
        pos = jnp.where(lane == 0.0, p1, jnp.where(lane == 1.0, p2, 0.0))
        pos_ref[pl.ds(pl.multiple_of(t * ROW_TILE, ROW_TILE), ROW_TILE), :] = pos.astype(jnp.int32)
        return run + jnp.sum(oh, axis=0, keepdims=True)

    lax.fori_loop(0, n_tiles, place, jnp.zeros((1, LANES), F32))
    row = lax.broadcasted_iota(jnp.int32, (8, LANES), 0)
    meta_ref[...] = jnp.where(row == 0, tiles_per, jnp.where(row == 1, off_tiles, 0.0))


def route(rt):
    n = rt.shape[0]
    assert n % ROW_TILE == 0
    return pl.pallas_call(
        functools.partial(_route_kernel, n_tiles=n // ROW_TILE),
        out_shape=(
            jax.ShapeDtypeStruct((n, LANES), jnp.int32),
            jax.ShapeDtypeStruct((8, LANES), F32),
        ),
        compiler_params=pltpu.CompilerParams(vmem_limit_bytes=VMEM_LIMIT),
        name="route",
    )(rt)


ROW_UNROLL = 8


def _dispatch_kernel(p1_ref, p2_ref, zt_ref, xa_ref, xb_ref, xs_ref, zero_ref, sem, zsem, *, tm, tiles_a, n_zero):
    i = pl.program_id(0)

    @pl.when(i == 0)
    def _():
        zero_ref[...] = jnp.zeros(zero_ref.shape, zero_ref.dtype)

        def zcopy(z):
            row0 = pl.multiple_of(zt_ref[z] * MOE_TILE, MOE_TILE)
            return pltpu.make_async_copy(zero_ref, xs_ref.at[pl.ds(row0, MOE_TILE)], zsem)

        for z in range(n_zero):
            @pl.when(zt_ref[z] >= 0)
            def _():
                zcopy(z).start()
        for z in range(n_zero):
            @pl.when(zt_ref[z] >= 0)
            def _():
                zcopy(z).wait()

    def scatter(x_ref):
        t0 = i * tm

        def issue(r, c):
            src = x_ref.at[pl.ds(r, 1)]
            pltpu.make_async_copy(src, xs_ref.at[pl.ds(p1_ref[t0 + r], 1)], sem).start()
            pltpu.make_async_copy(src, xs_ref.at[pl.ds(p2_ref[t0 + r], 1)], sem).start()
            return c

        lax.fori_loop(0, tm, issue, 0, unroll=ROW_UNROLL)
        for _ in range(2):
            pltpu.make_async_copy(x_ref, xs_ref.at[pl.ds(0, tm)], sem).wait()

    @pl.when(i < tiles_a)
    def _():
        scatter(xa_ref)

    @pl.when(i >= tiles_a)
    def _():
        scatter(xb_ref)


def dispatch(p1, p2, zero_tiles, xa, xb, rows, *, tm):
    tiles_a, tiles_b = xa.shape[0] // tm, xb.shape[0] // tm
    grid_spec = pltpu.PrefetchScalarGridSpec(
        num_scalar_prefetch=3,
        grid=(tiles_a + tiles_b,),
        in_specs=[
            pl.BlockSpec((tm, D_MODEL), lambda i, a, b, z: (jnp.minimum(i, tiles_a - 1), 0)),
            pl.BlockSpec((tm, D_MODEL), lambda i, a, b, z: (jnp.maximum(i - tiles_a, 0), 0)),
        ],
        out_specs=pl.BlockSpec(memory_space=pl.ANY),
        scratch_shapes=[pltpu.VMEM((MOE_TILE, D_MODEL), F32), pltpu.SemaphoreType.DMA(()), pltpu.SemaphoreType.DMA(())],
    )
    return pl.pallas_call(
        functools.partial(_dispatch_kernel, tm=tm, tiles_a=tiles_a, n_zero=zero_tiles.shape[0]),
        grid_spec=grid_spec,
        out_shape=jax.ShapeDtypeStruct((rows, D_MODEL), F32),
        compiler_params=_cparams(("arbitrary",)),
        name="dispatch",
    )(p1, p2, zero_tiles, xa, xb)


def _experts_kernel(te_ref, ts_ref, tf_ref, sl_ref, nx_ref, nu_ref, x_ref, gf_ref, wg_hbm, wu_hbm, wd_hbm, o_ref,
                    wg_buf, wu_buf, wd_buf, wsem, *, layer):
    j = pl.program_id(0)
    used = j < nu_ref[0]
    first = jnp.logical_and(used, tf_ref[j] == 1)
    slot = sl_ref[j]

    def copies(e, sl):
        return (pltpu.make_async_copy(wg_hbm.at[layer, e], wg_buf.at[sl], wsem.at[sl]),
                pltpu.make_async_copy(wu_hbm.at[layer, e], wu_buf.at[sl], wsem.at[sl]),
                pltpu.make_async_copy(wd_hbm.at[layer, e], wd_buf.at[sl], wsem.at[sl]))

    @pl.when(j == 0)
    def _():
        for c in copies(te_ref[0], 0):
            c.start()

    @pl.when(jnp.logical_and(first, nx_ref[j] >= 0))
    def _():
        for c in copies(nx_ref[j], 1 - slot):
            c.start()

    @pl.when(first)
    def _():
        for c in copies(te_ref[j], slot):
            c.wait()

    @pl.when(used)
    def _():
        h = x_ref[...]
        ms = jnp.mean(h * h, axis=-1, keepdims=True)
        x = (h * lax.rsqrt(ms + EPS) * gf_ref[...]).astype(BF16).astype(F32)
        a = _dot(x, wg_buf[slot])
        b = _dot(x, wu_buf[slot])
        act = (a * (1.0 / (1.0 + jnp.exp(-a)))) * b
        o_ref[...] = _dot(act.astype(BF16).astype(F32), wd_buf[slot])

    @pl.when(jnp.logical_not(used))
    def _():
        o_ref[...] = jnp.zeros(o_ref.shape, o_ref.dtype)


def experts(tile_expert, tile_src, tile_first, tile_slot, tile_next, n_used, xs, g_ffn, w_gate, w_up, w_down, *, layer):
    rows, w = xs.shape
    idx = lambda f: (lambda j, te, ts, tf, sl, nx, nu: f(j, ts))
    grid_spec = pltpu.PrefetchScalarGridSpec(
        num_scalar_prefetch=6,
        grid=(rows // MOE_TILE,),
        in_specs=[
            pl.BlockSpec((MOE_TILE, w), idx(lambda j, ts: (ts[j], 0))),
            pl.BlockSpec((1, D_MODEL), idx(lambda j, ts: (0, 0))),
            pl.BlockSpec(memory_space=pl.ANY), pl.BlockSpec(memory_space=pl.ANY), pl.BlockSpec(memory_space=pl.ANY),
        ],
        out_specs=pl.BlockSpec((MOE_TILE, D_MODEL), idx(lambda j, ts: (j, 0))),
        scratch_shapes=[
            pltpu.VMEM((2, D_MODEL, D_EXPERT), F32), pltpu.VMEM((2, D_MODEL, D_EXPERT), F32),
            pltpu.VMEM((2, D_EXPERT, D_MODEL), F32), pltpu.SemaphoreType.DMA((2,)),
        ],
    )
    return pl.pallas_call(
        functools.partial(_experts_kernel, layer=layer),
        grid_spec=grid_spec,
        out_shape=jax.ShapeDtypeStruct((rows, D_MODEL), F32),
        compiler_params=_cparams(("arbitrary",), EXPERT_VMEM_LIMIT),
        name="experts",
    )(tile_expert, tile_src, tile_first, tile_slot, tile_next, n_used, xs, g_ffn, w_gate, w_up, w_down)


def _combine_kernel(p1_ref, p2_ref, h2_ref, rt_ref, os_ref, y_ref, buf_ref, sem, *, tm):
    i = pl.program_id(0)
    slot = i % 2

    def gather(step, sl):
        t0 = step * tm

        def issue(r, c):
            pltpu.make_async_copy(os_ref.at[pl.ds(p1_ref[t0 + r], 1)], buf_ref.at[sl, 0, pl.ds(r, 1)],
                                  sem.at[sl]).start()
            pltpu.make_async_copy(os_ref.at[pl.ds(p2_ref[t0 + r], 1)], buf_ref.at[sl, 1, pl.ds(r, 1)],
                                  sem.at[sl]).start()
            return c

        lax.fori_loop(0, tm, issue, 0, unroll=ROW_UNROLL)

    @pl.when(i == 0)
    def _():
        gather(0, 0)

    @pl.when(i + 1 < pl.num_programs(0))
    def _():
        gather(i + 1, 1 - slot)

    for s in range(2):
        pltpu.make_async_copy(os_ref.at[pl.ds(0, tm)], buf_ref.at[slot, s], sem.at[slot]).wait()
    rt = rt_ref[...]
    y_ref[...] = h2_ref[...] + rt[:, 2:3] * buf_ref[slot, 0] + rt[:, 3:4] * buf_ref[slot, 1]


def combine(p1, p2, h2, rt, os, *, tm):
    n = h2.shape[0]
    row = lambda i, a, b: (i, 0)
    grid_spec = pltpu.PrefetchScalarGridSpec(
        num_scalar_prefetch=2,
        grid=(n // tm,),
        in_specs=[pl.BlockSpec((tm, D_MODEL), row), pl.BlockSpec((tm, LANES), row),
                  pl.BlockSpec(memory_space=pl.ANY)],
        out_specs=pl.BlockSpec((tm, D_MODEL), row),
        scratch_shapes=[pltpu.VMEM((2, 2, tm, D_MODEL), F32), pltpu.SemaphoreType.DMA((2,))],
    )
    return pl.pallas_call(
        functools.partial(_combine_kernel, tm=tm),
        grid_spec=grid_spec,
        out_shape=jax.ShapeDtypeStruct((n, D_MODEL), F32),
        compiler_params=_cparams(("arbitrary",)),
        name="combine",
    )(p1, p2, h2, rt, os)


def kernel(x_prompt, x_sample, cache_k, cache_v, state_pool, page_table, g_mix, w_in, g_q, g_k, w_pool, pool_scale, w_out, g_ffn, w_group_router, b_group_router, w_expert_router, b_expert_router, w_gate, w_up, w_down):
    B, S, _ = x_prompt.shape
    DB, T, _ = x_sample.shape
    depth = w_in.shape[0]
    assert B == 1 and depth == 1
    past_len = page_table.shape[1] * cache_k.shape[2]
    l = 0

    w_in_bf = w_in[l].astype(BF16)
    w_out_bf = w_out[l].astype(BF16)
    gm, gq, gk, gf = g_mix[l][None], g_q[l][None], g_k[l][None], g_ffn[l][None]
    ps = pool_scale[l][None]
    n_r = N_EXPERT_GROUPS + N_EXPERTS
    w_router = jnp.concatenate([w_group_router[l], w_expert_router[l].reshape(D_MODEL, N_EXPERTS)], axis=1)
    w_router = jnp.pad(w_router, ((0, 0), (0, LANES - n_r)))
    w_router_hi = w_router.astype(BF16)
    w_router = jnp.concatenate([w_router_hi, (w_router - w_router_hi.astype(F32)).astype(BF16)], axis=1)
    b_router = jnp.concatenate([b_group_router[l], b_expert_router[l].reshape(N_EXPERTS)])
    b_router = jnp.pad(b_router, (0, LANES - n_r))[None]

    def mixer_tail(x2d, o_attn, d):
        return out_proj(x2d, o_attn, d, w_pool[l], ps, w_out_bf, gf, w_router, b_router, tm=512)

    xp = x_prompt.reshape(S, D_MODEL)
    n_s = DB * T
    xs = x_sample.reshape(n_s, D_MODEL)
    q_p, k_p, kb_p, km_p, v_p, vb_p, u_p, d_p = in_proj(xp, gm, w_in_bf, gq, gk, tm=512, with_pool=True)
    q_s, k_s, _, _, v_s, _, u_s = in_proj(xs, gm, w_in_bf, gq, gk, tm=n_s, with_pool=False)
    r3 = lambda a: a.reshape(DB, T, ATTN_WIDTH)
    o_p, o_s = moba_attention(q_p, kb_p, vb_p, km_p.reshape(S // MOBA_BLOCK, ATTN_WIDTH),
                              r3(q_s), r3(k_s), r3(v_s), cache_k, cache_v, page_table + l * cache_k.shape[1])
    d_s, pool_s = pool_sample(u_s.reshape(DB, T, POOL_WIDTH), state_pool[l], start_pos=past_len)
    h2_p, rt_p = mixer_tail(xp, o_p, d_p)
    h2_s, rt_s = mixer_tail(xs, o_s.reshape(n_s, ATTN_WIDTH), d_s.reshape(n_s, POOL_WIDTH))

    pos, meta = route(jnp.concatenate([rt_p, rt_s], axis=0))
    p1, p2 = pos[:, 0], pos[:, 1]
    max_tiles = -(-2 * (S + n_s) // MOE_TILE) + N_EXPERTS
    tiles_per = meta[0, :N_EXPERTS].astype(jnp.int32)
    ends = tiles_per + meta[1, :N_EXPERTS].astype(jnp.int32)
    n_used = ends[N_EXPERTS - 1]
    tile_src = jnp.minimum(jnp.arange(max_tiles, dtype=jnp.int32), n_used - 1)
    tile_expert = jnp.minimum(jnp.sum(tile_src[:, None] >= ends[None, :], axis=1), N_EXPERTS - 1).astype(jnp.int32)
    tail = n_used + jnp.arange(N_EXPERTS, dtype=jnp.int32)
    zero_tiles = jnp.concatenate([jnp.where(tiles_per > 0, ends - 1, -1), jnp.where(tail < max_tiles, tail, -1)])
    x_sorted = dispatch(p1, p2, zero_tiles, h2_p, h2_s, max_tiles * MOE_TILE, tm=ROW_TILE)
    tile_ids = jnp.arange(max_tiles, dtype=jnp.int32)
    tile_first = jnp.logical_and(tile_ids < n_used,
                                 jnp.logical_or(tile_ids == 0, tile_expert != jnp.roll(tile_expert, 1))).astype(jnp.int32)
    tile_slot = ((jnp.cumsum(tile_first) - 1) % 2).astype(jnp.int32)
    e_ids = jnp.arange(N_EXPERTS, dtype=jnp.int32)
    later = jnp.logical_and(e_ids[None, :] > e_ids[:, None], tiles_per[None, :] > 0)
    next_nonempty = jnp.min(jnp.where(later, e_ids[None, :], N_EXPERTS), axis=1)
    tile_next = jnp.where(next_nonempty < N_EXPERTS, next_nonempty, -1)[tile_expert].astype(jnp.int32)
    o_sorted = experts(tile_expert, tile_src, tile_first, tile_slot, tile_next, n_used[None], x_sorted, gf,
                       w_gate, w_up, w_down, layer=l)
    y_prompt = combine(p1[:S], p2[:S], h2_p, rt_p, o_sorted, tm=ROW_TILE).reshape(B, S, D_MODEL)
    y_sample = combine(p1[S:], p2[S:], h2_s, rt_s, o_sorted, tm=ROW_TILE).reshape(DB, T, D_MODEL)

    hd = (N_HEADS, HEAD_DIM)
    return (
        y_prompt,
        y_sample,
        k_p.reshape(1, B, S, *hd),
        v_p.reshape(1, B, S, *hd),
        u_p[S - POOL_STATE:].reshape(1, B, POOL_STATE, POOL_WIDTH),
        k_s.reshape(1, DB, T, *hd),
        v_s.reshape(1, DB, T, *hd),
        pool_s.reshape(1, DB, POOL_STATE, POOL_WIDTH),
    )
```

```python
import functools

import jax
import jax.numpy as jnp
from jax import lax
from jax.experimental import pallas as pl
from jax.experimental.pallas import tpu as pltpu

D_MODEL = 2048
ATTN_WIDTH = 1024
POOL_WIDTH = 1024
HEAD_DIM = 128
N_HEADS = 8
POOL_WINDOWS = (2, 4, 8, 16)
POOL_GROUP_W = 256
POOL_STATE = 15
MOBA_BLOCK = 256
MOBA_TOPK = 3
N_EXPERT_GROUPS = 4
EXPERTS_PER_GROUP = 4
N_EXPERTS = 16
D_EXPERT = 768
EPS = 1e-6
NEG = -1e30
LANES = 128
VMEM_LIMIT = 56 * 1024 * 1024
PROJ_TILE = 512
ATTN_VMEM_LIMIT = 60 * 1024 * 1024
EXPERT_VMEM_LIMIT = 60 * 1024 * 1024

BF16 = jnp.bfloat16
F32 = jnp.float32


def _cparams(sem, vmem_limit=VMEM_LIMIT):
    return pltpu.CompilerParams(dimension_semantics=sem, vmem_limit_bytes=vmem_limit)


def _dot(a, b):
    return jnp.dot(a, b, preferred_element_type=F32)


def _dot_nt(a, b, precision=None):
    return lax.dot_general(a, b, (((1,), (1,)), ((), ())), precision=precision,
                           preferred_element_type=F32)


def _top_mask(g, ids, k, axis=-1):
    sel = jnp.zeros(g.shape, F32)
    for _ in range(k):
        m = jnp.max(g, axis=axis, keepdims=True)
        idx = jnp.min(jnp.where(g == m, ids, jnp.float32(g.shape[axis])), axis=axis, keepdims=True)
        pick = ids == idx
        sel = jnp.where(pick, 1.0, sel)
        g = jnp.where(pick, -jnp.inf, g)
    return sel


def _in_proj_kernel(x_ref, gmix_ref, w_ref, gq_ref, gk_ref,
                    q_ref, k_ref, kb_ref, km_ref, v_ref, vb_ref, u_ref, *rest, tm, with_pool):
    if with_pool:
        d_ref, hb_ref, ext_ref = rest
    else:
        (hb_ref,) = rest
    xf = x_ref[...]
    ms = jnp.mean(xf * xf, axis=-1, keepdims=True)
    hb_ref[...] = (xf * lax.rsqrt(ms + EPS) * gmix_ref[...]).astype(BF16)

    def head_norm(z, out_ref, g_ref):
        for h in range(N_HEADS):
            zh = z[:, h * HEAD_DIM:(h + 1) * HEAD_DIM]
            ms = jnp.mean(zh * zh, axis=-1, keepdims=True)
            out_ref[:, h * HEAD_DIM:(h + 1) * HEAD_DIM] = zh * lax.rsqrt(ms + EPS) * g_ref[...]

    head_norm(_dot(hb_ref[...], w_ref[:, 0:ATTN_WIDTH]), q_ref, gq_ref)
    head_norm(_dot(hb_ref[...], w_ref[:, ATTN_WIDTH:2 * ATTN_WIDTH]), k_ref, gk_ref)
    kb_ref[...] = k_ref[...].astype(BF16)
    for g in range(tm // MOBA_BLOCK):
        blk = k_ref[g * MOBA_BLOCK:(g + 1) * MOBA_BLOCK, :]
        km_ref[g] = jnp.mean(blk, axis=0, keepdims=True)
    zv = _dot(hb_ref[...], w_ref[:, 2 * ATTN_WIDTH:3 * ATTN_WIDTH])
    v_ref[...] = zv
    vb_ref[...] = zv.astype(BF16)
    zu = _dot(hb_ref[...], w_ref[:, 3 * ATTN_WIDTH:])
    u_ref[...] = zu
    if with_pool:
        _pool_differences(zu, d_ref, ext_ref, tm=tm)


POOL_PAD = 16


def _pool_differences(u, d_ref, ext_ref, *, tm):
    i = pl.program_id(0)

    @pl.when(i == 0)
    def _():
        ext_ref[0:POOL_PAD, :] = jnp.zeros((POOL_PAD, POOL_WIDTH), F32)

    ext_ref[POOL_PAD:POOL_PAD + tm, :] = u
    pos = (i * tm + lax.broadcasted_iota(jnp.int32, (tm, 1), 0)).astype(F32)
    for g, w in enumerate(POOL_WINDOWS):
        c0, c1 = g * POOL_GROUP_W, (g + 1) * POOL_GROUP_W
        wsum = u[:, c0:c1]
        for jj in range(1, w):
            wsum = wsum + ext_ref[POOL_PAD - jj:POOL_PAD - jj + tm, c0:c1]
        inv = 1.0 / jnp.minimum(jnp.float32(w), pos + 1.0)
        d_ref[:, c0:c1] = (wsum * inv - u[:, c0:c1]).astype(d_ref.dtype)
    ext_ref[0:POOL_PAD, :] = ext_ref[tm:tm + POOL_PAD, :]


def in_proj(x, g_mix, w_in_bf, g_q, g_k, *, tm, with_pool):
    n = x.shape[0]
    wide = lambda i: (i, 0)
    const = lambda i: (0, 0)
    out_shape = (
        jax.ShapeDtypeStruct((n, ATTN_WIDTH), F32),
        jax.ShapeDtypeStruct((n, ATTN_WIDTH), F32),
        jax.ShapeDtypeStruct((n, ATTN_WIDTH), BF16),
        jax.ShapeDtypeStruct((n // MOBA_BLOCK, 1, ATTN_WIDTH), F32),
        jax.ShapeDtypeStruct((n, ATTN_WIDTH), F32),
        jax.ShapeDtypeStruct((n, ATTN_WIDTH), BF16),
        jax.ShapeDtypeStruct((n, POOL_WIDTH), F32),
    )
    blk = pl.BlockSpec((tm, ATTN_WIDTH), wide)
    pool_out = (jax.ShapeDtypeStruct((n, POOL_WIDTH), BF16),) if with_pool else ()
    pool_scratch = [pltpu.VMEM((POOL_PAD + tm, POOL_WIDTH), F32)] if with_pool else []
    return pl.pallas_call(
        functools.partial(_in_proj_kernel, tm=tm, with_pool=with_pool),
        grid=(n // tm,),
        in_specs=[
            pl.BlockSpec((tm, D_MODEL), wide),
            pl.BlockSpec((1, D_MODEL), const),
            pl.BlockSpec(w_in_bf.shape, const, pipeline_mode=pl.Buffered(1)),
            pl.BlockSpec((1, HEAD_DIM), const),
            pl.BlockSpec((1, HEAD_DIM), const),
        ],
        out_specs=(blk, blk, blk,
                   pl.BlockSpec((tm // MOBA_BLOCK, 1, ATTN_WIDTH), lambda i: (i, 0, 0)),
                   blk, blk, blk) + (blk,) * len(pool_out),
        out_shape=out_shape + pool_out,
        scratch_shapes=[pltpu.VMEM((tm, D_MODEL), BF16)] + pool_scratch,
        compiler_params=_cparams(("arbitrary",)),
        name="in_proj",
    )(x, g_mix, w_in_bf, g_q, g_k)


LOG2E = 1.4426950408889634
ATT_HEADS = 2
ATT_GROUP = 4


def _moba_prompt_prologue(own, q_ref, k_ref, vt_ref, km_ref, qs_ref, bias_ref, m_ref, l_ref, acc_ref, *, nb, hp):
    base = pl.multiple_of(own * MOBA_BLOCK, MOBA_BLOCK)
    blk_id = lax.broadcasted_iota(jnp.int32, (nb, MOBA_BLOCK), 0).astype(F32)
    valid = blk_id < own.astype(F32)
    key_i = lax.broadcasted_iota(jnp.int32, (MOBA_BLOCK, MOBA_BLOCK), 0)
    qry_i = lax.broadcasted_iota(jnp.int32, (MOBA_BLOCK, MOBA_BLOCK), 1)

    for hh in range(hp):
        cols = slice(hh * HEAD_DIM, (hh + 1) * HEAD_DIM)
        q = q_ref[:, cols]
        gate = _dot_nt(km_ref[:, cols], q, precision=lax.Precision.HIGHEST)
        sel = _top_mask(jnp.where(valid, gate, NEG), blk_id, MOBA_TOPK, axis=0)
        bias = jnp.where(jnp.logical_and(sel > 0.5, valid), 0.0, NEG)
        for n in range(nb):
            bias_ref[hh, n] = bias[n:n + 1, :]
        qs = (q * (HEAD_DIM ** -0.5 * LOG2E)).astype(BF16)
        qs_ref[hh] = qs
        s = _dot_nt(k_ref[pl.ds(base, MOBA_BLOCK), cols], qs)
        s = jnp.where(key_i <= qry_i, s, NEG)
        m0 = jnp.max(s, axis=0, keepdims=True)
        p = jnp.exp2(s - m0)
        m_ref[hh] = m0
        l_ref[hh] = jnp.sum(p, axis=0, keepdims=True)
        acc_ref[hh] = _dot(vt_ref[own, cols, :], p.astype(BF16))


def _moba_prompt_main(own, k_ref, vt_ref, o_ref, qs_ref, bias_ref, m_ref, l_ref, acc_ref, s_ref, mb_ref,
                      *, nb, hp, grp):
    n_iter = (own + grp - 1) // grp

    def scores(i, slot):
        for hh in range(hp):
            cols = slice(hh * HEAD_DIM, (hh + 1) * HEAD_DIM)
            qs = qs_ref[hh]
            mb = None
            for gg in range(grp):
                n = jnp.minimum(i * grp + gg, nb - 1)
                off = pl.multiple_of(n * MOBA_BLOCK, MOBA_BLOCK)
                s = _dot_nt(k_ref[pl.ds(off, MOBA_BLOCK), cols], qs) + bias_ref[hh, n]
                s_ref[slot, hh, gg] = s
                smax = jnp.max(s, axis=0, keepdims=True)
                mb = smax if mb is None else jnp.maximum(mb, smax)
            mb_ref[slot, hh] = mb

    def softmax_pv(i, slot):
        for hh in range(hp):
            cols = slice(hh * HEAD_DIM, (hh + 1) * HEAD_DIM)
            m_prev = m_ref[hh]
            m_new = jnp.maximum(m_prev, mb_ref[slot, hh])
            alpha = jnp.exp2(m_prev - m_new)
            l_new = alpha * l_ref[hh]
            acc = alpha * acc_ref[hh]
            for gg in range(grp):
                p = jnp.exp2(s_ref[slot, hh, gg] - m_new)
                l_new = l_new + jnp.sum(p, axis=0, keepdims=True)
                acc = acc + _dot(vt_ref[i * grp + gg, cols, :], p.astype(BF16))
            m_ref[hh] = m_new
            l_ref[hh] = l_new
            acc_ref[hh] = acc

    scores(0, 0)

    def body(k, carry):
        scores(2 * k + 1, 1)
        softmax_pv(2 * k, 0)
        scores(2 * k + 2, 0)
        softmax_pv(2 * k + 1, 1)
        return carry

    lax.fori_loop(0, n_iter // 2, body, 0)

    @pl.when(n_iter % 2 == 1)
    def _():
        softmax_pv(n_iter - 1, 0)

    for hh in range(hp):
        o = acc_ref[hh] * (1.0 / l_ref[hh])
        o_ref[:, hh * HEAD_DIM:(hh + 1) * HEAD_DIM] = o.T.astype(o_ref.dtype)


def _prompt_scratch(nb, hp, grp):
    return [
        pltpu.VMEM((hp, MOBA_BLOCK, HEAD_DIM), BF16),
        pltpu.VMEM((hp, nb, 1, MOBA_BLOCK), F32),
        pltpu.VMEM((hp, 1, MOBA_BLOCK), F32),
        pltpu.VMEM((hp, 1, MOBA_BLOCK), F32),
        pltpu.VMEM((hp, HEAD_DIM, MOBA_BLOCK), F32),
        pltpu.VMEM((2, hp, grp, MOBA_BLOCK, MOBA_BLOCK), F32),
        pltpu.VMEM((2, hp, 1, MOBA_BLOCK), F32),
    ]


def _moba_sample_body(q_ref, kn_ref, vn_ref, k_pages, v_pages, o_ref, s_ref, *, t_new, between):
    n_pages = len(k_pages)
    pw = k_pages[0].shape[0]
    page = pw // N_HEADS
    rows = t_new * N_HEADS
    per_blk = MOBA_BLOCK // page
    n_blocks = n_pages // per_blk

    qa = (q_ref[0] * (HEAD_DIM ** -0.5 * LOG2E)).astype(BF16)
    r_i = lax.broadcasted_iota(jnp.int32, (rows, pw), 0)
    c_i = lax.broadcasted_iota(jnp.int32, (rows, pw), 1)
    same_head = (c_i % N_HEADS) == (r_i % N_HEADS)

    gcol = lax.broadcasted_iota(jnp.int32, (rows, LANES), 1).astype(F32)
    gate = jnp.full((rows, LANES), NEG, F32)
    bmax = jnp.full((rows, LANES), NEG, F32)
    for n in range(n_blocks):
        ssum = jnp.zeros((rows, pw), F32)
        smax = jnp.full((rows, pw), NEG, F32)
        for pp in range(per_blk):
            p_i = n * per_blk + pp
            sp = _dot_nt(qa, k_pages[p_i][...].astype(BF16))
            s_ref[:, p_i * pw:(p_i + 1) * pw] = sp
            ssum = ssum + jnp.where(same_head, sp, 0.0)
            smax = jnp.maximum(smax, jnp.where(same_head, sp, NEG))
        tot = jnp.sum(ssum, axis=-1, keepdims=True) * (1.0 / MOBA_BLOCK)
        gate = jnp.where(gcol == float(n), tot, gate)
        bmax = jnp.where(gcol == float(n), jnp.max(smax, axis=-1, keepdims=True), bmax)
    sel = _top_mask(gate, gcol, min(MOBA_TOPK, n_blocks))
    between()

    s_own = _dot_nt(qa, kn_ref[0].astype(BF16))
    orow = lax.broadcasted_iota(jnp.int32, s_own.shape, 0)
    ocol = lax.broadcasted_iota(jnp.int32, s_own.shape, 1)
    own_ok = jnp.logical_and(ocol % N_HEADS == orow % N_HEADS, ocol // N_HEADS <= orow // N_HEADS)
    s_own = jnp.where(own_ok, s_own, NEG)

    m = jnp.maximum(jnp.max(s_own, axis=-1, keepdims=True),
                    jnp.max(jnp.where(sel > 0.5, bmax, NEG), axis=-1, keepdims=True))
    p_own = jnp.exp2(s_own - m)
    l = jnp.sum(p_own, axis=-1, keepdims=True)
    acc = _dot(p_own.astype(BF16), vn_ref[0].astype(BF16))
    lsum = jnp.zeros((rows, pw), F32)
    for p_i in range(n_pages):
        n = p_i // per_blk
        sp = s_ref[:, p_i * pw:(p_i + 1) * pw]
        keep = jnp.logical_and(same_head, sel[:, n:n + 1] > 0.5)
        pp = jnp.where(keep, jnp.exp2(sp - m), 0.0)
        lsum = lsum + pp
        acc = acc + _dot(pp.astype(BF16), v_pages[p_i][...].astype(BF16))
    l = l + jnp.sum(lsum, axis=-1, keepdims=True)
    o_ref[0] = acc * (1.0 / l)


def _moba_kernel(pt_ref, q_ref, k_ref, vt_ref, km_ref, qs_ref, kn_ref, vn_ref, ck_ref, cv_ref, o_ref, os_ref,
                 qsc_ref, bias_ref, m_ref, l_ref, acc_ref, s_ref, mb_ref, sc_ref, kbuf_ref, vbuf_ref, page_sem,
                 *, nb, hp, grp, n_pages, t_new):
    i = pl.program_id(0)
    own = i % nb
    slot = i % 2

    def fetch(row, sl):
        for p in range(n_pages):
            pg = pt_ref[row, p]
            pltpu.make_async_copy(ck_ref.at[pg], kbuf_ref.at[sl, p], page_sem.at[sl]).start()
            pltpu.make_async_copy(cv_ref.at[pg], vbuf_ref.at[sl, p], page_sem.at[sl]).start()

    @pl.when(i == 0)
    def _():
        fetch(0, 0)

    @pl.when(i + 1 < pl.num_programs(0))
    def _():
        fetch(i + 1, 1 - slot)

    for src, buf in ((ck_ref, kbuf_ref), (cv_ref, vbuf_ref)):
        pltpu.make_async_copy(src.at[pl.ds(0, n_pages)], buf.at[slot], page_sem.at[slot]).wait()
    k_pages = [kbuf_ref.at[slot, p] for p in range(n_pages)]
    v_pages = [vbuf_ref.at[slot, p] for p in range(n_pages)]

    def prompt_prologue():
        _moba_prompt_prologue(own, q_ref, k_ref, vt_ref, km_ref, qsc_ref, bias_ref, m_ref, l_ref, acc_ref,
                              nb=nb, hp=hp)

    _moba_sample_body(qs_ref, kn_ref, vn_ref, k_pages, v_pages, os_ref, sc_ref, t_new=t_new,
                      between=prompt_prologue)
    _moba_prompt_main(own, k_ref, vt_ref, o_ref, qsc_ref, bias_ref, m_ref, l_ref, acc_ref, s_ref, mb_ref,
                      nb=nb, hp=hp, grp=grp)


def moba_attention(q, k_bf, v_bf, kmean, q_s, k_new, v_new, cache_k, cache_v, page_table):
    s = q.shape[0]
    nb = s // MOBA_BLOCK
    hp = ATT_HEADS
    grp = ATT_GROUP if nb % ATT_GROUP == 0 else 1
    wide = hp * HEAD_DIM
    vt = v_bf.reshape(nb, MOBA_BLOCK, ATTN_WIDTH).transpose(0, 2, 1)
    db, t_new, _ = q_s.shape
    th = t_new * N_HEADS
    q_s, k_new, v_new = (a.reshape(db, th, HEAD_DIM) for a in (q_s, k_new, v_new))
    assert db == (N_HEADS // hp) * nb, "one sample row per prompt tile step"
    n_pages = page_table.shape[1]
    page = cache_k.shape[-3]
    ck = cache_k.reshape(-1, page * N_HEADS, HEAD_DIM)
    cv = cache_v.reshape(-1, page * N_HEADS, HEAD_DIM)
    tok = pl.BlockSpec((1, th, HEAD_DIM), lambda i, pt: (i, 0, 0))
    tile = pl.BlockSpec((MOBA_BLOCK, wide), lambda i, pt: (i % nb, i // nb))

    page_rows = page * N_HEADS
    grid_spec = pltpu.PrefetchScalarGridSpec(
        num_scalar_prefetch=1,
        grid=(db,),
        in_specs=[
            tile,
            pl.BlockSpec((s, wide), lambda i, pt: (0, i // nb)),
            pl.BlockSpec((nb, wide, MOBA_BLOCK), lambda i, pt: (0, i // nb, 0)),
            pl.BlockSpec((nb, wide), lambda i, pt: (0, i // nb)),
            tok, tok, tok,
            pl.BlockSpec(memory_space=pl.ANY), pl.BlockSpec(memory_space=pl.ANY),
        ],
        out_specs=(tile, tok),
        scratch_shapes=_prompt_scratch(nb, hp, grp) + [
            pltpu.VMEM((th, n_pages * page_rows), F32),
            pltpu.VMEM((2, n_pages, page_rows, HEAD_DIM), F32),
            pltpu.VMEM((2, n_pages, page_rows, HEAD_DIM), F32),
            pltpu.SemaphoreType.DMA((2,)),
        ],
    )
    o_p, o_s = pl.pallas_call(
        functools.partial(_moba_kernel, nb=nb, hp=hp, grp=grp, n_pages=n_pages, t_new=t_new),
        grid_spec=grid_spec,
        out_shape=(jax.ShapeDtypeStruct((s, ATTN_WIDTH), BF16),
                   jax.ShapeDtypeStruct((db, th, HEAD_DIM), F32)),
        compiler_params=_cparams(("arbitrary",), ATTN_VMEM_LIMIT),
        name="moba_attention",
    )(page_table, q, k_bf, vt, kmean, q_s, k_new, v_new, ck, cv)
    return o_p, o_s.reshape(db, t_new, ATTN_WIDTH)


def _pool_sample_kernel(u_ref, st_ref, d_ref, new_ref, *, t_new, start_pos):
    ext = [st_ref[:, r, :] for r in range(POOL_STATE)] + [u_ref[:, t, :] for t in range(t_new)]
    for t in range(t_new):
        e = POOL_STATE + t
        parts = []
        for g, w in enumerate(POOL_WINDOWS):
            c0, c1 = g * POOL_GROUP_W, (g + 1) * POOL_GROUP_W
            wsum = ext[e][:, c0:c1]
            for jj in range(1, w):
                wsum = wsum + ext[e - jj][:, c0:c1]
            count = min(float(w), float(start_pos + t) + 1.0)
            parts.append(wsum * (1.0 / count) - ext[e][:, c0:c1])
        d_ref[:, t, :] = jnp.concatenate(parts, axis=-1).astype(d_ref.dtype)
    for r in range(POOL_STATE):
        new_ref[:, r, :] = ext[t_new + r]


def pool_sample(u, state, *, start_pos, bb=32):
    db, t_new, _ = u.shape
    return pl.pallas_call(
        functools.partial(_pool_sample_kernel, t_new=t_new, start_pos=start_pos),
        grid=(db // bb,),
        in_specs=[
            pl.BlockSpec((bb, t_new, POOL_WIDTH), lambda i: (i, 0, 0)),
            pl.BlockSpec((bb, POOL_STATE, POOL_WIDTH), lambda i: (i, 0, 0)),
        ],
        out_specs=(
            pl.BlockSpec((bb, t_new, POOL_WIDTH), lambda i: (i, 0, 0)),
            pl.BlockSpec((bb, POOL_STATE, POOL_WIDTH), lambda i: (i, 0, 0)),
        ),
        out_shape=(
            jax.ShapeDtypeStruct((db, t_new, POOL_WIDTH), F32),
            jax.ShapeDtypeStruct((db, POOL_STATE, POOL_WIDTH), F32),
        ),
        compiler_params=_cparams(("arbitrary",)),
        name="pool_sample",
    )(u, state)


def _out_proj_kernel(x_ref, oa_ref, d_ref, wp_ref, ps_ref, wo_ref, gf_ref, wr_ref, br_ref,
                     h2_ref, rt_ref, mix_ref):
    mix_ref[:, 0:ATTN_WIDTH] = oa_ref[...].astype(BF16)
    dd = d_ref[...].astype(BF16)
    for g in range(len(POOL_WINDOWS)):
        c0, c1 = g * POOL_GROUP_W, (g + 1) * POOL_GROUP_W
        yg = _dot(dd[:, c0:c1], wp_ref[g].astype(BF16)) * ps_ref[:, c0:c1]
        mix_ref[:, ATTN_WIDTH + c0:ATTN_WIDTH + c1] = yg.astype(BF16)
    h2 = x_ref[...] + _dot(mix_ref[...], wo_ref[...])
    h2_ref[...] = h2
    ms = jnp.mean(h2 * h2, axis=-1, keepdims=True)
    hn = h2 * lax.rsqrt(ms + EPS) * gf_ref[...]

    hn_hi = hn.astype(BF16)
    hn_lo = (hn - hn_hi.astype(F32)).astype(BF16)
    t = _dot(hn_hi, wr_ref[...])
    logits = t[:, :LANES] + t[:, LANES:] + _dot(hn_lo, wr_ref[:, :LANES]) + br_ref[...]
    lane = lax.broadcasted_iota(jnp.int32, logits.shape, 1).astype(F32)
    far = jnp.float32(LANES)
    is_g = lane < N_EXPERT_GROUPS
    gl = jnp.where(is_g, logits, -jnp.inf)
    g_max = jnp.max(gl, axis=-1, keepdims=True)
    g_top = jnp.min(jnp.where(gl == g_max, lane, far), axis=-1, keepdims=True)
    g_p = 1.0 / jnp.sum(jnp.where(is_g, jnp.exp(gl - g_max), 0.0), axis=-1, keepdims=True)
    lo = N_EXPERT_GROUPS + g_top * EXPERTS_PER_GROUP
    in_grp = jnp.logical_and(lane >= lo, lane < lo + EXPERTS_PER_GROUP)
    el = jnp.where(in_grp, logits, -jnp.inf)
    e1 = jnp.max(el, axis=-1, keepdims=True)
    i1 = jnp.min(jnp.where(el == e1, lane, far), axis=-1, keepdims=True)
    el2 = jnp.where(lane == i1, -jnp.inf, el)
    e2 = jnp.max(el2, axis=-1, keepdims=True)
    i2 = jnp.min(jnp.where(el2 == e2, lane, far), axis=-1, keepdims=True)
    ex2 = jnp.exp(e2 - e1)
    den = 1.0 + ex2
    w1 = (1.0 / den) * g_p
    w2 = (ex2 / den) * g_p
    rt_ref[...] = jnp.where(lane == 0.0, i1 - N_EXPERT_GROUPS,
                            jnp.where(lane == 1.0, i2 - N_EXPERT_GROUPS,
                                      jnp.where(lane == 2.0, w1, jnp.where(lane == 3.0, w2, 0.0))))


def out_proj(x, o_attn, d, w_pool, pool_scale, w_out_bf, g_ffn, w_router, b_router, *, tm):
    n = x.shape[0]
    row = lambda i: (i, 0)
    const = lambda i: (0, 0)
    return pl.pallas_call(
        _out_proj_kernel,
        grid=(n // tm,),
        in_specs=[
            pl.BlockSpec((tm, D_MODEL), row),
            pl.BlockSpec((tm, ATTN_WIDTH), row),
            pl.BlockSpec((tm, POOL_WIDTH), row),
            pl.BlockSpec(w_pool.shape, lambda i: (0, 0, 0)),
            pl.BlockSpec((1, POOL_WIDTH), const),
            pl.BlockSpec((D_MODEL, D_MODEL), const),
            pl.BlockSpec((1, D_MODEL), const),
            pl.BlockSpec((D_MODEL, 2 * LANES), const),
            pl.BlockSpec((1, LANES), const),
        ],
        out_specs=(
            pl.BlockSpec((tm, D_MODEL), row),
            pl.BlockSpec((tm, LANES), row),
        ),
        out_shape=(
            jax.ShapeDtypeStruct((n, D_MODEL), F32),
            jax.ShapeDtypeStruct((n, LANES), F32),
        ),
        scratch_shapes=[pltpu.VMEM((tm, D_MODEL), BF16)],
        compiler_params=_cparams(("arbitrary",)),
        name="out_proj",
    )(x, o_attn, d, w_pool, pool_scale, w_out_bf, g_ffn, w_router, b_router)


MOE_TILE = 256
ROW_TILE = 256


def _route_kernel(rt_ref, pos_ref, meta_ref, *, n_tiles):
    lane = lax.broadcasted_iota(jnp.int32, (ROW_TILE, LANES), 1).astype(F32)
    r_i = lax.broadcasted_iota(jnp.int32, (ROW_TILE, ROW_TILE), 0)
    c_i = lax.broadcasted_iota(jnp.int32, (ROW_TILE, ROW_TILE), 1)
    tri = jnp.where(c_i < r_i, 1.0, 0.0).astype(BF16)

    def one_hot(t):
        rt = rt_ref[pl.ds(pl.multiple_of(t * ROW_TILE, ROW_TILE), ROW_TILE), :]
        e1, e2 = rt[:, 0:1], rt[:, 1:2]
        return e1, e2, jnp.where(jnp.logical_or(lane == e1, lane == e2), 1.0, 0.0)

    def count(t, cnt):
        return cnt + jnp.sum(one_hot(t)[2], axis=0, keepdims=True)

    cnt = lax.fori_loop(0, n_tiles, count, jnp.zeros((1, LANES), F32))
    tiles_per = jnp.floor((cnt + (MOE_TILE - 1)) * (1.0 / MOE_TILE))
    e_r = lax.broadcasted_iota(jnp.int32, (LANES, LANES), 0)
    e_c = lax.broadcasted_iota(jnp.int32, (LANES, LANES), 1)
    upper = jnp.where(e_r < e_c, 1.0, 0.0).astype(BF16)
    off_tiles = _dot(jnp.broadcast_to(tiles_per, (8, LANES)).astype(BF16), upper)[0:1]
    base = off_tiles * MOE_TILE

    def place(t, run):
        e1, e2, oh = one_hot(t)
        dest = base + run + _dot(tri, oh.astype(BF16))
        p1 = jnp.sum(jnp.where(lane == e1, dest, 0.0), axis=1, keepdims=True)
        p2 = jnp.sum(jnp.where(lane == e2, dest, 0.0), axis=1, keepdims=True)
        pos = jnp.where(lane == 0.0, p1, jnp.where(lane == 1.0, p2, 0.0))
        pos_ref[pl.ds(pl.multiple_of(t * ROW_TILE, ROW_TILE), ROW_TILE), :] = pos.astype(jnp.int32)
        return run + jnp.sum(oh, axis=0, keepdims=True)

    lax.fori_loop(0, n_tiles, place, jnp.zeros((1, LANES), F32))
    row = lax.broadcasted_iota(jnp.int32, (8, LANES), 0)
    meta_ref[...] = jnp.where(row == 0, tiles_per, jnp.where(row == 1, off_tiles, 0.0))


def route(rt):
    n = rt.shape[0]
    assert n % ROW_TILE == 0
    return pl.pallas_call(
        functools.partial(_route_kernel, n_tiles=n // ROW_TILE),
        out_shape=(
            jax.ShapeDtypeStruct((n, LANES), jnp.int32),
            jax.ShapeDtypeStruct((8, LANES), F32),
        ),
        compiler_params=pltpu.CompilerParams(vmem_limit_bytes=VMEM_LIMIT),
        name="route",
    )(rt)


ROW_UNROLL = 8


def _dispatch_kernel(p1_ref, p2_ref, zt_ref, xa_ref, xb_ref, xs_ref, zero_ref, sem, zsem, *, tm, tiles_a, n_zero):
    i = pl.program_id(0)

    @pl.when(i == 0)
    def _():
        zero_ref[...] = jnp.zeros(zero_ref.shape, zero_ref.dtype)

        def zcopy(z):
            row0 = pl.multiple_of(zt_ref[z] * MOE_TILE, MOE_TILE)
            return pltpu.make_async_copy(zero_ref, xs_ref.at[pl.ds(row0, MOE_TILE)], zsem)

        for z in range(n_zero):
            @pl.when(zt_ref[z] >= 0)
            def _():
                zcopy(z).start()
        for z in range(n_zero):
            @pl.when(zt_ref[z] >= 0)
            def _():
                zcopy(z).wait()

    def scatter(x_ref):
        t0 = i * tm

        def issue(r, c):
            src = x_ref.at[pl.ds(r, 1)]
            pltpu.make_async_copy(src, xs_ref.at[pl.ds(p1_ref[t0 + r], 1)], sem).start()
            pltpu.make_async_copy(src, xs_ref.at[pl.ds(p2_ref[t0 + r], 1)], sem).start()
            return c

        lax.fori_loop(0, tm, issue, 0, unroll=ROW_UNROLL)
        for _ in range(2):
            pltpu.make_async_copy(x_ref, xs_ref.at[pl.ds(0, tm)], sem).wait()

    @pl.when(i < tiles_a)
    def _():
        scatter(xa_ref)

    @pl.when(i >= tiles_a)
    def _():
        scatter(xb_ref)


def dispatch(p1, p2, zero_tiles, xa, xb, rows, *, tm):
    tiles_a, tiles_b = xa.shape[0] // tm, xb.shape[0] // tm
    grid_spec = pltpu.PrefetchScalarGridSpec(
        num_scalar_prefetch=3,
        grid=(tiles_a + tiles_b,),
        in_specs=[
            pl.BlockSpec((tm, D_MODEL), lambda i, a, b, z: (jnp.minimum(i, tiles_a - 1), 0)),
            pl.BlockSpec((tm, D_MODEL), lambda i, a, b, z: (jnp.maximum(i - tiles_a, 0), 0)),
        ],
        out_specs=pl.BlockSpec(memory_space=pl.ANY),
        scratch_shapes=[pltpu.VMEM((MOE_TILE, D_MODEL), F32), pltpu.SemaphoreType.DMA(()), pltpu.SemaphoreType.DMA(())],
    )
    return pl.pallas_call(
        functools.partial(_dispatch_kernel, tm=tm, tiles_a=tiles_a, n_zero=zero_tiles.shape[0]),
        grid_spec=grid_spec,
        out_shape=jax.ShapeDtypeStruct((rows, D_MODEL), F32),
        compiler_params=_cparams(("arbitrary",)),
        name="dispatch",
    )(p1, p2, zero_tiles, xa, xb)


def _experts_kernel(te_ref, ts_ref, tf_ref, sl_ref, nx_ref, nu_ref, x_ref, gf_ref, wg_hbm, wu_hbm, wd_hbm, o_ref,
                    wg_buf, wu_buf, wd_buf, wsem, *, layer):
    j = pl.program_id(0)
    used = j < nu_ref[0]
    first = jnp.logical_and(used, tf_ref[j] == 1)
    slot = sl_ref[j]

    def copies(e, sl):
        return (pltpu.make_async_copy(wg_hbm.at[layer, e], wg_buf.at[sl], wsem.at[sl]),
                pltpu.make_async_copy(wu_hbm.at[layer, e], wu_buf.at[sl], wsem.at[sl]),
                pltpu.make_async_copy(wd_hbm.at[layer, e], wd_buf.at[sl], wsem.at[sl]))

    @pl.when(j == 0)
    def _():
        for c in copies(te_ref[0], 0):
            c.start()

    @pl.when(jnp.logical_and(first, nx_ref[j] >= 0))
    def _():
        for c in copies(nx_ref[j], 1 - slot):
            c.start()

    @pl.when(first)
    def _():
        for c in copies(te_ref[j], slot):
            c.wait()

    @pl.when(used)
    def _():
        h = x_ref[...]
        ms = jnp.mean(h * h, axis=-1, keepdims=True)
        x = (h * lax.rsqrt(ms + EPS) * gf_ref[...]).astype(BF16).astype(F32)
        a = _dot(x, wg_buf[slot])
        b = _dot(x, wu_buf[slot])
        act = (a * (1.0 / (1.0 + jnp.exp(-a)))) * b
        o_ref[...] = _dot(act.astype(BF16).astype(F32), wd_buf[slot])

    @pl.when(jnp.logical_not(used))
    def _():
        o_ref[...] = jnp.zeros(o_ref.shape, o_ref.dtype)


def experts(tile_expert, tile_src, tile_first, tile_slot, tile_next, n_used, xs, g_ffn, w_gate, w_up, w_down, *, layer):
    rows, w = xs.shape
    idx = lambda f: (lambda j, te, ts, tf, sl, nx, nu: f(j, ts))
    grid_spec = pltpu.PrefetchScalarGridSpec(
        num_scalar_prefetch=6,
        grid=(rows // MOE_TILE,),
        in_specs=[
            pl.BlockSpec((MOE_TILE, w), idx(lambda j, ts: (ts[j], 0))),
            pl.BlockSpec((1, D_MODEL), idx(lambda j, ts: (0, 0))),
            pl.BlockSpec(memory_space=pl.ANY), pl.BlockSpec(memory_space=pl.ANY), pl.BlockSpec(memory_space=pl.ANY),
        ],
        out_specs=pl.BlockSpec((MOE_TILE, D_MODEL), idx(lambda j, ts: (j, 0))),
        scratch_shapes=[
            pltpu.VMEM((2, D_MODEL, D_EXPERT), F32), pltpu.VMEM((2, D_MODEL, D_EXPERT), F32),
            pltpu.VMEM((2, D_EXPERT, D_MODEL), F32), pltpu.SemaphoreType.DMA((2,)),
        ],
    )
    return pl.pallas_call(
        functools.partial(_experts_kernel, layer=layer),
        grid_spec=grid_spec,
        out_shape=jax.ShapeDtypeStruct((rows, D_MODEL), F32),
        compiler_params=_cparams(("arbitrary",), EXPERT_VMEM_LIMIT),
        name="experts",
    )(tile_expert, tile_src, tile_first, tile_slot, tile_next, n_used, xs, g_ffn, w_gate, w_up, w_down)


def _combine_kernel(p1_ref, p2_ref, h2_ref, rt_ref, os_ref, y_ref, buf_ref, sem, *, tm):
    i = pl.program_id(0)
    slot = i % 2

    def gather(step, sl):
        t0 = step * tm

        def issue(r, c):
            pltpu.make_async_copy(os_ref.at[pl.ds(p1_ref[t0 + r], 1)], buf_ref.at[sl, 0, pl.ds(r, 1)],
                                  sem.at[sl]).start()
            pltpu.make_async_copy(os_ref.at[pl.ds(p2_ref[t0 + r], 1)], buf_ref.at[sl, 1, pl.ds(r, 1)],
                                  sem.at[sl]).start()
            return c

        lax.fori_loop(0, tm, issue, 0, unroll=ROW_UNROLL)

    @pl.when(i == 0)
    def _():
        gather(0, 0)

    @pl.when(i + 1 < pl.num_programs(0))
    def _():
        gather(i + 1, 1 - slot)

    for s in range(2):
        pltpu.make_async_copy(os_ref.at[pl.ds(0, tm)], buf_ref.at[slot, s], sem.at[slot]).wait()
    rt = rt_ref[...]
    y_ref[...] = h2_ref[...] + rt[:, 2:3] * buf_ref[slot, 0] + rt[:, 3:4] * buf_ref[slot, 1]


def combine(p1, p2, h2, rt, os, *, tm):
    n = h2.shape[0]
    row = lambda i, a, b: (i, 0)
    grid_spec = pltpu.PrefetchScalarGridSpec(
        num_scalar_prefetch=2,
        grid=(n // tm,),
        in_specs=[pl.BlockSpec((tm, D_MODEL), row), pl.BlockSpec((tm, LANES), row),
                  pl.BlockSpec(memory_space=pl.ANY)],
        out_specs=pl.BlockSpec((tm, D_MODEL), row),
        scratch_shapes=[pltpu.VMEM((2, 2, tm, D_MODEL), F32), pltpu.SemaphoreType.DMA((2,))],
    )
    return pl.pallas_call(
        functools.partial(_combine_kernel, tm=tm),
        grid_spec=grid_spec,
        out_shape=jax.ShapeDtypeStruct((n, D_MODEL), F32),
        compiler_params=_cparams(("arbitrary",)),
        name="combine",
    )(p1, p2, h2, rt, os)


def kernel(x_prompt, x_sample, cache_k, cache_v, state_pool, page_table, g_mix, w_in, g_q, g_k, w_pool, pool_scale, w_out, g_ffn, w_group_router, b_group_router, w_expert_router, b_expert_router, w_gate, w_up, w_down):
    B, S, _ = x_prompt.shape
    DB, T, _ = x_sample.shape
    depth = w_in.shape[0]
    assert B == 1 and depth == 1
    past_len = page_table.shape[1] * cache_k.shape[2]
    l = 0

    w_in_bf = w_in[l].astype(BF16)
    w_out_bf = w_out[l].astype(BF16)
    gm, gq, gk, gf = g_mix[l][None], g_q[l][None], g_k[l][None], g_ffn[l][None]
    ps = pool_scale[l][None]
    n_r = N_EXPERT_GROUPS + N_EXPERTS
    w_router = jnp.concatenate([w_group_router[l], w_expert_router[l].reshape(D_MODEL, N_EXPERTS)], axis=1)
    w_router = jnp.pad(w_router, ((0, 0), (0, LANES - n_r)))
    w_router_hi = w_router.astype(BF16)
    w_router = jnp.concatenate([w_router_hi, (w_router - w_router_hi.astype(F32)).astype(BF16)], axis=1)
    b_router = jnp.concatenate([b_group_router[l], b_expert_router[l].reshape(N_EXPERTS)])
    b_router = jnp.pad(b_router, (0, LANES - n_r))[None]

    def mixer_tail(x2d, o_attn, d):
        return out_proj(x2d, o_attn, d, w_pool[l], ps, w_out_bf, gf, w_router, b_router, tm=PROJ_TILE)

    xp = x_prompt.reshape(S, D_MODEL)
    n_s = DB * T
    xs = x_sample.reshape(n_s, D_MODEL)
    q_p, k_p, kb_p, km_p, v_p, vb_p, u_p, d_p = in_proj(xp, gm, w_in_bf, gq, gk, tm=PROJ_TILE, with_pool=True)
    q_s, k_s, _, _, v_s, _, u_s = in_proj(xs, gm, w_in_bf, gq, gk, tm=n_s, with_pool=False)
    r3 = lambda a: a.reshape(DB, T, ATTN_WIDTH)
    o_p, o_s = moba_attention(q_p, kb_p, vb_p, km_p.reshape(S // MOBA_BLOCK, ATTN_WIDTH),
                              r3(q_s), r3(k_s), r3(v_s), cache_k, cache_v, page_table + l * cache_k.shape[1])
    d_s, pool_s = pool_sample(u_s.reshape(DB, T, POOL_WIDTH), state_pool[l], start_pos=past_len)
    h2_p, rt_p = mixer_tail(xp, o_p, d_p)
    h2_s, rt_s = mixer_tail(xs, o_s.reshape(n_s, ATTN_WIDTH), d_s.reshape(n_s, POOL_WIDTH))

    pos, meta = route(jnp.concatenate([rt_p, rt_s], axis=0))
    p1, p2 = pos[:, 0], pos[:, 1]
    max_tiles = -(-2 * (S + n_s) // MOE_TILE) + N_EXPERTS
    tiles_per = meta[0, :N_EXPERTS].astype(jnp.int32)
    ends = tiles_per + meta[1, :N_EXPERTS].astype(jnp.int32)
    n_used = ends[N_EXPERTS - 1]
    tile_src = jnp.minimum(jnp.arange(max_tiles, dtype=jnp.int32), n_used - 1)
    tile_expert = jnp.minimum(jnp.sum(tile_src[:, None] >= ends[None, :], axis=1), N_EXPERTS - 1).astype(jnp.int32)
    tail = n_used + jnp.arange(N_EXPERTS, dtype=jnp.int32)
    zero_tiles = jnp.concatenate([jnp.where(tiles_per > 0, ends - 1, -1), jnp.where(tail < max_tiles, tail, -1)])
    x_sorted = dispatch(p1, p2, zero_tiles, h2_p, h2_s, max_tiles * MOE_TILE, tm=ROW_TILE)
    tile_ids = jnp.arange(max_tiles, dtype=jnp.int32)
    tile_first = jnp.logical_and(tile_ids < n_used,
                                 jnp.logical_or(tile_ids == 0, tile_expert != jnp.roll(tile_expert, 1))).astype(jnp.int32)
    tile_slot = ((jnp.cumsum(tile_first) - 1) % 2).astype(jnp.int32)
    e_ids = jnp.arange(N_EXPERTS, dtype=jnp.int32)
    later = jnp.logical_and(e_ids[None, :] > e_ids[:, None], tiles_per[None, :] > 0)
    next_nonempty = jnp.min(jnp.where(later, e_ids[None, :], N_EXPERTS), axis=1)
    tile_next = jnp.where(next_nonempty < N_EXPERTS, next_nonempty, -1)[tile_expert].astype(jnp.int32)
    o_sorted = experts(tile_expert, tile_src, tile_first, tile_slot, tile_next, n_used[None], x_sorted, gf,
                       w_gate, w_up, w_down, layer=l)
    y_prompt = combine(p1[:S], p2[:S], h2_p, rt_p, o_sorted, tm=ROW_TILE).reshape(B, S, D_MODEL)
    y_sample = combine(p1[S:], p2[S:], h2_s, rt_s, o_sorted, tm=ROW_TILE).reshape(DB, T, D_MODEL)

    hd = (N_HEADS, HEAD_DIM)
    return (
        y_prompt,
        y_sample,
        k_p.reshape(1, B, S, *hd),
        v_p.reshape(1, B, S, *hd),
        u_p[S - POOL_STATE:].reshape(1, B, POOL_STATE, POOL_WIDTH),
        k_s.reshape(1, DB, T, *hd),
        v_s.reshape(1, DB, T, *hd),
        pool_s.reshape(1, DB, POOL_STATE, POOL_WIDTH),
    )
```

```python
import functools

import jax
import jax.numpy as jnp
from jax import lax
from jax.experimental import pallas as pl
from jax.experimental.pallas import tpu as pltpu

D_MODEL = 2048
ATTN_WIDTH = 1024
POOL_WIDTH = 1024
HEAD_DIM = 128
N_HEADS = 8
POOL_WINDOWS = (2, 4, 8, 16)
POOL_GROUP_W = 256
POOL_STATE = 15
MOBA_BLOCK = 256
MOBA_TOPK = 3
N_EXPERT_GROUPS = 4
EXPERTS_PER_GROUP = 4
N_EXPERTS = 16
D_EXPERT = 768
EPS = 1e-6
NEG = -1e30
LANES = 128
VMEM_LIMIT = 56 * 1024 * 1024
PROJ_TILE = 512
ATTN_VMEM_LIMIT = 60 * 1024 * 1024
EXPERT_VMEM_LIMIT = 60 * 1024 * 1024

BF16 = jnp.bfloat16
F32 = jnp.float32


def _cparams(sem, vmem_limit=VMEM_LIMIT):
    return pltpu.CompilerParams(dimension_semantics=sem, vmem_limit_bytes=vmem_limit)


def _dot(a, b):
    return jnp.dot(a, b, preferred_element_type=F32)


def _dot_nt(a, b, precision=None):
    return lax.dot_general(a, b, (((1,), (1,)), ((), ())), precision=precision,
                           preferred_element_type=F32)


def _top_mask(g, ids, k, axis=-1):
    sel = jnp.zeros(g.shape, F32)
    for _ in range(k):
        m = jnp.max(g, axis=axis, keepdims=True)
        idx = jnp.min(jnp.where(g == m, ids, jnp.float32(g.shape[axis])), axis=axis, keepdims=True)
        pick = ids == idx
        sel = jnp.where(pick, 1.0, sel)
        g = jnp.where(pick, -jnp.inf, g)
    return sel


def _in_proj_kernel(x_ref, gmix_ref, w_ref, gq_ref, gk_ref,
                    q_ref, k_ref, kb_ref, km_ref, v_ref, vb_ref, u_ref, *rest, tm, with_pool):
    if with_pool:
        d_ref, hb_ref, ext_ref = rest
    else:
        (hb_ref,) = rest
    xf = x_ref[...]
    ms = jnp.mean(xf * xf, axis=-1, keepdims=True)
    hb_ref[...] = (xf * lax.rsqrt(ms + EPS) * gmix_ref[...]).astype(BF16)

    def head_norm(z, out_ref, g_ref):
        for h in range(N_HEADS):
            zh = z[:, h * HEAD_DIM:(h + 1) * HEAD_DIM]
            ms = jnp.mean(zh * zh, axis=-1, keepdims=True)
            out_ref[:, h * HEAD_DIM:(h + 1) * HEAD_DIM] = zh * lax.rsqrt(ms + EPS) * g_ref[...]

    head_norm(_dot(hb_ref[...], w_ref[:, 0:ATTN_WIDTH]), q_ref, gq_ref)
    head_norm(_dot(hb_ref[...], w_ref[:, ATTN_WIDTH:2 * ATTN_WIDTH]), k_ref, gk_ref)
    kb_ref[...] = k_ref[...].astype(BF16)
    for g in range(tm // MOBA_BLOCK):
        blk = k_ref[g * MOBA_BLOCK:(g + 1) * MOBA_BLOCK, :]
        km_ref[g] = jnp.mean(blk, axis=0, keepdims=True)
    zv = _dot(hb_ref[...], w_ref[:, 2 * ATTN_WIDTH:3 * ATTN_WIDTH])
    v_ref[...] = zv
    vb_ref[...] = zv.astype(BF16)
    zu = _dot(hb_ref[...], w_ref[:, 3 * ATTN_WIDTH:])
    u_ref[...] = zu
    if with_pool:
        _pool_differences(zu, d_ref, ext_ref, tm=tm)


POOL_PAD = 16


def _pool_differences(u, d_ref, ext_ref, *, tm):
    i = pl.program_id(0)

    @pl.when(i == 0)
    def _():
        ext_ref[0:POOL_PAD, :] = jnp.zeros((POOL_PAD, POOL_WIDTH), F32)

    ext_ref[POOL_PAD:POOL_PAD + tm, :] = u
    pos = (i * tm + lax.broadcasted_iota(jnp.int32, (tm, 1), 0)).astype(F32)
    for g, w in enumerate(POOL_WINDOWS):
        c0, c1 = g * POOL_GROUP_W, (g + 1) * POOL_GROUP_W
        wsum = u[:, c0:c1]
        for jj in range(1, w):
            wsum = wsum + ext_ref[POOL_PAD - jj:POOL_PAD - jj + tm, c0:c1]
        inv = 1.0 / jnp.minimum(jnp.float32(w), pos + 1.0)
        d_ref[:, c0:c1] = (wsum * inv - u[:, c0:c1]).astype(d_ref.dtype)
    ext_ref[0:POOL_PAD, :] = ext_ref[tm:tm + POOL_PAD, :]


def in_proj(x, g_mix, w_in_bf, g_q, g_k, *, tm, with_pool):
    n = x.shape[0]
    wide = lambda i: (i, 0)
    const = lambda i: (0, 0)
    out_shape = (
        jax.ShapeDtypeStruct((n, ATTN_WIDTH), F32),
        jax.ShapeDtypeStruct((n, ATTN_WIDTH), F32),
        jax.ShapeDtypeStruct((n, ATTN_WIDTH), BF16),
        jax.ShapeDtypeStruct((n // MOBA_BLOCK, 1, ATTN_WIDTH), F32),
        jax.ShapeDtypeStruct((n, ATTN_WIDTH), F32),
        jax.ShapeDtypeStruct((n, ATTN_WIDTH), BF16),
        jax.ShapeDtypeStruct((n, POOL_WIDTH), F32),
    )
    blk = pl.BlockSpec((tm, ATTN_WIDTH), wide)
    pool_out = (jax.ShapeDtypeStruct((n, POOL_WIDTH), BF16),) if with_pool else ()
    pool_scratch = [pltpu.VMEM((POOL_PAD + tm, POOL_WIDTH), F32)] if with_pool else []
    return pl.pallas_call(
        functools.partial(_in_proj_kernel, tm=tm, with_pool=with_pool),
        grid=(n // tm,),
        in_specs=[
            pl.BlockSpec((tm, D_MODEL), wide),
            pl.BlockSpec((1, D_MODEL), const),
            pl.BlockSpec(w_in_bf.shape, const, pipeline_mode=pl.Buffered(1)),
            pl.BlockSpec((1, HEAD_DIM), const),
            pl.BlockSpec((1, HEAD_DIM), const),
        ],
        out_specs=(blk, blk, blk,
                   pl.BlockSpec((tm // MOBA_BLOCK, 1, ATTN_WIDTH), lambda i: (i, 0, 0)),
                   blk, blk, blk) + (blk,) * len(pool_out),
        out_shape=out_shape + pool_out,
        scratch_shapes=[pltpu.VMEM((tm, D_MODEL), BF16)] + pool_scratch,
        compiler_params=_cparams(("arbitrary",)),
        name="in_proj",
    )(x, g_mix, w_in_bf, g_q, g_k)


LOG2E = 1.4426950408889634
ATT_HEADS = 2
ATT_GROUP = 4
VT_ROWS = HEAD_DIM + 16


def _moba_prompt_prologue(own, q_ref, k_ref, vt_ref, km_ref, qs_ref, bias_ref, m_ref, acc_ref, *, nb, hp):
    base = pl.multiple_of(own * MOBA_BLOCK, MOBA_BLOCK)
    blk_id = lax.broadcasted_iota(jnp.int32, (nb, MOBA_BLOCK), 0).astype(F32)
    valid = blk_id < own.astype(F32)
    key_i = lax.broadcasted_iota(jnp.int32, (MOBA_BLOCK, MOBA_BLOCK), 0)
    qry_i = lax.broadcasted_iota(jnp.int32, (MOBA_BLOCK, MOBA_BLOCK), 1)

    for hh in range(hp):
        cols = slice(hh * HEAD_DIM, (hh + 1) * HEAD_DIM)
        q = q_ref[:, cols]
        gate = _dot_nt(km_ref[:, cols], q, precision=lax.Precision.HIGHEST)
        sel = _top_mask(jnp.where(valid, gate, NEG), blk_id, MOBA_TOPK, axis=0)
        bias = jnp.where(jnp.logical_and(sel > 0.5, valid), 0.0, NEG)
        for n in range(nb):
            bias_ref[hh, n] = bias[n:n + 1, :]
        qs = (q * (HEAD_DIM ** -0.5 * LOG2E)).astype(BF16)
        qs_ref[hh] = qs
        s = _dot_nt(k_ref[pl.ds(base, MOBA_BLOCK), cols], qs)
        s = jnp.where(key_i <= qry_i, s, NEG)
        m0 = jnp.max(s, axis=0, keepdims=True)
        p = jnp.exp2(s - m0)
        m_ref[hh] = m0
        acc_ref[hh] = _dot(vt_ref[own, hh * VT_ROWS:(hh + 1) * VT_ROWS, :], p.astype(BF16))


def _moba_prompt_main(own, k_ref, vt_ref, o_ref, qs_ref, bias_ref, m_ref, acc_ref, s_ref, mb_ref,
                      *, nb, hp, grp):
    n_iter = (own + grp - 1) // grp

    def scores(i, slot):
        for hh in range(hp):
            cols = slice(hh * HEAD_DIM, (hh + 1) * HEAD_DIM)
            qs = qs_ref[hh]
            mb = None
            for gg in range(grp):
                n = jnp.minimum(i * grp + gg, nb - 1)
                off = pl.multiple_of(n * MOBA_BLOCK, MOBA_BLOCK)
                s = _dot_nt(k_ref[pl.ds(off, MOBA_BLOCK), cols], qs) + bias_ref[hh, n]
                s_ref[slot, hh, gg] = s
                smax = jnp.max(s, axis=0, keepdims=True)
                mb = smax if mb is None else jnp.maximum(mb, smax)
            mb_ref[slot, hh] = mb

    def softmax_pv(i, slot):
        for hh in range(hp):
            rows = slice(hh * VT_ROWS, (hh + 1) * VT_ROWS)
            m_prev = m_ref[hh]
            m_new = jnp.maximum(m_prev, mb_ref[slot, hh])
            acc = jnp.exp2(m_prev - m_new) * acc_ref[hh]
            for gg in range(grp):
                p = jnp.exp2(s_ref[slot, hh, gg] - m_new)
                acc = acc + _dot(vt_ref[i * grp + gg, rows, :], p.astype(BF16))
            m_ref[hh] = m_new
            acc_ref[hh] = acc

    scores(0, 0)

    def body(k, carry):
        scores(2 * k + 1, 1)
        softmax_pv(2 * k, 0)
        scores(2 * k + 2, 0)
        softmax_pv(2 * k + 1, 1)
        return carry

    lax.fori_loop(0, n_iter // 2, body, 0)

    @pl.when(n_iter % 2 == 1)
    def _():
        softmax_pv(n_iter - 1, 0)

    for hh in range(hp):
        acc = acc_ref[hh]
        o = acc[:HEAD_DIM] * (1.0 / acc[HEAD_DIM:HEAD_DIM + 1])
        o_ref[:, hh * HEAD_DIM:(hh + 1) * HEAD_DIM] = o.T.astype(o_ref.dtype)


def _prompt_scratch(nb, hp, grp):
    return [
        pltpu.VMEM((hp, MOBA_BLOCK, HEAD_DIM), BF16),
        pltpu.VMEM((hp, nb, 1, MOBA_BLOCK), F32),
        pltpu.VMEM((hp, 1, MOBA_BLOCK), F32),
        pltpu.VMEM((hp, VT_ROWS, MOBA_BLOCK), F32),
        pltpu.VMEM((2, hp, grp, MOBA_BLOCK, MOBA_BLOCK), F32),
        pltpu.VMEM((2, hp, 1, MOBA_BLOCK), F32),
    ]


def _moba_sample_body(q_ref, kn_ref, vn_ref, k_pages, v_pages, o_ref, s_ref, *, t_new, between):
    n_pages = len(k_pages)
    pw = k_pages[0].shape[0]
    page = pw // N_HEADS
    rows = t_new * N_HEADS
    per_blk = MOBA_BLOCK // page
    n_blocks = n_pages // per_blk

    qa = (q_ref[0] * (HEAD_DIM ** -0.5 * LOG2E)).astype(BF16)
    r_i = lax.broadcasted_iota(jnp.int32, (rows, pw), 0)
    c_i = lax.broadcasted_iota(jnp.int32, (rows, pw), 1)
    same_head = (c_i % N_HEADS) == (r_i % N_HEADS)

    gcol = lax.broadcasted_iota(jnp.int32, (rows, LANES), 1).astype(F32)
    gate = jnp.full((rows, LANES), NEG, F32)
    bmax = jnp.full((rows, LANES), NEG, F32)
    for n in range(n_blocks):
        ssum = jnp.zeros((rows, pw), F32)
        smax = jnp.full((rows, pw), NEG, F32)
        for pp in range(per_blk):
            p_i = n * per_blk + pp
            sp = _dot_nt(qa, k_pages[p_i][...].astype(BF16))
            s_ref[:, p_i * pw:(p_i + 1) * pw] = sp
            ssum = ssum + jnp.where(same_head, sp, 0.0)
            smax = jnp.maximum(smax, jnp.where(same_head, sp, NEG))
        tot = jnp.sum(ssum, axis=-1, keepdims=True) * (1.0 / MOBA_BLOCK)
        gate = jnp.where(gcol == float(n), tot, gate)
        bmax = jnp.where(gcol == float(n), jnp.max(smax, axis=-1, keepdims=True), bmax)
    sel = _top_mask(gate, gcol, min(MOBA_TOPK, n_blocks))
    between()

    s_own = _dot_nt(qa, kn_ref[0].astype(BF16))
    orow = lax.broadcasted_iota(jnp.int32, s_own.shape, 0)
    ocol = lax.broadcasted_iota(jnp.int32, s_own.shape, 1)
    own_ok = jnp.logical_and(ocol % N_HEADS == orow % N_HEADS, ocol // N_HEADS <= orow // N_HEADS)
    s_own = jnp.where(own_ok, s_own, NEG)

    m = jnp.maximum(jnp.max(s_own, axis=-1, keepdims=True),
                    jnp.max(jnp.where(sel > 0.5, bmax, NEG), axis=-1, keepdims=True))
    p_own = jnp.exp2(s_own - m)
    l = jnp.sum(p_own, axis=-1, keepdims=True)
    acc = _dot(p_own.astype(BF16), vn_ref[0].astype(BF16))
    lsum = jnp.zeros((rows, pw), F32)
    for p_i in range(n_pages):
        n = p_i // per_blk
        sp = s_ref[:, p_i * pw:(p_i + 1) * pw]
        keep = jnp.logical_and(same_head, sel[:, n:n + 1] > 0.5)
        pp = jnp.where(keep, jnp.exp2(sp - m), 0.0)
        lsum = lsum + pp
        acc = acc + _dot(pp.astype(BF16), v_pages[p_i][...].astype(BF16))
    l = l + jnp.sum(lsum, axis=-1, keepdims=True)
    o_ref[0] = acc * (1.0 / l)


def _moba_kernel(pt_ref, q_ref, k_ref, vt_ref, km_ref, qs_ref, kn_ref, vn_ref, ck_ref, cv_ref, o_ref, os_ref,
                 qsc_ref, bias_ref, m_ref, acc_ref, s_ref, mb_ref, sc_ref, kbuf_ref, vbuf_ref, page_sem,
                 *, nb, hp, grp, n_pages, t_new):
    i = pl.program_id(0)
    own = i % nb
    slot = i % 2

    def fetch(row, sl):
        for p in range(n_pages):
            pg = pt_ref[row, p]
            pltpu.make_async_copy(ck_ref.at[pg], kbuf_ref.at[sl, p], page_sem.at[sl]).start()
            pltpu.make_async_copy(cv_ref.at[pg], vbuf_ref.at[sl, p], page_sem.at[sl]).start()

    @pl.when(i == 0)
    def _():
        fetch(0, 0)

    @pl.when(i + 1 < pl.num_programs(0))
    def _():
        fetch(i + 1, 1 - slot)

    for src, buf in ((ck_ref, kbuf_ref), (cv_ref, vbuf_ref)):
        pltpu.make_async_copy(src.at[pl.ds(0, n_pages)], buf.at[slot], page_sem.at[slot]).wait()
    k_pages = [kbuf_ref.at[slot, p] for p in range(n_pages)]
    v_pages = [vbuf_ref.at[slot, p] for p in range(n_pages)]

    def prompt_prologue():
        _moba_prompt_prologue(own, q_ref, k_ref, vt_ref, km_ref, qsc_ref, bias_ref, m_ref, acc_ref,
                              nb=nb, hp=hp)

    _moba_sample_body(qs_ref, kn_ref, vn_ref, k_pages, v_pages, os_ref, sc_ref, t_new=t_new,
                      between=prompt_prologue)
    _moba_prompt_main(own, k_ref, vt_ref, o_ref, qsc_ref, bias_ref, m_ref, acc_ref, s_ref, mb_ref,
                      nb=nb, hp=hp, grp=grp)


def moba_attention(q, k_bf, v_bf, kmean, q_s, k_new, v_new, cache_k, cache_v, page_table):
    s = q.shape[0]
    nb = s // MOBA_BLOCK
    hp = ATT_HEADS
    grp = ATT_GROUP if nb % ATT_GROUP == 0 else 1
    wide = hp * HEAD_DIM
    vt = v_bf.reshape(nb, MOBA_BLOCK, N_HEADS, HEAD_DIM).transpose(0, 2, 3, 1)
    vt = jnp.concatenate([vt, jnp.ones((nb, N_HEADS, VT_ROWS - HEAD_DIM, MOBA_BLOCK), BF16)], axis=2)
    vt = vt.reshape(nb, N_HEADS * VT_ROWS, MOBA_BLOCK)
    db, t_new, _ = q_s.shape
    th = t_new * N_HEADS
    q_s, k_new, v_new = (a.reshape(db, th, HEAD_DIM) for a in (q_s, k_new, v_new))
    assert db == (N_HEADS // hp) * nb, "one sample row per prompt tile step"
    n_pages = page_table.shape[1]
    page = cache_k.shape[-3]
    ck = cache_k.reshape(-1, page * N_HEADS, HEAD_DIM)
    cv = cache_v.reshape(-1, page * N_HEADS, HEAD_DIM)
    tok = pl.BlockSpec((1, th, HEAD_DIM), lambda i, pt: (i, 0, 0))
    tile = pl.BlockSpec((MOBA_BLOCK, wide), lambda i, pt: (i % nb, i // nb))

    page_rows = page * N_HEADS
    grid_spec = pltpu.PrefetchScalarGridSpec(
        num_scalar_prefetch=1,
        grid=(db,),
        in_specs=[
            tile,
            pl.BlockSpec((s, wide), lambda i, pt: (0, i // nb)),
            pl.BlockSpec((nb, hp * VT_ROWS, MOBA_BLOCK), lambda i, pt: (0, i // nb, 0)),
            pl.BlockSpec((nb, wide), lambda i, pt: (0, i // nb)),
            tok, tok, tok,
            pl.BlockSpec(memory_space=pl.ANY), pl.BlockSpec(memory_space=pl.ANY),
        ],
        out_specs=(tile, tok),
        scratch_shapes=_prompt_scratch(nb, hp, grp) + [
            pltpu.VMEM((th, n_pages * page_rows), F32),
            pltpu.VMEM((2, n_pages, page_rows, HEAD_DIM), F32),
            pltpu.VMEM((2, n_pages, page_rows, HEAD_DIM), F32),
            pltpu.SemaphoreType.DMA((2,)),
        ],
    )
    o_p, o_s = pl.pallas_call(
        functools.partial(_moba_kernel, nb=nb, hp=hp, grp=grp, n_pages=n_pages, t_new=t_new),
        grid_spec=grid_spec,
        out_shape=(jax.ShapeDtypeStruct((s, ATTN_WIDTH), BF16),
                   jax.ShapeDtypeStruct((db, th, HEAD_DIM), F32)),
        compiler_params=_cparams(("arbitrary",), ATTN_VMEM_LIMIT),
        name="moba_attention",
    )(page_table, q, k_bf, vt, kmean, q_s, k_new, v_new, ck, cv)
    return o_p, o_s.reshape(db, t_new, ATTN_WIDTH)


def _pool_sample_kernel(u_ref, st_ref, d_ref, new_ref, *, t_new, start_pos):
    ext = [st_ref[:, r, :] for r in range(POOL_STATE)] + [u_ref[:, t, :] for t in range(t_new)]
    for t in range(t_new):
        e = POOL_STATE + t
        parts = []
        for g, w in enumerate(POOL_WINDOWS):
            c0, c1 = g * POOL_GROUP_W, (g + 1) * POOL_GROUP_W
            wsum = ext[e][:, c0:c1]
            for jj in range(1, w):
                wsum = wsum + ext[e - jj][:, c0:c1]
            count = min(float(w), float(start_pos + t) + 1.0)
            parts.append(wsum * (1.0 / count) - ext[e][:, c0:c1])
        d_ref[:, t, :] = jnp.concatenate(parts, axis=-1).astype(d_ref.dtype)
    for r in range(POOL_STATE):
        new_ref[:, r, :] = ext[t_new + r]


def pool_sample(u, state, *, start_pos, bb=32):
    db, t_new, _ = u.shape
    return pl.pallas_call(
        functools.partial(_pool_sample_kernel, t_new=t_new, start_pos=start_pos),
        grid=(db // bb,),
        in_specs=[
            pl.BlockSpec((bb, t_new, POOL_WIDTH), lambda i: (i, 0, 0)),
            pl.BlockSpec((bb, POOL_STATE, POOL_WIDTH), lambda i: (i, 0, 0)),
        ],
        out_specs=(
            pl.BlockSpec((bb, t_new, POOL_WIDTH), lambda i: (i, 0, 0)),
            pl.BlockSpec((bb, POOL_STATE, POOL_WIDTH), lambda i: (i, 0, 0)),
        ),
        out_shape=(
            jax.ShapeDtypeStruct((db, t_new, POOL_WIDTH), F32),
            jax.ShapeDtypeStruct((db, POOL_STATE, POOL_WIDTH), F32),
        ),
        compiler_params=_cparams(("arbitrary",)),
        name="pool_sample",
    )(u, state)


def _out_proj_kernel(x_ref, oa_ref, d_ref, wp_ref, ps_ref, wo_ref, gf_ref, wr_ref, br_ref,
                     h2_ref, rt_ref, mix_ref):
    mix_ref[:, 0:ATTN_WIDTH] = oa_ref[...].astype(BF16)
    dd = d_ref[...].astype(BF16)
    for g in range(len(POOL_WINDOWS)):
        c0, c1 = g * POOL_GROUP_W, (g + 1) * POOL_GROUP_W
        yg = _dot(dd[:, c0:c1], wp_ref[g].astype(BF16)) * ps_ref[:, c0:c1]
        mix_ref[:, ATTN_WIDTH + c0:ATTN_WIDTH + c1] = yg.astype(BF16)
    h2 = x_ref[...] + _dot(mix_ref[...], wo_ref[...])
    h2_ref[...] = h2
    ms = jnp.mean(h2 * h2, axis=-1, keepdims=True)
    hn = h2 * lax.rsqrt(ms + EPS) * gf_ref[...]

    hn_hi = hn.astype(BF16)
    hn_lo = (hn - hn_hi.astype(F32)).astype(BF16)
    t = _dot(hn_hi, wr_ref[...])
    logits = t[:, :LANES] + t[:, LANES:] + _dot(hn_lo, wr_ref[:, :LANES]) + br_ref[...]
    lane = lax.broadcasted_iota(jnp.int32, logits.shape, 1).astype(F32)
    far = jnp.float32(LANES)
    is_g = lane < N_EXPERT_GROUPS
    gl = jnp.where(is_g, logits, -jnp.inf)
    g_max = jnp.max(gl, axis=-1, keepdims=True)
    g_top = jnp.min(jnp.where(gl == g_max, lane, far), axis=-1, keepdims=True)
    g_p = 1.0 / jnp.sum(jnp.where(is_g, jnp.exp(gl - g_max), 0.0), axis=-1, keepdims=True)
    lo = N_EXPERT_GROUPS + g_top * EXPERTS_PER_GROUP
    in_grp = jnp.logical_and(lane >= lo, lane < lo + EXPERTS_PER_GROUP)
    el = jnp.where(in_grp, logits, -jnp.inf)
    e1 = jnp.max(el, axis=-1, keepdims=True)
    i1 = jnp.min(jnp.where(el == e1, lane, far), axis=-1, keepdims=True)
    el2 = jnp.where(lane == i1, -jnp.inf, el)
    e2 = jnp.max(el2, axis=-1, keepdims=True)
    i2 = jnp.min(jnp.where(el2 == e2, lane, far), axis=-1, keepdims=True)
    ex2 = jnp.exp(e2 - e1)
    den = 1.0 + ex2
    w1 = (1.0 / den) * g_p
    w2 = (ex2 / den) * g_p
    rt_ref[...] = jnp.where(lane == 0.0, i1 - N_EXPERT_GROUPS,
                            jnp.where(lane == 1.0, i2 - N_EXPERT_GROUPS,
                                      jnp.where(lane == 2.0, w1, jnp.where(lane == 3.0, w2, 0.0))))


def out_proj(x, o_attn, d, w_pool, pool_scale, w_out_bf, g_ffn, w_router, b_router, *, tm):
    n = x.shape[0]
    row = lambda i: (i, 0)
    const = lambda i: (0, 0)
    return pl.pallas_call(
        _out_proj_kernel,
        grid=(n // tm,),
        in_specs=[
            pl.BlockSpec((tm, D_MODEL), row),
            pl.BlockSpec((tm, ATTN_WIDTH), row),
            pl.BlockSpec((tm, POOL_WIDTH), row),
            pl.BlockSpec(w_pool.shape, lambda i: (0, 0, 0)),
            pl.BlockSpec((1, POOL_WIDTH), const),
            pl.BlockSpec((D_MODEL, D_MODEL), const),
            pl.BlockSpec((1, D_MODEL), const),
            pl.BlockSpec((D_MODEL, 2 * LANES), const),
            pl.BlockSpec((1, LANES), const),
        ],
        out_specs=(
            pl.BlockSpec((tm, D_MODEL), row),
            pl.BlockSpec((tm, LANES), row),
        ),
        out_shape=(
            jax.ShapeDtypeStruct((n, D_MODEL), F32),
            jax.ShapeDtypeStruct((n, LANES), F32),
        ),
        scratch_shapes=[pltpu.VMEM((tm, D_MODEL), BF16)],
        compiler_params=_cparams(("arbitrary",)),
        name="out_proj",
    )(x, o_attn, d, w_pool, pool_scale, w_out_bf, g_ffn, w_router, b_router)


MOE_TILE = 256
ROW_TILE = 256


def _route_kernel(rt_ref, pos_ref, meta_ref, *, n_tiles):
    lane = lax.broadcasted_iota(jnp.int32, (ROW_TILE, LANES), 1).astype(F32)
    r_i = lax.broadcasted_iota(jnp.int32, (ROW_TILE, ROW_TILE), 0)
    c_i = lax.broadcasted_iota(jnp.int32, (ROW_TILE, ROW_TILE), 1)
    tri = jnp.where(c_i < r_i, 1.0, 0.0).astype(BF16)

    def one_hot(t):
        rt = rt_ref[pl.ds(pl.multiple_of(t * ROW_TILE, ROW_TILE), ROW_TILE), :]
        e1, e2 = rt[:, 0:1], rt[:, 1:2]
        return e1, e2, jnp.where(jnp.logical_or(lane == e1, lane == e2), 1.0, 0.0)

    def count(t, cnt):
        return cnt + jnp.sum(one_hot(t)[2], axis=0, keepdims=True)

    cnt = lax.fori_loop(0, n_tiles, count, jnp.zeros((1, LANES), F32))
    tiles_per = jnp.floor((cnt + (MOE_TILE - 1)) * (1.0 / MOE_TILE))
    e_r = lax.broadcasted_iota(jnp.int32, (LANES, LANES), 0)
    e_c = lax.broadcasted_iota(jnp.int32, (LANES, LANES), 1)
    upper = jnp.where(e_r < e_c, 1.0, 0.0).astype(BF16)
    off_tiles = _dot(jnp.broadcast_to(tiles_per, (8, LANES)).astype(BF16), upper)[0:1]
    base = off_tiles * MOE_TILE

    def place(t, run):
        e1, e2, oh = one_hot(t)
        dest = base + run + _dot(tri, oh.astype(BF16))
        p1 = jnp.sum(jnp.where(lane == e1, dest, 0.0), axis=1, keepdims=True)
        p2 = jnp.sum(jnp.where(lane == e2, dest, 0.0), axis=1, keepdims=True)
        pos = jnp.where(lane == 0.0, p1, jnp.where(lane == 1.0, p2, 0.0))
        pos_ref[pl.ds(pl.multiple_of(t * ROW_TILE, ROW_TILE), ROW_TILE), :] = pos.astype(jnp.int32)
        return run + jnp.sum(oh, axis=0, keepdims=True)

    lax.fori_loop(0, n_tiles, place, jnp.zeros((1, LANES), F32))
    row = lax.broadcasted_iota(jnp.int32, (8, LANES), 0)
    meta_ref[...] = jnp.where(row == 0, tiles_per, jnp.where(row == 1, off_tiles, 0.0))


def route(rt):
    n = rt.shape[0]
    assert n % ROW_TILE == 0
    return pl.pallas_call(
        functools.partial(_route_kernel, n_tiles=n // ROW_TILE),
        out_shape=(
            jax.ShapeDtypeStruct((n, LANES), jnp.int32),
            jax.ShapeDtypeStruct((8, LANES), F32),
        ),
        compiler_params=pltpu.CompilerParams(vmem_limit_bytes=VMEM_LIMIT),
        name="route",
    )(rt)


ROW_UNROLL = 8


def _dispatch_kernel(p1_ref, p2_ref, zt_ref, xa_ref, xb_ref, xs_ref, zero_ref, sem, zsem, *, tm, tiles_a, n_zero):
    i = pl.program_id(0)

    @pl.when(i == 0)
    def _():
        zero_ref[...] = jnp.zeros(zero_ref.shape, zero_ref.dtype)

        def zcopy(z):
            row0 = pl.multiple_of(zt_ref[z] * MOE_TILE, MOE_TILE)
            return pltpu.make_async_copy(zero_ref, xs_ref.at[pl.ds(row0, MOE_TILE)], zsem)

        for z in range(n_zero):
            @pl.when(zt_ref[z] >= 0)
            def _():
                zcopy(z).start()
        for z in range(n_zero):
            @pl.when(zt_ref[z] >= 0)
            def _():
                zcopy(z).wait()

    def scatter(x_ref):
        t0 = i * tm

        def issue(r, c):
            src = x_ref.at[pl.ds(r, 1)]
            pltpu.make_async_copy(src, xs_ref.at[pl.ds(p1_ref[t0 + r], 1)], sem).start()
            pltpu.make_async_copy(src, xs_ref.at[pl.ds(p2_ref[t0 + r], 1)], sem).start()
            return c

        lax.fori_loop(0, tm, issue, 0, unroll=ROW_UNROLL)
        for _ in range(2):
            pltpu.make_async_copy(x_ref, xs_ref.at[pl.ds(0, tm)], sem).wait()

    @pl.when(i < tiles_a)
    def _():
        scatter(xa_ref)

    @pl.when(i >= tiles_a)
    def _():
        scatter(xb_ref)


def dispatch(p1, p2, zero_tiles, xa, xb, rows, *, tm):
    tiles_a, tiles_b = xa.shape[0] // tm, xb.shape[0] // tm
    grid_spec = pltpu.PrefetchScalarGridSpec(
        num_scalar_prefetch=3,
        grid=(tiles_a + tiles_b,),
        in_specs=[
            pl.BlockSpec((tm, D_MODEL), lambda i, a, b, z: (jnp.minimum(i, tiles_a - 1), 0)),
            pl.BlockSpec((tm, D_MODEL), lambda i, a, b, z: (jnp.maximum(i - tiles_a, 0), 0)),
        ],
        out_specs=pl.BlockSpec(memory_space=pl.ANY),
        scratch_shapes=[pltpu.VMEM((MOE_TILE, D_MODEL), F32), pltpu.SemaphoreType.DMA(()), pltpu.SemaphoreType.DMA(())],
    )
    return pl.pallas_call(
        functools.partial(_dispatch_kernel, tm=tm, tiles_a=tiles_a, n_zero=zero_tiles.shape[0]),
        grid_spec=grid_spec,
        out_shape=jax.ShapeDtypeStruct((rows, D_MODEL), F32),
        compiler_params=_cparams(("arbitrary",)),
        name="dispatch",
    )(p1, p2, zero_tiles, xa, xb)


def _experts_kernel(te_ref, ts_ref, tf_ref, sl_ref, nx_ref, nu_ref, x_ref, gf_ref, wg_hbm, wu_hbm, wd_hbm, o_ref,
                    wg_buf, wu_buf, wd_buf, wsem, *, layer):
    j = pl.program_id(0)
    used = j < nu_ref[0]
    first = jnp.logical_and(used, tf_ref[j] == 1)
    slot = sl_ref[j]

    def copies(e, sl):
        return (pltpu.make_async_copy(wg_hbm.at[layer, e], wg_buf.at[sl], wsem.at[sl]),
                pltpu.make_async_copy(wu_hbm.at[layer, e], wu_buf.at[sl], wsem.at[sl]),
                pltpu.make_async_copy(wd_hbm.at[layer, e], wd_buf.at[sl], wsem.at[sl]))

    @pl.when(j == 0)
    def _():
        for c in copies(te_ref[0], 0):
            c.start()

    @pl.when(jnp.logical_and(first, nx_ref[j] >= 0))
    def _():
        for c in copies(nx_ref[j], 1 - slot):
            c.start()

    @pl.when(first)
    def _():
        for c in copies(te_ref[j], slot):
            c.wait()

    @pl.when(used)
    def _():
        h = x_ref[...]
        ms = jnp.mean(h * h, axis=-1, keepdims=True)
        x = (h * lax.rsqrt(ms + EPS) * gf_ref[...]).astype(BF16).astype(F32)
        a = _dot(x, wg_buf[slot])
        b = _dot(x, wu_buf[slot])
        act = (a * (1.0 / (1.0 + jnp.exp(-a)))) * b
        o_ref[...] = _dot(act.astype(BF16).astype(F32), wd_buf[slot])

    @pl.when(jnp.logical_not(used))
    def _():
        o_ref[...] = jnp.zeros(o_ref.shape, o_ref.dtype)


def experts(tile_expert, tile_src, tile_first, tile_slot, tile_next, n_used, xs, g_ffn, w_gate, w_up, w_down, *, layer):
    rows, w = xs.shape
    idx = lambda f: (lambda j, te, ts, tf, sl, nx, nu: f(j, ts))
    grid_spec = pltpu.PrefetchScalarGridSpec(
        num_scalar_prefetch=6,
        grid=(rows // MOE_TILE,),
        in_specs=[
            pl.BlockSpec((MOE_TILE, w), idx(lambda j, ts: (ts[j], 0))),
            pl.BlockSpec((1, D_MODEL), idx(lambda j, ts: (0, 0))),
            pl.BlockSpec(memory_space=pl.ANY), pl.BlockSpec(memory_space=pl.ANY), pl.BlockSpec(memory_space=pl.ANY),
        ],
        out_specs=pl.BlockSpec((MOE_TILE, D_MODEL), idx(lambda j, ts: (j, 0))),
        scratch_shapes=[
            pltpu.VMEM((2, D_MODEL, D_EXPERT), F32), pltpu.VMEM((2, D_MODEL, D_EXPERT), F32),
            pltpu.VMEM((2, D_EXPERT, D_MODEL), F32), pltpu.SemaphoreType.DMA((2,)),
        ],
    )
    return pl.pallas_call(
        functools.partial(_experts_kernel, layer=layer),
        grid_spec=grid_spec,
        out_shape=jax.ShapeDtypeStruct((rows, D_MODEL), F32),
        compiler_params=_cparams(("arbitrary",), EXPERT_VMEM_LIMIT),
        name="experts",
    )(tile_expert, tile_src, tile_first, tile_slot, tile_next, n_used, xs, g_ffn, w_gate, w_up, w_down)


def _combine_kernel(p1_ref, p2_ref, h2_ref, rt_ref, os_ref, y_ref, buf_ref, sem, *, tm):
    i = pl.program_id(0)
    slot = i % 2

    def gather(step, sl):
        t0 = step * tm

        def issue(r, c):
            pltpu.make_async_copy(os_ref.at[pl.ds(p1_ref[t0 + r], 1)], buf_ref.at[sl, 0, pl.ds(r, 1)],
                                  sem.at[sl]).start()
            pltpu.make_async_copy(os_ref.at[pl.ds(p2_ref[t0 + r], 1)], buf_ref.at[sl, 1, pl.ds(r, 1)],
                                  sem.at[sl]).start()
            return c

        lax.fori_loop(0, tm, issue, 0, unroll=ROW_UNROLL)

    @pl.when(i == 0)
    def _():
        gather(0, 0)

    @pl.when(i + 1 < pl.num_programs(0))
    def _():
        gather(i + 1, 1 - slot)

    for s in range(2):
        pltpu.make_async_copy(os_ref.at[pl.ds(0, tm)], buf_ref.at[slot, s], sem.at[slot]).wait()
    rt = rt_ref[...]
    y_ref[...] = h2_ref[...] + rt[:, 2:3] * buf_ref[slot, 0] + rt[:, 3:4] * buf_ref[slot, 1]


def combine(p1, p2, h2, rt, os, *, tm):
    n = h2.shape[0]
    row = lambda i, a, b: (i, 0)
    grid_spec = pltpu.PrefetchScalarGridSpec(
        num_scalar_prefetch=2,
        grid=(n // tm,),
        in_specs=[pl.BlockSpec((tm, D_MODEL), row), pl.BlockSpec((tm, LANES), row),
                  pl.BlockSpec(memory_space=pl.ANY)],
        out_specs=pl.BlockSpec((tm, D_MODEL), row),
        scratch_shapes=[pltpu.VMEM((2, 2, tm, D_MODEL), F32), pltpu.SemaphoreType.DMA((2,))],
    )
    return pl.pallas_call(
        functools.partial(_combine_kernel, tm=tm),
        grid_spec=grid_spec,
        out_shape=jax.ShapeDtypeStruct((n, D_MODEL), F32),
        compiler_params=_cparams(("arbitrary",)),
        name="combine",
    )(p1, p2, h2, rt, os)


def kernel(x_prompt, x_sample, cache_k, cache_v, state_pool, page_table, g_mix, w_in, g_q, g_k, w_pool, pool_scale, w_out, g_ffn, w_group_router, b_group_router, w_expert_router, b_expert_router, w_gate, w_up, w_down):
    B, S, _ = x_prompt.shape
    DB, T, _ = x_sample.shape
    depth = w_in.shape[0]
    assert B == 1 and depth == 1
    past_len = page_table.shape[1] * cache_k.shape[2]
    l = 0

    w_in_bf = w_in[l].astype(BF16)
    w_out_bf = w_out[l].astype(BF16)
    gm, gq, gk, gf = g_mix[l][None], g_q[l][None], g_k[l][None], g_ffn[l][None]
    ps = pool_scale[l][None]
    n_r = N_EXPERT_GROUPS + N_EXPERTS
    w_router = jnp.concatenate([w_group_router[l], w_expert_router[l].reshape(D_MODEL, N_EXPERTS)], axis=1)
    w_router = jnp.pad(w_router, ((0, 0), (0, LANES - n_r)))
    w_router_hi = w_router.astype(BF16)
    w_router = jnp.concatenate([w_router_hi, (w_router - w_router_hi.astype(F32)).astype(BF16)], axis=1)
    b_router = jnp.concatenate([b_group_router[l], b_expert_router[l].reshape(N_EXPERTS)])
    b_router = jnp.pad(b_router, (0, LANES - n_r))[None]

    def mixer_tail(x2d, o_attn, d):
        return out_proj(x2d, o_attn, d, w_pool[l], ps, w_out_bf, gf, w_router, b_router, tm=PROJ_TILE)

    xp = x_prompt.reshape(S, D_MODEL)
    n_s = DB * T
    xs = x_sample.reshape(n_s, D_MODEL)
    q_p, k_p, kb_p, km_p, v_p, vb_p, u_p, d_p = in_proj(xp, gm, w_in_bf, gq, gk, tm=PROJ_TILE, with_pool=True)
    q_s, k_s, _, _, v_s, _, u_s = in_proj(xs, gm, w_in_bf, gq, gk, tm=n_s, with_pool=False)
    r3 = lambda a: a.reshape(DB, T, ATTN_WIDTH)
    o_p, o_s = moba_attention(q_p, kb_p, vb_p, km_p.reshape(S // MOBA_BLOCK, ATTN_WIDTH),
                              r3(q_s), r3(k_s), r3(v_s), cache_k, cache_v, page_table + l * cache_k.shape[1])
    d_s, pool_s = pool_sample(u_s.reshape(DB, T, POOL_WIDTH), state_pool[l], start_pos=past_len)
    h2_p, rt_p = mixer_tail(xp, o_p, d_p)
    h2_s, rt_s = mixer_tail(xs, o_s.reshape(n_s, ATTN_WIDTH), d_s.reshape(n_s, POOL_WIDTH))

    pos, meta = route(jnp.concatenate([rt_p, rt_s], axis=0))
    p1, p2 = pos[:, 0], pos[:, 1]
    max_tiles = -(-2 * (S + n_s) // MOE_TILE) + N_EXPERTS
    tiles_per = meta[0, :N_EXPERTS].astype(jnp.int32)
    ends = tiles_per + meta[1, :N_EXPERTS].astype(jnp.int32)
    n_used = ends[N_EXPERTS - 1]
    tile_src = jnp.minimum(jnp.arange(max_tiles, dtype=jnp.int32), n_used - 1)
    tile_expert = jnp.minimum(jnp.sum(tile_src[:, None] >= ends[None, :], axis=1), N_EXPERTS - 1).astype(jnp.int32)
    tail = n_used + jnp.arange(N_EXPERTS, dtype=jnp.int32)
    zero_tiles = jnp.concatenate([jnp.where(tiles_per > 0, ends - 1, -1), jnp.where(tail < max_tiles, tail, -1)])
    x_sorted = dispatch(p1, p2, zero_tiles, h2_p, h2_s, max_tiles * MOE_TILE, tm=ROW_TILE)
    tile_ids = jnp.arange(max_tiles, dtype=jnp.int32)
    tile_first = jnp.logical_and(tile_ids < n_used,
                                 jnp.logical_or(tile_ids == 0, tile_expert != jnp.roll(tile_expert, 1))).astype(jnp.int32)
    tile_slot = ((jnp.cumsum(tile_first) - 1) % 2).astype(jnp.int32)
    e_ids = jnp.arange(N_EXPERTS, dtype=jnp.int32)
    later = jnp.logical_and(e_ids[None, :] > e_ids[:, None], tiles_per[None, :] > 0)
    next_nonempty = jnp.min(jnp.where(later, e_ids[None, :], N_EXPERTS), axis=1)
    tile_next = jnp.where(next_nonempty < N_EXPERTS, next_nonempty, -1)[tile_expert].astype(jnp.int32)
    o_sorted = experts(tile_expert, tile_src, tile_first, tile_slot, tile_next, n_used[None], x_sorted, gf,
                       w_gate, w_up, w_down, layer=l)
    y_prompt = combine(p1[:S], p2[:S], h2_p, rt_p, o_sorted, tm=ROW_TILE).reshape(B, S, D_MODEL)
    y_sample = combine(p1[S:], p2[S:], h2_s, rt_s, o_sorted, tm=ROW_TILE).reshape(DB, T, D_MODEL)

    hd = (N_HEADS, HEAD_DIM)
    return (
        y_prompt,
        y_sample,
        k_p.reshape(1, B, S, *hd),
        v_p.reshape(1, B, S, *hd),
        u_p[S - POOL_STATE:].reshape(1, B, POOL_STATE, POOL_WIDTH),
        k_s.reshape(1, DB, T, *hd),
        v_s.reshape(1, DB, T, *hd),
        pool_s.reshape(1, DB, POOL_STATE, POOL_WIDTH),
    )
```

```python
import functools

import jax
import jax.numpy as jnp
from jax import lax
from jax.experimental import pallas as pl
from jax.experimental.pallas import tpu as pltpu

D_MODEL = 2048
ATTN_WIDTH = 1024
POOL_WIDTH = 1024
HEAD_DIM = 128
N_HEADS = 8
POOL_WINDOWS = (2, 4, 8, 16)
POOL_GROUP_W = 256
POOL_STATE = 15
MOBA_BLOCK = 256
MOBA_TOPK = 3
N_EXPERT_GROUPS = 4
EXPERTS_PER_GROUP = 4
N_EXPERTS = 16
D_EXPERT = 768
EPS = 1e-6
NEG = -1e30
LANES = 128
VMEM_LIMIT = 56 * 1024 * 1024
PROJ_TILE = 512
ATTN_VMEM_LIMIT = 60 * 1024 * 1024
EXPERT_VMEM_LIMIT = 60 * 1024 * 1024

BF16 = jnp.bfloat16
F32 = jnp.float32


def _cparams(sem, vmem_limit=VMEM_LIMIT):
    return pltpu.CompilerParams(dimension_semantics=sem, vmem_limit_bytes=vmem_limit)


def _dot(a, b):
    return jnp.dot(a, b, preferred_element_type=F32)


def _dot_nt(a, b, precision=None):
    return lax.dot_general(a, b, (((1,), (1,)), ((), ())), precision=precision,
                           preferred_element_type=F32)


def _top_mask(g, ids, k, axis=-1):
    sel = jnp.zeros(g.shape, F32)
    for _ in range(k):
        m = jnp.max(g, axis=axis, keepdims=True)
        idx = jnp.min(jnp.where(g == m, ids, jnp.float32(g.shape[axis])), axis=axis, keepdims=True)
        pick = ids == idx
        sel = jnp.where(pick, 1.0, sel)
        g = jnp.where(pick, -jnp.inf, g)
    return sel


def _in_proj_kernel(x_ref, gmix_ref, w_ref, gq_ref, gk_ref,
                    q_ref, k_ref, kb_ref, km_ref, v_ref, vb_ref, u_ref, *rest, tm, with_pool):
    if with_pool:
        d_ref, vt_ref, hb_ref, ext_ref = rest
    else:
        (hb_ref,) = rest
    xf = x_ref[...]
    ms = jnp.mean(xf * xf, axis=-1, keepdims=True)
    hb_ref[...] = (xf * lax.rsqrt(ms + EPS) * gmix_ref[...]).astype(BF16)

    def head_norm(z, out_ref, g_ref):
        for h in range(N_HEADS):
            zh = z[:, h * HEAD_DIM:(h + 1) * HEAD_DIM]
            ms = jnp.mean(zh * zh, axis=-1, keepdims=True)
            out_ref[:, h * HEAD_DIM:(h + 1) * HEAD_DIM] = zh * lax.rsqrt(ms + EPS) * g_ref[...]

    head_norm(_dot(hb_ref[...], w_ref[:, 0:ATTN_WIDTH]), q_ref, gq_ref)
    head_norm(_dot(hb_ref[...], w_ref[:, ATTN_WIDTH:2 * ATTN_WIDTH]), k_ref, gk_ref)
    kb_ref[...] = k_ref[...].astype(BF16)
    for g in range(tm // MOBA_BLOCK):
        blk = k_ref[g * MOBA_BLOCK:(g + 1) * MOBA_BLOCK, :]
        km_ref[g] = jnp.mean(blk, axis=0, keepdims=True)
    zv = _dot(hb_ref[...], w_ref[:, 2 * ATTN_WIDTH:3 * ATTN_WIDTH])
    v_ref[...] = zv
    vb_ref[...] = zv.astype(BF16)
    if with_pool:
        for g in range(tm // MOBA_BLOCK):
            for h in range(N_HEADS):
                blk = zv[g * MOBA_BLOCK:(g + 1) * MOBA_BLOCK, h * HEAD_DIM:(h + 1) * HEAD_DIM]
                vt_ref[g, h * VT_ROWS:h * VT_ROWS + HEAD_DIM, :] = blk.T.astype(BF16)
                vt_ref[g, h * VT_ROWS + HEAD_DIM:(h + 1) * VT_ROWS, :] = jnp.ones(
                    (VT_ROWS - HEAD_DIM, MOBA_BLOCK), BF16)
    zu = _dot(hb_ref[...], w_ref[:, 3 * ATTN_WIDTH:])
    u_ref[...] = zu
    if with_pool:
        _pool_differences(zu, d_ref, ext_ref, tm=tm)


POOL_PAD = 16


def _pool_differences(u, d_ref, ext_ref, *, tm):
    i = pl.program_id(0)

    @pl.when(i == 0)
    def _():
        ext_ref[0:POOL_PAD, :] = jnp.zeros((POOL_PAD, POOL_WIDTH), F32)

    ext_ref[POOL_PAD:POOL_PAD + tm, :] = u
    pos = (i * tm + lax.broadcasted_iota(jnp.int32, (tm, 1), 0)).astype(F32)
    for g, w in enumerate(POOL_WINDOWS):
        c0, c1 = g * POOL_GROUP_W, (g + 1) * POOL_GROUP_W
        wsum = u[:, c0:c1]
        for jj in range(1, w):
            wsum = wsum + ext_ref[POOL_PAD - jj:POOL_PAD - jj + tm, c0:c1]
        inv = 1.0 / jnp.minimum(jnp.float32(w), pos + 1.0)
        d_ref[:, c0:c1] = (wsum * inv - u[:, c0:c1]).astype(d_ref.dtype)
    ext_ref[0:POOL_PAD, :] = ext_ref[tm:tm + POOL_PAD, :]


def in_proj(x, g_mix, w_in_bf, g_q, g_k, *, tm, with_pool):
    n = x.shape[0]
    wide = lambda i: (i, 0)
    const = lambda i: (0, 0)
    out_shape = (
        jax.ShapeDtypeStruct((n, ATTN_WIDTH), F32),
        jax.ShapeDtypeStruct((n, ATTN_WIDTH), F32),
        jax.ShapeDtypeStruct((n, ATTN_WIDTH), BF16),
        jax.ShapeDtypeStruct((n // MOBA_BLOCK, 1, ATTN_WIDTH), F32),
        jax.ShapeDtypeStruct((n, ATTN_WIDTH), F32),
        jax.ShapeDtypeStruct((n, ATTN_WIDTH), BF16),
        jax.ShapeDtypeStruct((n, POOL_WIDTH), F32),
    )
    blk = pl.BlockSpec((tm, ATTN_WIDTH), wide)
    vt_shape = (n // MOBA_BLOCK, N_HEADS * VT_ROWS, MOBA_BLOCK)
    pool_out = (jax.ShapeDtypeStruct((n, POOL_WIDTH), BF16), jax.ShapeDtypeStruct(vt_shape, BF16)) if with_pool else ()
    pool_specs = (blk, pl.BlockSpec((tm // MOBA_BLOCK,) + vt_shape[1:], lambda i: (i, 0, 0))) if with_pool else ()
    pool_scratch = [pltpu.VMEM((POOL_PAD + tm, POOL_WIDTH), F32)] if with_pool else []
    return pl.pallas_call(
        functools.partial(_in_proj_kernel, tm=tm, with_pool=with_pool),
        grid=(n // tm,),
        in_specs=[
            pl.BlockSpec((tm, D_MODEL), wide),
            pl.BlockSpec((1, D_MODEL), const),
            pl.BlockSpec(w_in_bf.shape, const, pipeline_mode=pl.Buffered(1)),
            pl.BlockSpec((1, HEAD_DIM), const),
            pl.BlockSpec((1, HEAD_DIM), const),
        ],
        out_specs=(blk, blk, blk,
                   pl.BlockSpec((tm // MOBA_BLOCK, 1, ATTN_WIDTH), lambda i: (i, 0, 0)),
                   blk, blk, blk) + pool_specs,
        out_shape=out_shape + pool_out,
        scratch_shapes=[pltpu.VMEM((tm, D_MODEL), BF16)] + pool_scratch,
        compiler_params=_cparams(("arbitrary",)),
        name="in_proj",
    )(x, g_mix, w_in_bf, g_q, g_k)


LOG2E = 1.4426950408889634
ATT_HEADS = 2
ATT_GROUP = 4
VT_ROWS = HEAD_DIM + 16


def _moba_prompt_prologue(own, q_ref, k_ref, vt_ref, km_ref, qs_ref, bias_ref, m_ref, acc_ref, *, nb, hp):
    base = pl.multiple_of(own * MOBA_BLOCK, MOBA_BLOCK)
    blk_id = lax.broadcasted_iota(jnp.int32, (nb, MOBA_BLOCK), 0).astype(F32)
    valid = blk_id < own.astype(F32)
    key_i = lax.broadcasted_iota(jnp.int32, (MOBA_BLOCK, MOBA_BLOCK), 0)
    qry_i = lax.broadcasted_iota(jnp.int32, (MOBA_BLOCK, MOBA_BLOCK), 1)

    for hh in range(hp):
        cols = slice(hh * HEAD_DIM, (hh + 1) * HEAD_DIM)
        q = q_ref[:, cols]
        gate = _dot_nt(km_ref[:, cols], q, precision=lax.Precision.HIGHEST)
        sel = _top_mask(jnp.where(valid, gate, NEG), blk_id, MOBA_TOPK, axis=0)
        bias = jnp.where(jnp.logical_and(sel > 0.5, valid), 0.0, NEG)
        for n in range(nb):
            bias_ref[hh, n] = bias[n:n + 1, :]
        qs = (q * (HEAD_DIM ** -0.5 * LOG2E)).astype(BF16)
        qs_ref[hh] = qs
        s = _dot_nt(k_ref[pl.ds(base, MOBA_BLOCK), cols], qs)
        s = jnp.where(key_i <= qry_i, s, NEG)
        m0 = jnp.max(s, axis=0, keepdims=True)
        p = jnp.exp2(s - m0)
        m_ref[hh] = m0
        acc_ref[hh] = _dot(vt_ref[own, hh * VT_ROWS:(hh + 1) * VT_ROWS, :], p.astype(BF16))


def _moba_prompt_main(own, k_ref, vt_ref, o_ref, qs_ref, bias_ref, m_ref, acc_ref, s_ref, mb_ref,
                      *, nb, hp, grp):
    n_iter = (own + grp - 1) // grp

    def scores(i, slot):
        for hh in range(hp):
            cols = slice(hh * HEAD_DIM, (hh + 1) * HEAD_DIM)
            qs = qs_ref[hh]
            mb = None
            for gg in range(grp):
                n = jnp.minimum(i * grp + gg, nb - 1)
                off = pl.multiple_of(n * MOBA_BLOCK, MOBA_BLOCK)
                s = _dot_nt(k_ref[pl.ds(off, MOBA_BLOCK), cols], qs) + bias_ref[hh, n]
                s_ref[slot, hh, gg] = s
                smax = jnp.max(s, axis=0, keepdims=True)
                mb = smax if mb is None else jnp.maximum(mb, smax)
            mb_ref[slot, hh] = mb

    def softmax_pv(i, slot):
        for hh in range(hp):
            rows = slice(hh * VT_ROWS, (hh + 1) * VT_ROWS)
            m_prev = m_ref[hh]
            m_new = jnp.maximum(m_prev, mb_ref[slot, hh])
            acc = jnp.exp2(m_prev - m_new) * acc_ref[hh]
            for gg in range(grp):
                p = jnp.exp2(s_ref[slot, hh, gg] - m_new)
                acc = acc + _dot(vt_ref[i * grp + gg, rows, :], p.astype(BF16))
            m_ref[hh] = m_new
            acc_ref[hh] = acc

    scores(0, 0)

    def body(k, carry):
        scores(2 * k + 1, 1)
        softmax_pv(2 * k, 0)
        scores(2 * k + 2, 0)
        softmax_pv(2 * k + 1, 1)
        return carry

    lax.fori_loop(0, n_iter // 2, body, 0)

    @pl.when(n_iter % 2 == 1)
    def _():
        softmax_pv(n_iter - 1, 0)

    for hh in range(hp):
        acc = acc_ref[hh]
        o = acc[:HEAD_DIM] * (1.0 / acc[HEAD_DIM:HEAD_DIM + 1])
        o_ref[:, hh * HEAD_DIM:(hh + 1) * HEAD_DIM] = o.T.astype(o_ref.dtype)


def _prompt_scratch(nb, hp, grp):
    return [
        pltpu.VMEM((hp, MOBA_BLOCK, HEAD_DIM), BF16),
        pltpu.VMEM((hp, nb, 1, MOBA_BLOCK), F32),
        pltpu.VMEM((hp, 1, MOBA_BLOCK), F32),
        pltpu.VMEM((hp, VT_ROWS, MOBA_BLOCK), F32),
        pltpu.VMEM((2, hp, grp, MOBA_BLOCK, MOBA_BLOCK), F32),
        pltpu.VMEM((2, hp, 1, MOBA_BLOCK), F32),
    ]


def _moba_sample_body(q_ref, kn_ref, vn_ref, k_pages, v_pages, o_ref, s_ref, *, t_new, between):
    n_pages = len(k_pages)
    pw = k_pages[0].shape[0]
    page = pw // N_HEADS
    rows = t_new * N_HEADS
    per_blk = MOBA_BLOCK // page
    n_blocks = n_pages // per_blk

    qa = (q_ref[0] * (HEAD_DIM ** -0.5 * LOG2E)).astype(BF16)
    r_i = lax.broadcasted_iota(jnp.int32, (rows, pw), 0)
    c_i = lax.broadcasted_iota(jnp.int32, (rows, pw), 1)
    same_head = (c_i % N_HEADS) == (r_i % N_HEADS)

    gcol = lax.broadcasted_iota(jnp.int32, (rows, LANES), 1).astype(F32)
    gate = jnp.full((rows, LANES), NEG, F32)
    bmax = jnp.full((rows, LANES), NEG, F32)
    for n in range(n_blocks):
        ssum = jnp.zeros((rows, pw), F32)
        smax = jnp.full((rows, pw), NEG, F32)
        for pp in range(per_blk):
            p_i = n * per_blk + pp
            sp = _dot_nt(qa, k_pages[p_i][...].astype(BF16))
            s_ref[:, p_i * pw:(p_i + 1) * pw] = sp
            ssum = ssum + jnp.where(same_head, sp, 0.0)
            smax = jnp.maximum(smax, jnp.where(same_head, sp, NEG))
        tot = jnp.sum(ssum, axis=-1, keepdims=True) * (1.0 / MOBA_BLOCK)
        gate = jnp.where(gcol == float(n), tot, gate)
        bmax = jnp.where(gcol == float(n), jnp.max(smax, axis=-1, keepdims=True), bmax)
    sel = _top_mask(gate, gcol, min(MOBA_TOPK, n_blocks))
    between()

    s_own = _dot_nt(qa, kn_ref[0].astype(BF16))
    orow = lax.broadcasted_iota(jnp.int32, s_own.shape, 0)
    ocol = lax.broadcasted_iota(jnp.int32, s_own.shape, 1)
    own_ok = jnp.logical_and(ocol % N_HEADS == orow % N_HEADS, ocol // N_HEADS <= orow // N_HEADS)
    s_own = jnp.where(own_ok, s_own, NEG)

    m = jnp.maximum(jnp.max(s_own, axis=-1, keepdims=True),
                    jnp.max(jnp.where(sel > 0.5, bmax, NEG), axis=-1, keepdims=True))
    p_own = jnp.exp2(s_own - m)
    l = jnp.sum(p_own, axis=-1, keepdims=True)
    acc = _dot(p_own.astype(BF16), vn_ref[0].astype(BF16))
    lsum = jnp.zeros((rows, pw), F32)
    for p_i in range(n_pages):
        n = p_i // per_blk
        sp = s_ref[:, p_i * pw:(p_i + 1) * pw]
        keep = jnp.logical_and(same_head, sel[:, n:n + 1] > 0.5)
        pp = jnp.where(keep, jnp.exp2(sp - m), 0.0)
        lsum = lsum + pp
        acc = acc + _dot(pp.astype(BF16), v_pages[p_i][...].astype(BF16))
    l = l + jnp.sum(lsum, axis=-1, keepdims=True)
    o_ref[0] = acc * (1.0 / l)


def _moba_kernel(pt_ref, q_ref, k_ref, vt_ref, km_ref, qs_ref, kn_ref, vn_ref, ck_ref, cv_ref, o_ref, os_ref,
                 qsc_ref, bias_ref, m_ref, acc_ref, s_ref, mb_ref, sc_ref, kbuf_ref, vbuf_ref, page_sem,
                 *, nb, hp, grp, n_pages, t_new):
    i = pl.program_id(0)
    own = i % nb
    slot = i % 2

    def fetch(row, sl):
        for p in range(n_pages):
            pg = pt_ref[row, p]
            pltpu.make_async_copy(ck_ref.at[pg], kbuf_ref.at[sl, p], page_sem.at[sl]).start()
            pltpu.make_async_copy(cv_ref.at[pg], vbuf_ref.at[sl, p], page_sem.at[sl]).start()

    @pl.when(i == 0)
    def _():
        fetch(0, 0)

    @pl.when(i + 1 < pl.num_programs(0))
    def _():
        fetch(i + 1, 1 - slot)

    for src, buf in ((ck_ref, kbuf_ref), (cv_ref, vbuf_ref)):
        pltpu.make_async_copy(src.at[pl.ds(0, n_pages)], buf.at[slot], page_sem.at[slot]).wait()
    k_pages = [kbuf_ref.at[slot, p] for p in range(n_pages)]
    v_pages = [vbuf_ref.at[slot, p] for p in range(n_pages)]

    def prompt_prologue():
        _moba_prompt_prologue(own, q_ref, k_ref, vt_ref, km_ref, qsc_ref, bias_ref, m_ref, acc_ref,
                              nb=nb, hp=hp)

    _moba_sample_body(qs_ref, kn_ref, vn_ref, k_pages, v_pages, os_ref, sc_ref, t_new=t_new,
                      between=prompt_prologue)
    _moba_prompt_main(own, k_ref, vt_ref, o_ref, qsc_ref, bias_ref, m_ref, acc_ref, s_ref, mb_ref,
                      nb=nb, hp=hp, grp=grp)


def moba_attention(q, k_bf, vt, kmean, q_s, k_new, v_new, cache_k, cache_v, page_table):
    s = q.shape[0]
    nb = s // MOBA_BLOCK
    hp = ATT_HEADS
    grp = ATT_GROUP if nb % ATT_GROUP == 0 else 1
    wide = hp * HEAD_DIM
    db, t_new, _ = q_s.shape
    th = t_new * N_HEADS
    q_s, k_new, v_new = (a.reshape(db, th, HEAD_DIM) for a in (q_s, k_new, v_new))
    assert db == (N_HEADS // hp) * nb, "one sample row per prompt tile step"
    n_pages = page_table.shape[1]
    page = cache_k.shape[-3]
    ck = cache_k.reshape(-1, page * N_HEADS, HEAD_DIM)
    cv = cache_v.reshape(-1, page * N_HEADS, HEAD_DIM)
    tok = pl.BlockSpec((1, th, HEAD_DIM), lambda i, pt: (i, 0, 0))
    tile = pl.BlockSpec((MOBA_BLOCK, wide), lambda i, pt: (i % nb, i // nb))

    page_rows = page * N_HEADS
    grid_spec = pltpu.PrefetchScalarGridSpec(
        num_scalar_prefetch=1,
        grid=(db,),
        in_specs=[
            tile,
            pl.BlockSpec((s, wide), lambda i, pt: (0, i // nb)),
            pl.BlockSpec((nb, hp * VT_ROWS, MOBA_BLOCK), lambda i, pt: (0, i // nb, 0)),
            pl.BlockSpec((nb, wide), lambda i, pt: (0, i // nb)),
            tok, tok, tok,
            pl.BlockSpec(memory_space=pl.ANY), pl.BlockSpec(memory_space=pl.ANY),
        ],
        out_specs=(tile, tok),
        scratch_shapes=_prompt_scratch(nb, hp, grp) + [
            pltpu.VMEM((th, n_pages * page_rows), F32),
            pltpu.VMEM((2, n_pages, page_rows, HEAD_DIM), F32),
            pltpu.VMEM((2, n_pages, page_rows, HEAD_DIM), F32),
            pltpu.SemaphoreType.DMA((2,)),
        ],
    )
    o_p, o_s = pl.pallas_call(
        functools.partial(_moba_kernel, nb=nb, hp=hp, grp=grp, n_pages=n_pages, t_new=t_new),
        grid_spec=grid_spec,
        out_shape=(jax.ShapeDtypeStruct((s, ATTN_WIDTH), BF16),
                   jax.ShapeDtypeStruct((db, th, HEAD_DIM), F32)),
        compiler_params=_cparams(("arbitrary",), ATTN_VMEM_LIMIT),
        name="moba_attention",
    )(page_table, q, k_bf, vt, kmean, q_s, k_new, v_new, ck, cv)
    return o_p, o_s.reshape(db, t_new, ATTN_WIDTH)


def _pool_sample_kernel(u_ref, st_ref, d_ref, new_ref, *, t_new, start_pos):
    ext = [st_ref[:, r, :] for r in range(POOL_STATE)] + [u_ref[:, t, :] for t in range(t_new)]
    for t in range(t_new):
        e = POOL_STATE + t
        parts = []
        for g, w in enumerate(POOL_WINDOWS):
            c0, c1 = g * POOL_GROUP_W, (g + 1) * POOL_GROUP_W
            wsum = ext[e][:, c0:c1]
            for jj in range(1, w):
                wsum = wsum + ext[e - jj][:, c0:c1]
            count = min(float(w), float(start_pos + t) + 1.0)
            parts.append(wsum * (1.0 / count) - ext[e][:, c0:c1])
        d_ref[:, t, :] = jnp.concatenate(parts, axis=-1).astype(d_ref.dtype)
    for r in range(POOL_STATE):
        new_ref[:, r, :] = ext[t_new + r]


def pool_sample(u, state, *, start_pos, bb=32):
    db, t_new, _ = u.shape
    return pl.pallas_call(
        functools.partial(_pool_sample_kernel, t_new=t_new, start_pos=start_pos),
        grid=(db // bb,),
        in_specs=[
            pl.BlockSpec((bb, t_new, POOL_WIDTH), lambda i: (i, 0, 0)),
            pl.BlockSpec((bb, POOL_STATE, POOL_WIDTH), lambda i: (i, 0, 0)),
        ],
        out_specs=(
            pl.BlockSpec((bb, t_new, POOL_WIDTH), lambda i: (i, 0, 0)),
            pl.BlockSpec((bb, POOL_STATE, POOL_WIDTH), lambda i: (i, 0, 0)),
        ),
        out_shape=(
            jax.ShapeDtypeStruct((db, t_new, POOL_WIDTH), F32),
            jax.ShapeDtypeStruct((db, POOL_STATE, POOL_WIDTH), F32),
        ),
        compiler_params=_cparams(("arbitrary",)),
        name="pool_sample",
    )(u, state)


def _out_proj_kernel(x_ref, oa_ref, d_ref, wp_ref, ps_ref, wo_ref, gf_ref, wr_ref, br_ref,
                     h2_ref, rt_ref, mix_ref):
    mix_ref[:, 0:ATTN_WIDTH] = oa_ref[...].astype(BF16)
    dd = d_ref[...].astype(BF16)
    for g in range(len(POOL_WINDOWS)):
        c0, c1 = g * POOL_GROUP_W, (g + 1) * POOL_GROUP_W
        yg = _dot(dd[:, c0:c1], wp_ref[g].astype(BF16)) * ps_ref[:, c0:c1]
        mix_ref[:, ATTN_WIDTH + c0:ATTN_WIDTH + c1] = yg.astype(BF16)
    h2 = x_ref[...] + _dot(mix_ref[...], wo_ref[...])
    h2_ref[...] = h2
    ms = jnp.mean(h2 * h2, axis=-1, keepdims=True)
    hn = h2 * lax.rsqrt(ms + EPS) * gf_ref[...]

    hn_hi = hn.astype(BF16)
    hn_lo = (hn - hn_hi.astype(F32)).astype(BF16)
    t = _dot(hn_hi, wr_ref[...])
    logits = t[:, :LANES] + t[:, LANES:] + _dot(hn_lo, wr_ref[:, :LANES]) + br_ref[...]
    lane = lax.broadcasted_iota(jnp.int32, logits.shape, 1).astype(F32)
    far = jnp.float32(LANES)
    is_g = lane < N_EXPERT_GROUPS
    gl = jnp.where(is_g, logits, -jnp.inf)
    g_max = jnp.max(gl, axis=-1, keepdims=True)
    g_top = jnp.min(jnp.where(gl == g_max, lane, far), axis=-1, keepdims=True)
    g_p = 1.0 / jnp.sum(jnp.where(is_g, jnp.exp(gl - g_max), 0.0), axis=-1, keepdims=True)
    lo = N_EXPERT_GROUPS + g_top * EXPERTS_PER_GROUP
    in_grp = jnp.logical_and(lane >= lo, lane < lo + EXPERTS_PER_GROUP)
    el = jnp.where(in_grp, logits, -jnp.inf)
    e1 = jnp.max(el, axis=-1, keepdims=True)
    i1 = jnp.min(jnp.where(el == e1, lane, far), axis=-1, keepdims=True)
    el2 = jnp.where(lane == i1, -jnp.inf, el)
    e2 = jnp.max(el2, axis=-1, keepdims=True)
    i2 = jnp.min(jnp.where(el2 == e2, lane, far), axis=-1, keepdims=True)
    ex2 = jnp.exp(e2 - e1)
    den = 1.0 + ex2
    w1 = (1.0 / den) * g_p
    w2 = (ex2 / den) * g_p
    rt_ref[...] = jnp.where(lane == 0.0, i1 - N_EXPERT_GROUPS,
                            jnp.where(lane == 1.0, i2 - N_EXPERT_GROUPS,
                                      jnp.where(lane == 2.0, w1, jnp.where(lane == 3.0, w2, 0.0))))


def out_proj(x, o_attn, d, w_pool, pool_scale, w_out_bf, g_ffn, w_router, b_router, *, tm):
    n = x.shape[0]
    row = lambda i: (i, 0)
    const = lambda i: (0, 0)
    return pl.pallas_call(
        _out_proj_kernel,
        grid=(n // tm,),
        in_specs=[
            pl.BlockSpec((tm, D_MODEL), row),
            pl.BlockSpec((tm, ATTN_WIDTH), row),
            pl.BlockSpec((tm, POOL_WIDTH), row),
            pl.BlockSpec(w_pool.shape, lambda i: (0, 0, 0)),
            pl.BlockSpec((1, POOL_WIDTH), const),
            pl.BlockSpec((D_MODEL, D_MODEL), const),
            pl.BlockSpec((1, D_MODEL), const),
            pl.BlockSpec((D_MODEL, 2 * LANES), const),
            pl.BlockSpec((1, LANES), const),
        ],
        out_specs=(
            pl.BlockSpec((tm, D_MODEL), row),
            pl.BlockSpec((tm, LANES), row),
        ),
        out_shape=(
            jax.ShapeDtypeStruct((n, D_MODEL), F32),
            jax.ShapeDtypeStruct((n, LANES), F32),
        ),
        scratch_shapes=[pltpu.VMEM((tm, D_MODEL), BF16)],
        compiler_params=_cparams(("arbitrary",)),
        name="out_proj",
    )(x, o_attn, d, w_pool, pool_scale, w_out_bf, g_ffn, w_router, b_router)


MOE_TILE = 256
ROW_TILE = 256


def _route_kernel(rt_ref, pos_ref, meta_ref, *, n_tiles):
    lane = lax.broadcasted_iota(jnp.int32, (ROW_TILE, LANES), 1).astype(F32)
    r_i = lax.broadcasted_iota(jnp.int32, (ROW_TILE, ROW_TILE), 0)
    c_i = lax.broadcasted_iota(jnp.int32, (ROW_TILE, ROW_TILE), 1)
    tri = jnp.where(c_i < r_i, 1.0, 0.0).astype(BF16)

    def one_hot(t):
        rt = rt_ref[pl.ds(pl.multiple_of(t * ROW_TILE, ROW_TILE), ROW_TILE), :]
        e1, e2 = rt[:, 0:1], rt[:, 1:2]
        return e1, e2, jnp.where(jnp.logical_or(lane == e1, lane == e2), 1.0, 0.0)

    def count(t, cnt):
        return cnt + jnp.sum(one_hot(t)[2], axis=0, keepdims=True)

    cnt = lax.fori_loop(0, n_tiles, count, jnp.zeros((1, LANES), F32))
    tiles_per = jnp.floor((cnt + (MOE_TILE - 1)) * (1.0 / MOE_TILE))
    e_r = lax.broadcasted_iota(jnp.int32, (LANES, LANES), 0)
    e_c = lax.broadcasted_iota(jnp.int32, (LANES, LANES), 1)
    upper = jnp.where(e_r < e_c, 1.0, 0.0).astype(BF16)
    off_tiles = _dot(jnp.broadcast_to(tiles_per, (8, LANES)).astype(BF16), upper)[0:1]
    base = off_tiles * MOE_TILE

    def place(t, run):
        e1, e2, oh = one_hot(t)
        dest = base + run + _dot(tri, oh.astype(BF16))
        p1 = jnp.sum(jnp.where(lane == e1, dest, 0.0), axis=1, keepdims=True)
        p2 = jnp.sum(jnp.where(lane == e2, dest, 0.0), axis=1, keepdims=True)
        pos = jnp.where(lane == 0.0, p1, jnp.where(lane == 1.0, p2, 0.0))
        pos_ref[pl.ds(pl.multiple_of(t * ROW_TILE, ROW_TILE), ROW_TILE), :] = pos.astype(jnp.int32)
        return run + jnp.sum(oh, axis=0, keepdims=True)

    lax.fori_loop(0, n_tiles, place, jnp.zeros((1, LANES), F32))
    row = lax.broadcasted_iota(jnp.int32, (8, LANES), 0)
    meta_ref[...] = jnp.where(row == 0, tiles_per, jnp.where(row == 1, off_tiles, 0.0))


def route(rt):
    n = rt.shape[0]
    assert n % ROW_TILE == 0
    return pl.pallas_call(
        functools.partial(_route_kernel, n_tiles=n // ROW_TILE),
        out_shape=(
            jax.ShapeDtypeStruct((n, LANES), jnp.int32),
            jax.ShapeDtypeStruct((8, LANES), F32),
        ),
        compiler_params=pltpu.CompilerParams(vmem_limit_bytes=VMEM_LIMIT),
        name="route",
    )(rt)


ROW_UNROLL = 8


def _dispatch_kernel(p1_ref, p2_ref, zt_ref, xa_ref, xb_ref, xs_ref, zero_ref, sem, zsem, *, tm, tiles_a, n_zero):
    i = pl.program_id(0)

    @pl.when(i == 0)
    def _():
        zero_ref[...] = jnp.zeros(zero_ref.shape, zero_ref.dtype)

        def zcopy(z):
            row0 = pl.multiple_of(zt_ref[z] * MOE_TILE, MOE_TILE)
            return pltpu.make_async_copy(zero_ref, xs_ref.at[pl.ds(row0, MOE_TILE)], zsem)

        for z in range(n_zero):
            @pl.when(zt_ref[z] >= 0)
            def _():
                zcopy(z).start()
        for z in range(n_zero):
            @pl.when(zt_ref[z] >= 0)
            def _():
                zcopy(z).wait()

    def scatter(x_ref):
        t0 = i * tm

        def issue(r, c):
            src = x_ref.at[pl.ds(r, 1)]
            pltpu.make_async_copy(src, xs_ref.at[pl.ds(p1_ref[t0 + r], 1)], sem).start()
            pltpu.make_async_copy(src, xs_ref.at[pl.ds(p2_ref[t0 + r], 1)], sem).start()
            return c

        lax.fori_loop(0, tm, issue, 0, unroll=ROW_UNROLL)
        for _ in range(2):
            pltpu.make_async_copy(x_ref, xs_ref.at[pl.ds(0, tm)], sem).wait()

    @pl.when(i < tiles_a)
    def _():
        scatter(xa_ref)

    @pl.when(i >= tiles_a)
    def _():
        scatter(xb_ref)


def dispatch(p1, p2, zero_tiles, xa, xb, rows, *, tm):
    tiles_a, tiles_b = xa.shape[0] // tm, xb.shape[0] // tm
    grid_spec = pltpu.PrefetchScalarGridSpec(
        num_scalar_prefetch=3,
        grid=(tiles_a + tiles_b,),
        in_specs=[
            pl.BlockSpec((tm, D_MODEL), lambda i, a, b, z: (jnp.minimum(i, tiles_a - 1), 0)),
            pl.BlockSpec((tm, D_MODEL), lambda i, a, b, z: (jnp.maximum(i - tiles_a, 0), 0)),
        ],
        out_specs=pl.BlockSpec(memory_space=pl.ANY),
        scratch_shapes=[pltpu.VMEM((MOE_TILE, D_MODEL), F32), pltpu.SemaphoreType.DMA(()), pltpu.SemaphoreType.DMA(())],
    )
    return pl.pallas_call(
        functools.partial(_dispatch_kernel, tm=tm, tiles_a=tiles_a, n_zero=zero_tiles.shape[0]),
        grid_spec=grid_spec,
        out_shape=jax.ShapeDtypeStruct((rows, D_MODEL), F32),
        compiler_params=_cparams(("arbitrary",)),
        name="dispatch",
    )(p1, p2, zero_tiles, xa, xb)


def _experts_kernel(te_ref, ts_ref, tf_ref, sl_ref, nx_ref, nu_ref, x_ref, gf_ref, wg_hbm, wu_hbm, wd_hbm, o_ref,
                    wg_buf, wu_buf, wd_buf, wsem, *, layer):
    j = pl.program_id(0)
    used = j < nu_ref[0]
    first = jnp.logical_and(used, tf_ref[j] == 1)
    slot = sl_ref[j]

    def copies(e, sl):
        return (pltpu.make_async_copy(wg_hbm.at[layer, e], wg_buf.at[sl], wsem.at[sl]),
                pltpu.make_async_copy(wu_hbm.at[layer, e], wu_buf.at[sl], wsem.at[sl]),
                pltpu.make_async_copy(wd_hbm.at[layer, e], wd_buf.at[sl], wsem.at[sl]))

    @pl.when(j == 0)
    def _():
        for c in copies(te_ref[0], 0):
            c.start()

    @pl.when(jnp.logical_and(first, nx_ref[j] >= 0))
    def _():
        for c in copies(nx_ref[j], 1 - slot):
            c.start()

    @pl.when(first)
    def _():
        for c in copies(te_ref[j], slot):
            c.wait()

    @pl.when(used)
    def _():
        h = x_ref[...]
        ms = jnp.mean(h * h, axis=-1, keepdims=True)
        x = (h * lax.rsqrt(ms + EPS) * gf_ref[...]).astype(BF16).astype(F32)
        a = _dot(x, wg_buf[slot])
        b = _dot(x, wu_buf[slot])
        act = (a * (1.0 / (1.0 + jnp.exp(-a)))) * b
        o_ref[...] = _dot(act.astype(BF16).astype(F32), wd_buf[slot])

    @pl.when(jnp.logical_not(used))
    def _():
        o_ref[...] = jnp.zeros(o_ref.shape, o_ref.dtype)


def experts(tile_expert, tile_src, tile_first, tile_slot, tile_next, n_used, xs, g_ffn, w_gate, w_up, w_down, *, layer):
    rows, w = xs.shape
    idx = lambda f: (lambda j, te, ts, tf, sl, nx, nu: f(j, ts))
    grid_spec = pltpu.PrefetchScalarGridSpec(
        num_scalar_prefetch=6,
        grid=(rows // MOE_TILE,),
        in_specs=[
            pl.BlockSpec((MOE_TILE, w), idx(lambda j, ts: (ts[j], 0))),
            pl.BlockSpec((1, D_MODEL), idx(lambda j, ts: (0, 0))),
            pl.BlockSpec(memory_space=pl.ANY), pl.BlockSpec(memory_space=pl.ANY), pl.BlockSpec(memory_space=pl.ANY),
        ],
        out_specs=pl.BlockSpec((MOE_TILE, D_MODEL), idx(lambda j, ts: (j, 0))),
        scratch_shapes=[
            pltpu.VMEM((2, D_MODEL, D_EXPERT), F32), pltpu.VMEM((2, D_MODEL, D_EXPERT), F32),
            pltpu.VMEM((2, D_EXPERT, D_MODEL), F32), pltpu.SemaphoreType.DMA((2,)),
        ],
    )
    return pl.pallas_call(
        functools.partial(_experts_kernel, layer=layer),
        grid_spec=grid_spec,
        out_shape=jax.ShapeDtypeStruct((rows, D_MODEL), F32),
        compiler_params=_cparams(("arbitrary",), EXPERT_VMEM_LIMIT),
        name="experts",
    )(tile_expert, tile_src, tile_first, tile_slot, tile_next, n_used, xs, g_ffn, w_gate, w_up, w_down)


def _combine_kernel(p1_ref, p2_ref, h2_ref, rt_ref, os_ref, y_ref, buf_ref, sem, *, tm):
    i = pl.program_id(0)
    slot = i % 2

    def gather(step, sl):
        t0 = step * tm

        def issue(r, c):
            pltpu.make_async_copy(os_ref.at[pl.ds(p1_ref[t0 + r], 1)], buf_ref.at[sl, 0, pl.ds(r, 1)],
                                  sem.at[sl]).start()
            pltpu.make_async_copy(os_ref.at[pl.ds(p2_ref[t0 + r], 1)], buf_ref.at[sl, 1, pl.ds(r, 1)],
                                  sem.at[sl]).start()
            return c

        lax.fori_loop(0, tm, issue, 0, unroll=ROW_UNROLL)

    @pl.when(i == 0)
    def _():
        gather(0, 0)

    @pl.when(i + 1 < pl.num_programs(0))
    def _():
        gather(i + 1, 1 - slot)

    for s in range(2):
        pltpu.make_async_copy(os_ref.at[pl.ds(0, tm)], buf_ref.at[slot, s], sem.at[slot]).wait()
    rt = rt_ref[...]
    y_ref[...] = h2_ref[...] + rt[:, 2:3] * buf_ref[slot, 0] + rt[:, 3:4] * buf_ref[slot, 1]


def combine(p1, p2, h2, rt, os, *, tm):
    n = h2.shape[0]
    row = lambda i, a, b: (i, 0)
    grid_spec = pltpu.PrefetchScalarGridSpec(
        num_scalar_prefetch=2,
        grid=(n // tm,),
        in_specs=[pl.BlockSpec((tm, D_MODEL), row), pl.BlockSpec((tm, LANES), row),
                  pl.BlockSpec(memory_space=pl.ANY)],
        out_specs=pl.BlockSpec((tm, D_MODEL), row),
        scratch_shapes=[pltpu.VMEM((2, 2, tm, D_MODEL), F32), pltpu.SemaphoreType.DMA((2,))],
    )
    return pl.pallas_call(
        functools.partial(_combine_kernel, tm=tm),
        grid_spec=grid_spec,
        out_shape=jax.ShapeDtypeStruct((n, D_MODEL), F32),
        compiler_params=_cparams(("arbitrary",)),
        name="combine",
    )(p1, p2, h2, rt, os)


def kernel(x_prompt, x_sample, cache_k, cache_v, state_pool, page_table, g_mix, w_in, g_q, g_k, w_pool, pool_scale, w_out, g_ffn, w_group_router, b_group_router, w_expert_router, b_expert_router, w_gate, w_up, w_down):
    B, S, _ = x_prompt.shape
    DB, T, _ = x_sample.shape
    depth = w_in.shape[0]
    assert B == 1 and depth == 1
    past_len = page_table.shape[1] * cache_k.shape[2]
    l = 0

    w_in_bf = w_in[l].astype(BF16)
    w_out_bf = w_out[l].astype(BF16)
    gm, gq, gk, gf = g_mix[l][None], g_q[l][None], g_k[l][None], g_ffn[l][None]
    ps = pool_scale[l][None]
    n_r = N_EXPERT_GROUPS + N_EXPERTS
    w_router = jnp.concatenate([w_group_router[l], w_expert_router[l].reshape(D_MODEL, N_EXPERTS)], axis=1)
    w_router = jnp.pad(w_router, ((0, 0), (0, LANES - n_r)))
    w_router_hi = w_router.astype(BF16)
    w_router = jnp.concatenate([w_router_hi, (w_router - w_router_hi.astype(F32)).astype(BF16)], axis=1)
    b_router = jnp.concatenate([b_group_router[l], b_expert_router[l].reshape(N_EXPERTS)])
    b_router = jnp.pad(b_router, (0, LANES - n_r))[None]

    def mixer_tail(x2d, o_attn, d):
        return out_proj(x2d, o_attn, d, w_pool[l], ps, w_out_bf, gf, w_router, b_router, tm=PROJ_TILE)

    xp = x_prompt.reshape(S, D_MODEL)
    n_s = DB * T
    xs = x_sample.reshape(n_s, D_MODEL)
    q_p, k_p, kb_p, km_p, v_p, _, u_p, d_p, vt_p = in_proj(xp, gm, w_in_bf, gq, gk, tm=PROJ_TILE, with_pool=True)
    q_s, k_s, _, _, v_s, _, u_s = in_proj(xs, gm, w_in_bf, gq, gk, tm=n_s, with_pool=False)
    r3 = lambda a: a.reshape(DB, T, ATTN_WIDTH)
    o_p, o_s = moba_attention(q_p, kb_p, vt_p, km_p.reshape(S // MOBA_BLOCK, ATTN_WIDTH),
                              r3(q_s), r3(k_s), r3(v_s), cache_k, cache_v, page_table + l * cache_k.shape[1])
    d_s, pool_s = pool_sample(u_s.reshape(DB, T, POOL_WIDTH), state_pool[l], start_pos=past_len)
    h2_p, rt_p = mixer_tail(xp, o_p, d_p)
    h2_s, rt_s = mixer_tail(xs, o_s.reshape(n_s, ATTN_WIDTH), d_s.reshape(n_s, POOL_WIDTH))

    pos, meta = route(jnp.concatenate([rt_p, rt_s], axis=0))
    p1, p2 = pos[:, 0], pos[:, 1]
    max_tiles = -(-2 * (S + n_s) // MOE_TILE) + N_EXPERTS
    tiles_per = meta[0, :N_EXPERTS].astype(jnp.int32)
    ends = tiles_per + meta[1, :N_EXPERTS].astype(jnp.int32)
    n_used = ends[N_EXPERTS - 1]
    tile_src = jnp.minimum(jnp.arange(max_tiles, dtype=jnp.int32), n_used - 1)
    tile_expert = jnp.minimum(jnp.sum(tile_src[:, None] >= ends[None, :], axis=1), N_EXPERTS - 1).astype(jnp.int32)
    tail = n_used + jnp.arange(N_EXPERTS, dtype=jnp.int32)
    zero_tiles = jnp.concatenate([jnp.where(tiles_per > 0, ends - 1, -1), jnp.where(tail < max_tiles, tail, -1)])
    x_sorted = dispatch(p1, p2, zero_tiles, h2_p, h2_s, max_tiles * MOE_TILE, tm=ROW_TILE)
    tile_ids = jnp.arange(max_tiles, dtype=jnp.int32)
    tile_first = jnp.logical_and(tile_ids < n_used,
                                 jnp.logical_or(tile_ids == 0, tile_expert != jnp.roll(tile_expert, 1))).astype(jnp.int32)
    tile_slot = ((jnp.cumsum(tile_first) - 1) % 2).astype(jnp.int32)
    e_ids = jnp.arange(N_EXPERTS, dtype=jnp.int32)
    later = jnp.logical_and(e_ids[None, :] > e_ids[:, None], tiles_per[None, :] > 0)
    next_nonempty = jnp.min(jnp.where(later, e_ids[None, :], N_EXPERTS), axis=1)
    tile_next = jnp.where(next_nonempty < N_EXPERTS, next_nonempty, -1)[tile_expert].astype(jnp.int32)
    o_sorted = experts(tile_expert, tile_src, tile_first, tile_slot, tile_next, n_used[None], x_sorted, gf,
                       w_gate, w_up, w_down, layer=l)
    y_prompt = combine(p1[:S], p2[:S], h2_p, rt_p, o_sorted, tm=ROW_TILE).reshape(B, S, D_MODEL)
    y_sample = combine(p1[S:], p2[S:], h2_s, rt_s, o_sorted, tm=ROW_TILE).reshape(DB, T, D_MODEL)

    hd = (N_HEADS, HEAD_DIM)
    return (
        y_prompt,
        y_sample,
        k_p.reshape(1, B, S, *hd),
        v_p.reshape(1, B, S, *hd),
        u_p[S - POOL_STATE:].reshape(1, B, POOL_STATE, POOL_WIDTH),
        k_s.reshape(1, DB, T, *hd),
        v_s.reshape(1, DB, T, *hd),
        pool_s.reshape(1, DB, POOL_STATE, POOL_WIDTH),
    )
```

```python
import functools

import jax
import jax.numpy as jnp
from jax import lax
from jax.experimental import pallas as pl
from jax.experimental.pallas import tpu as pltpu

D_MODEL = 2048
ATTN_WIDTH = 1024
POOL_WIDTH = 1024
HEAD_DIM = 128
N_HEADS = 8
POOL_WINDOWS = (2, 4, 8, 16)
POOL_GROUP_W = 256
POOL_STATE = 15
MOBA_BLOCK = 256
MOBA_TOPK = 3
N_EXPERT_GROUPS = 4
EXPERTS_PER_GROUP = 4
N_EXPERTS = 16
D_EXPERT = 768
EPS = 1e-6
NEG = -1e30
LANES = 128
VMEM_LIMIT = 56 * 1024 * 1024
PROJ_TILE = 512
ATTN_VMEM_LIMIT = 60 * 1024 * 1024
EXPERT_VMEM_LIMIT = 60 * 1024 * 1024

BF16 = jnp.bfloat16
F32 = jnp.float32


def _cparams(sem, vmem_limit=VMEM_LIMIT):
    return pltpu.CompilerParams(dimension_semantics=sem, vmem_limit_bytes=vmem_limit)


def _dot(a, b):
    return jnp.dot(a, b, preferred_element_type=F32)


def _dot_nt(a, b, precision=None):
    return lax.dot_general(a, b, (((1,), (1,)), ((), ())), precision=precision,
                           preferred_element_type=F32)


def _top_mask(g, ids, k, axis=-1):
    sel = jnp.zeros(g.shape, F32)
    for _ in range(k):
        m = jnp.max(g, axis=axis, keepdims=True)
        idx = jnp.min(jnp.where(g == m, ids, jnp.float32(g.shape[axis])), axis=axis, keepdims=True)
        pick = ids == idx
        sel = jnp.where(pick, 1.0, sel)
        g = jnp.where(pick, -jnp.inf, g)
    return sel


def _in_proj_kernel(x_ref, gmix_ref, w_ref, gq_ref, gk_ref,
                    q_ref, k_ref, kb_ref, km_ref, v_ref, u_ref, *rest, tm, with_pool):
    if with_pool:
        d_ref, vt_ref, hb_ref, ext_ref = rest
    else:
        (hb_ref,) = rest
    xf = x_ref[...]
    ms = jnp.mean(xf * xf, axis=-1, keepdims=True)
    hb_ref[...] = (xf * lax.rsqrt(ms + EPS) * gmix_ref[...]).astype(BF16)

    def head_norm(z, out_ref, g_ref):
        for h in range(N_HEADS):
            zh = z[:, h * HEAD_DIM:(h + 1) * HEAD_DIM]
            ms = jnp.mean(zh * zh, axis=-1, keepdims=True)
            out_ref[:, h * HEAD_DIM:(h + 1) * HEAD_DIM] = zh * lax.rsqrt(ms + EPS) * g_ref[...]

    head_norm(_dot(hb_ref[...], w_ref[:, 0:ATTN_WIDTH]), q_ref, gq_ref)
    head_norm(_dot(hb_ref[...], w_ref[:, ATTN_WIDTH:2 * ATTN_WIDTH]), k_ref, gk_ref)
    kb_ref[...] = k_ref[...].astype(BF16)
    for g in range(tm // MOBA_BLOCK):
        blk = k_ref[g * MOBA_BLOCK:(g + 1) * MOBA_BLOCK, :]
        km_ref[g] = jnp.mean(blk, axis=0, keepdims=True)
    zv = _dot(hb_ref[...], w_ref[:, 2 * ATTN_WIDTH:3 * ATTN_WIDTH])
    v_ref[...] = zv
    if with_pool:
        for g in range(tm // MOBA_BLOCK):
            for h in range(N_HEADS):
                blk = zv[g * MOBA_BLOCK:(g + 1) * MOBA_BLOCK, h * HEAD_DIM:(h + 1) * HEAD_DIM]
                vt_ref[g, h * VT_ROWS:h * VT_ROWS + HEAD_DIM, :] = blk.T.astype(BF16)
                vt_ref[g, h * VT_ROWS + HEAD_DIM:(h + 1) * VT_ROWS, :] = jnp.ones(
                    (VT_ROWS - HEAD_DIM, MOBA_BLOCK), BF16)
    zu = _dot(hb_ref[...], w_ref[:, 3 * ATTN_WIDTH:])
    u_ref[...] = zu
    if with_pool:
        _pool_differences(zu, d_ref, ext_ref, tm=tm)


POOL_PAD = 16


def _pool_differences(u, d_ref, ext_ref, *, tm):
    i = pl.program_id(0)

    @pl.when(i == 0)
    def _():
        ext_ref[0:POOL_PAD, :] = jnp.zeros((POOL_PAD, POOL_WIDTH), F32)

    ext_ref[POOL_PAD:POOL_PAD + tm, :] = u
    pos = (i * tm + lax.broadcasted_iota(jnp.int32, (tm, 1), 0)).astype(F32)
    for g, w in enumerate(POOL_WINDOWS):
        c0, c1 = g * POOL_GROUP_W, (g + 1) * POOL_GROUP_W
        wsum = u[:, c0:c1]
        for jj in range(1, w):
            wsum = wsum + ext_ref[POOL_PAD - jj:POOL_PAD - jj + tm, c0:c1]
        inv = 1.0 / jnp.minimum(jnp.float32(w), pos + 1.0)
        d_ref[:, c0:c1] = (wsum * inv - u[:, c0:c1]).astype(d_ref.dtype)
    ext_ref[0:POOL_PAD, :] = ext_ref[tm:tm + POOL_PAD, :]


def in_proj(x, g_mix, w_in_bf, g_q, g_k, *, tm, with_pool):
    n = x.shape[0]
    wide = lambda i: (i, 0)
    const = lambda i: (0, 0)
    out_shape = (
        jax.ShapeDtypeStruct((n, ATTN_WIDTH), F32),
        jax.ShapeDtypeStruct((n, ATTN_WIDTH), F32),
        jax.ShapeDtypeStruct((n, ATTN_WIDTH), BF16),
        jax.ShapeDtypeStruct((n // MOBA_BLOCK, 1, ATTN_WIDTH), F32),
        jax.ShapeDtypeStruct((n, ATTN_WIDTH), F32),
        jax.ShapeDtypeStruct((n, POOL_WIDTH), F32),
    )
    blk = pl.BlockSpec((tm, ATTN_WIDTH), wide)
    vt_shape = (n // MOBA_BLOCK, N_HEADS * VT_ROWS, MOBA_BLOCK)
    pool_out = (jax.ShapeDtypeStruct((n, POOL_WIDTH), BF16), jax.ShapeDtypeStruct(vt_shape, BF16)) if with_pool else ()
    pool_specs = (blk, pl.BlockSpec((tm // MOBA_BLOCK,) + vt_shape[1:], lambda i: (i, 0, 0))) if with_pool else ()
    pool_scratch = [pltpu.VMEM((POOL_PAD + tm, POOL_WIDTH), F32)] if with_pool else []
    return pl.pallas_call(
        functools.partial(_in_proj_kernel, tm=tm, with_pool=with_pool),
        grid=(n // tm,),
        in_specs=[
            pl.BlockSpec((tm, D_MODEL), wide),
            pl.BlockSpec((1, D_MODEL), const),
            pl.BlockSpec(w_in_bf.shape, const, pipeline_mode=pl.Buffered(1)),
            pl.BlockSpec((1, HEAD_DIM), const),
            pl.BlockSpec((1, HEAD_DIM), const),
        ],
        out_specs=(blk, blk, blk,
                   pl.BlockSpec((tm // MOBA_BLOCK, 1, ATTN_WIDTH), lambda i: (i, 0, 0)),
                   blk, blk) + pool_specs,
        out_shape=out_shape + pool_out,
        scratch_shapes=[pltpu.VMEM((tm, D_MODEL), BF16)] + pool_scratch,
        compiler_params=_cparams(("arbitrary",)),
        name="in_proj",
    )(x, g_mix, w_in_bf, g_q, g_k)


LOG2E = 1.4426950408889634
ATT_HEADS = 2
ATT_GROUP = 4
VT_ROWS = HEAD_DIM + 16


def _moba_prompt_prologue(own, q_ref, k_ref, vt_ref, km_ref, qs_ref, bias_ref, m_ref, acc_ref, *, nb, hp):
    base = pl.multiple_of(own * MOBA_BLOCK, MOBA_BLOCK)
    blk_id = lax.broadcasted_iota(jnp.int32, (nb, MOBA_BLOCK), 0).astype(F32)
    valid = blk_id < own.astype(F32)
    key_i = lax.broadcasted_iota(jnp.int32, (MOBA_BLOCK, MOBA_BLOCK), 0)
    qry_i = lax.broadcasted_iota(jnp.int32, (MOBA_BLOCK, MOBA_BLOCK), 1)

    for hh in range(hp):
        cols = slice(hh * HEAD_DIM, (hh + 1) * HEAD_DIM)
        q = q_ref[:, cols]
        gate = _dot_nt(km_ref[:, cols], q, precision=lax.Precision.HIGHEST)
        sel = _top_mask(jnp.where(valid, gate, NEG), blk_id, MOBA_TOPK, axis=0)
        bias = jnp.where(jnp.logical_and(sel > 0.5, valid), 0.0, NEG)
        for n in range(nb):
            bias_ref[hh, n] = bias[n:n + 1, :]
        qs = (q * (HEAD_DIM ** -0.5 * LOG2E)).astype(BF16)
        qs_ref[hh] = qs
        s = _dot_nt(k_ref[pl.ds(base, MOBA_BLOCK), cols], qs)
        s = jnp.where(key_i <= qry_i, s, NEG)
        m0 = jnp.max(s, axis=0, keepdims=True)
        p = jnp.exp2(s - m0)
        m_ref[hh] = m0
        acc_ref[hh] = _dot(vt_ref[own, hh * VT_ROWS:(hh + 1) * VT_ROWS, :], p.astype(BF16))


def _moba_prompt_main(own, k_ref, vt_ref, o_ref, qs_ref, bias_ref, m_ref, acc_ref, s_ref, mb_ref,
                      *, nb, hp, grp):
    n_iter = (own + grp - 1) // grp

    def scores(i, slot):
        for hh in range(hp):
            cols = slice(hh * HEAD_DIM, (hh + 1) * HEAD_DIM)
            qs = qs_ref[hh]
            mb = None
            for gg in range(grp):
                n = jnp.minimum(i * grp + gg, nb - 1)
                off = pl.multiple_of(n * MOBA_BLOCK, MOBA_BLOCK)
                s = _dot_nt(k_ref[pl.ds(off, MOBA_BLOCK), cols], qs) + bias_ref[hh, n]
                s_ref[slot, hh, gg] = s
                smax = jnp.max(s, axis=0, keepdims=True)
                mb = smax if mb is None else jnp.maximum(mb, smax)
            mb_ref[slot, hh] = mb

    def softmax_pv(i, slot):
        for hh in range(hp):
            rows = slice(hh * VT_ROWS, (hh + 1) * VT_ROWS)
            m_prev = m_ref[hh]
            m_new = jnp.maximum(m_prev, mb_ref[slot, hh])
            acc = jnp.exp2(m_prev - m_new) * acc_ref[hh]
            for gg in range(grp):
                p = jnp.exp2(s_ref[slot, hh, gg] - m_new)
                acc = acc + _dot(vt_ref[i * grp + gg, rows, :], p.astype(BF16))
            m_ref[hh] = m_new
            acc_ref[hh] = acc

    scores(0, 0)

    def body(k, carry):
        scores(2 * k + 1, 1)
        softmax_pv(2 * k, 0)
        scores(2 * k + 2, 0)
        softmax_pv(2 * k + 1, 1)
        return carry

    lax.fori_loop(0, n_iter // 2, body, 0)

    @pl.when(n_iter % 2 == 1)
    def _():
        softmax_pv(n_iter - 1, 0)

    for hh in range(hp):
        acc = acc_ref[hh]
        o = acc[:HEAD_DIM] * (1.0 / acc[HEAD_DIM:HEAD_DIM + 1])
        o_ref[:, hh * HEAD_DIM:(hh + 1) * HEAD_DIM] = o.T.astype(o_ref.dtype)


def _prompt_scratch(nb, hp, grp):
    return [
        pltpu.VMEM((hp, MOBA_BLOCK, HEAD_DIM), BF16),
        pltpu.VMEM((hp, nb, 1, MOBA_BLOCK), F32),
        pltpu.VMEM((hp, 1, MOBA_BLOCK), F32),
        pltpu.VMEM((hp, VT_ROWS, MOBA_BLOCK), F32),
        pltpu.VMEM((2, hp, grp, MOBA_BLOCK, MOBA_BLOCK), F32),
        pltpu.VMEM((2, hp, 1, MOBA_BLOCK), F32),
    ]


def _moba_sample_body(q_ref, kn_ref, vn_ref, k_pages, v_pages, o_ref, s_ref, *, t_new, between):
    n_pages = len(k_pages)
    pw = k_pages[0].shape[0]
    page = pw // N_HEADS
    rows = t_new * N_HEADS
    per_blk = MOBA_BLOCK // page
    n_blocks = n_pages // per_blk

    qa = (q_ref[0] * (HEAD_DIM ** -0.5 * LOG2E)).astype(BF16)
    r_i = lax.broadcasted_iota(jnp.int32, (rows, pw), 0)
    c_i = lax.broadcasted_iota(jnp.int32, (rows, pw), 1)
    same_head = (c_i % N_HEADS) == (r_i % N_HEADS)

    gcol = lax.broadcasted_iota(jnp.int32, (rows, LANES), 1).astype(F32)
    gate = jnp.full((rows, LANES), NEG, F32)
    bmax = jnp.full((rows, LANES), NEG, F32)
    for n in range(n_blocks):
        ssum = jnp.zeros((rows, pw), F32)
        smax = jnp.full((rows, pw), NEG, F32)
        for pp in range(per_blk):
            p_i = n * per_blk + pp
            sp = _dot_nt(qa, k_pages[p_i][...].astype(BF16))
            s_ref[:, p_i * pw:(p_i + 1) * pw] = sp
            ssum = ssum + jnp.where(same_head, sp, 0.0)
            smax = jnp.maximum(smax, jnp.where(same_head, sp, NEG))
        tot = jnp.sum(ssum, axis=-1, keepdims=True) * (1.0 / MOBA_BLOCK)
        gate = jnp.where(gcol == float(n), tot, gate)
        bmax = jnp.where(gcol == float(n), jnp.max(smax, axis=-1, keepdims=True), bmax)
    sel = _top_mask(gate, gcol, min(MOBA_TOPK, n_blocks))
    between()

    s_own = _dot_nt(qa, kn_ref[0].astype(BF16))
    orow = lax.broadcasted_iota(jnp.int32, s_own.shape, 0)
    ocol = lax.broadcasted_iota(jnp.int32, s_own.shape, 1)
    own_ok = jnp.logical_and(ocol % N_HEADS == orow % N_HEADS, ocol // N_HEADS <= orow // N_HEADS)
    s_own = jnp.where(own_ok, s_own, NEG)

    m = jnp.maximum(jnp.max(s_own, axis=-1, keepdims=True),
                    jnp.max(jnp.where(sel > 0.5, bmax, NEG), axis=-1, keepdims=True))
    p_own = jnp.exp2(s_own - m)
    l = jnp.sum(p_own, axis=-1, keepdims=True)
    acc = _dot(p_own.astype(BF16), vn_ref[0].astype(BF16))
    lsum = jnp.zeros((rows, pw), F32)
    for p_i in range(n_pages):
        n = p_i // per_blk
        sp = s_ref[:, p_i * pw:(p_i + 1) * pw]
        keep = jnp.logical_and(same_head, sel[:, n:n + 1] > 0.5)
        pp = jnp.where(keep, jnp.exp2(sp - m), 0.0)
        lsum = lsum + pp
        acc = acc + _dot(pp.astype(BF16), v_pages[p_i][...].astype(BF16))
    l = l + jnp.sum(lsum, axis=-1, keepdims=True)
    o_ref[0] = acc * (1.0 / l)


def _moba_kernel(pt_ref, q_ref, k_ref, vt_ref, km_ref, qs_ref, kn_ref, vn_ref, ck_ref, cv_ref, o_ref, os_ref,
                 qsc_ref, bias_ref, m_ref, acc_ref, s_ref, mb_ref, sc_ref, kbuf_ref, vbuf_ref, page_sem,
                 *, nb, hp, grp, n_pages, t_new):
    i = pl.program_id(0)
    own = i % nb
    slot = i % 2

    def fetch(row, sl):
        for p in range(n_pages):
            pg = pt_ref[row, p]
            pltpu.make_async_copy(ck_ref.at[pg], kbuf_ref.at[sl, p], page_sem.at[sl]).start()
            pltpu.make_async_copy(cv_ref.at[pg], vbuf_ref.at[sl, p], page_sem.at[sl]).start()

    @pl.when(i == 0)
    def _():
        fetch(0, 0)

    @pl.when(i + 1 < pl.num_programs(0))
    def _():
        fetch(i + 1, 1 - slot)

    for src, buf in ((ck_ref, kbuf_ref), (cv_ref, vbuf_ref)):
        pltpu.make_async_copy(src.at[pl.ds(0, n_pages)], buf.at[slot], page_sem.at[slot]).wait()
    k_pages = [kbuf_ref.at[slot, p] for p in range(n_pages)]
    v_pages = [vbuf_ref.at[slot, p] for p in range(n_pages)]

    def prompt_prologue():
        _moba_prompt_prologue(own, q_ref, k_ref, vt_ref, km_ref, qsc_ref, bias_ref, m_ref, acc_ref,
                              nb=nb, hp=hp)

    _moba_sample_body(qs_ref, kn_ref, vn_ref, k_pages, v_pages, os_ref, sc_ref, t_new=t_new,
                      between=prompt_prologue)
    _moba_prompt_main(own, k_ref, vt_ref, o_ref, qsc_ref, bias_ref, m_ref, acc_ref, s_ref, mb_ref,
                      nb=nb, hp=hp, grp=grp)


def moba_attention(q, k_bf, vt, kmean, q_s, k_new, v_new, cache_k, cache_v, page_table):
    s = q.shape[0]
    nb = s // MOBA_BLOCK
    hp = ATT_HEADS
    grp = ATT_GROUP if nb % ATT_GROUP == 0 else 1
    wide = hp * HEAD_DIM
    db, t_new, _ = q_s.shape
    th = t_new * N_HEADS
    q_s, k_new, v_new = (a.reshape(db, th, HEAD_DIM) for a in (q_s, k_new, v_new))
    assert db == (N_HEADS // hp) * nb, "one sample row per prompt tile step"
    n_pages = page_table.shape[1]
    page = cache_k.shape[-3]
    ck = cache_k.reshape(-1, page * N_HEADS, HEAD_DIM)
    cv = cache_v.reshape(-1, page * N_HEADS, HEAD_DIM)
    tok = pl.BlockSpec((1, th, HEAD_DIM), lambda i, pt: (i, 0, 0))
    tile = pl.BlockSpec((MOBA_BLOCK, wide), lambda i, pt: (i % nb, i // nb))

    page_rows = page * N_HEADS
    grid_spec = pltpu.PrefetchScalarGridSpec(
        num_scalar_prefetch=1,
        grid=(db,),
        in_specs=[
            tile,
            pl.BlockSpec((s, wide), lambda i, pt: (0, i // nb)),
            pl.BlockSpec((nb, hp * VT_ROWS, MOBA_BLOCK), lambda i, pt: (0, i // nb, 0)),
            pl.BlockSpec((nb, wide), lambda i, pt: (0, i // nb)),
            tok, tok, tok,
            pl.BlockSpec(memory_space=pl.ANY), pl.BlockSpec(memory_space=pl.ANY),
        ],
        out_specs=(tile, tok),
        scratch_shapes=_prompt_scratch(nb, hp, grp) + [
            pltpu.VMEM((th, n_pages * page_rows), F32),
            pltpu.VMEM((2, n_pages, page_rows, HEAD_DIM), F32),
            pltpu.VMEM((2, n_pages, page_rows, HEAD_DIM), F32),
            pltpu.SemaphoreType.DMA((2,)),
        ],
    )
    o_p, o_s = pl.pallas_call(
        functools.partial(_moba_kernel, nb=nb, hp=hp, grp=grp, n_pages=n_pages, t_new=t_new),
        grid_spec=grid_spec,
        out_shape=(jax.ShapeDtypeStruct((s, ATTN_WIDTH), BF16),
                   jax.ShapeDtypeStruct((db, th, HEAD_DIM), F32)),
        compiler_params=_cparams(("arbitrary",), ATTN_VMEM_LIMIT),
        name="moba_attention",
    )(page_table, q, k_bf, vt, kmean, q_s, k_new, v_new, ck, cv)
    return o_p, o_s.reshape(db, t_new, ATTN_WIDTH)


def _pool_sample_kernel(u_ref, st_ref, d_ref, new_ref, *, t_new, start_pos):
    ext = [st_ref[:, r, :] for r in range(POOL_STATE)] + [u_ref[:, t, :] for t in range(t_new)]
    for t in range(t_new):
        e = POOL_STATE + t
        parts = []
        for g, w in enumerate(POOL_WINDOWS):
            c0, c1 = g * POOL_GROUP_W, (g + 1) * POOL_GROUP_W
            wsum = ext[e][:, c0:c1]
            for jj in range(1, w):
                wsum = wsum + ext[e - jj][:, c0:c1]
            count = min(float(w), float(start_pos + t) + 1.0)
            parts.append(wsum * (1.0 / count) - ext[e][:, c0:c1])
        d_ref[:, t, :] = jnp.concatenate(parts, axis=-1).astype(d_ref.dtype)
    for r in range(POOL_STATE):
        new_ref[:, r, :] = ext[t_new + r]


def pool_sample(u, state, *, start_pos, bb=32):
    db, t_new, _ = u.shape
    return pl.pallas_call(
        functools.partial(_pool_sample_kernel, t_new=t_new, start_pos=start_pos),
        grid=(db // bb,),
        in_specs=[
            pl.BlockSpec((bb, t_new, POOL_WIDTH), lambda i: (i, 0, 0)),
            pl.BlockSpec((bb, POOL_STATE, POOL_WIDTH), lambda i: (i, 0, 0)),
        ],
        out_specs=(
            pl.BlockSpec((bb, t_new, POOL_WIDTH), lambda i: (i, 0, 0)),
            pl.BlockSpec((bb, POOL_STATE, POOL_WIDTH), lambda i: (i, 0, 0)),
        ),
        out_shape=(
            jax.ShapeDtypeStruct((db, t_new, POOL_WIDTH), F32),
            jax.ShapeDtypeStruct((db, POOL_STATE, POOL_WIDTH), F32),
        ),
        compiler_params=_cparams(("arbitrary",)),
        name="pool_sample",
    )(u, state)


def _out_proj_kernel(x_ref, oa_ref, d_ref, wp_ref, ps_ref, wo_ref, gf_ref, wr_ref, br_ref,
                     h2_ref, rt_ref, mix_ref):
    mix_ref[:, 0:ATTN_WIDTH] = oa_ref[...].astype(BF16)
    dd = d_ref[...].astype(BF16)
    for g in range(len(POOL_WINDOWS)):
        c0, c1 = g * POOL_GROUP_W, (g + 1) * POOL_GROUP_W
        yg = _dot(dd[:, c0:c1], wp_ref[g].astype(BF16)) * ps_ref[:, c0:c1]
        mix_ref[:, ATTN_WIDTH + c0:ATTN_WIDTH + c1] = yg.astype(BF16)
    h2 = x_ref[...] + _dot(mix_ref[...], wo_ref[...])
    h2_ref[...] = h2
    ms = jnp.mean(h2 * h2, axis=-1, keepdims=True)
    hn = h2 * lax.rsqrt(ms + EPS) * gf_ref[...]

    hn_hi = hn.astype(BF16)
    hn_lo = (hn - hn_hi.astype(F32)).astype(BF16)
    t = _dot(hn_hi, wr_ref[...])
    logits = t[:, :LANES] + t[:, LANES:] + _dot(hn_lo, wr_ref[:, :LANES]) + br_ref[...]
    lane = lax.broadcasted_iota(jnp.int32, logits.shape, 1).astype(F32)
    far = jnp.float32(LANES)
    is_g = lane < N_EXPERT_GROUPS
    gl = jnp.where(is_g, logits, -jnp.inf)
    g_max = jnp.max(gl, axis=-1, keepdims=True)
    g_top = jnp.min(jnp.where(gl == g_max, lane, far), axis=-1, keepdims=True)
    g_p = 1.0 / jnp.sum(jnp.where(is_g, jnp.exp(gl - g_max), 0.0), axis=-1, keepdims=True)
    lo = N_EXPERT_GROUPS + g_top * EXPERTS_PER_GROUP
    in_grp = jnp.logical_and(lane >= lo, lane < lo + EXPERTS_PER_GROUP)
    el = jnp.where(in_grp, logits, -jnp.inf)
    e1 = jnp.max(el, axis=-1, keepdims=True)
    i1 = jnp.min(jnp.where(el == e1, lane, far), axis=-1, keepdims=True)
    el2 = jnp.where(lane == i1, -jnp.inf, el)
    e2 = jnp.max(el2, axis=-1, keepdims=True)
    i2 = jnp.min(jnp.where(el2 == e2, lane, far), axis=-1, keepdims=True)
    ex2 = jnp.exp(e2 - e1)
    den = 1.0 + ex2
    w1 = (1.0 / den) * g_p
    w2 = (ex2 / den) * g_p
    rt_ref[...] = jnp.where(lane == 0.0, i1 - N_EXPERT_GROUPS,
                            jnp.where(lane == 1.0, i2 - N_EXPERT_GROUPS,
                                      jnp.where(lane == 2.0, w1, jnp.where(lane == 3.0, w2, 0.0))))


def out_proj(x, o_attn, d, w_pool, pool_scale, w_out_bf, g_ffn, w_router, b_router, *, tm):
    n = x.shape[0]
    row = lambda i: (i, 0)
    const = lambda i: (0, 0)
    return pl.pallas_call(
        _out_proj_kernel,
        grid=(n // tm,),
        in_specs=[
            pl.BlockSpec((tm, D_MODEL), row),
            pl.BlockSpec((tm, ATTN_WIDTH), row),
            pl.BlockSpec((tm, POOL_WIDTH), row),
            pl.BlockSpec(w_pool.shape, lambda i: (0, 0, 0)),
            pl.BlockSpec((1, POOL_WIDTH), const),
            pl.BlockSpec((D_MODEL, D_MODEL), const),
            pl.BlockSpec((1, D_MODEL), const),
            pl.BlockSpec((D_MODEL, 2 * LANES), const),
            pl.BlockSpec((1, LANES), const),
        ],
        out_specs=(
            pl.BlockSpec((tm, D_MODEL), row),
            pl.BlockSpec((tm, LANES), row),
        ),
        out_shape=(
            jax.ShapeDtypeStruct((n, D_MODEL), F32),
            jax.ShapeDtypeStruct((n, LANES), F32),
        ),
        scratch_shapes=[pltpu.VMEM((tm, D_MODEL), BF16)],
        compiler_params=_cparams(("arbitrary",)),
        name="out_proj",
    )(x, o_attn, d, w_pool, pool_scale, w_out_bf, g_ffn, w_router, b_router)


MOE_TILE = 256
ROW_TILE = 256


def _route_kernel(rt_ref, pos_ref, meta_ref, *, n_tiles):
    lane = lax.broadcasted_iota(jnp.int32, (ROW_TILE, LANES), 1).astype(F32)
    r_i = lax.broadcasted_iota(jnp.int32, (ROW_TILE, ROW_TILE), 0)
    c_i = lax.broadcasted_iota(jnp.int32, (ROW_TILE, ROW_TILE), 1)
    tri = jnp.where(c_i < r_i, 1.0, 0.0).astype(BF16)

    def one_hot(t):
        rt = rt_ref[pl.ds(pl.multiple_of(t * ROW_TILE, ROW_TILE), ROW_TILE), :]
        e1, e2 = rt[:, 0:1], rt[:, 1:2]
        return e1, e2, jnp.where(jnp.logical_or(lane == e1, lane == e2), 1.0, 0.0)

    def count(t, cnt):
        return cnt + jnp.sum(one_hot(t)[2], axis=0, keepdims=True)

    cnt = lax.fori_loop(0, n_tiles, count, jnp.zeros((1, LANES), F32))
    tiles_per = jnp.floor((cnt + (MOE_TILE - 1)) * (1.0 / MOE_TILE))
    e_r = lax.broadcasted_iota(jnp.int32, (LANES, LANES), 0)
    e_c = lax.broadcasted_iota(jnp.int32, (LANES, LANES), 1)
    upper = jnp.where(e_r < e_c, 1.0, 0.0).astype(BF16)
    off_tiles = _dot(jnp.broadcast_to(tiles_per, (8, LANES)).astype(BF16), upper)[0:1]
    base = off_tiles * MOE_TILE

    def place(t, run):
        e1, e2, oh = one_hot(t)
        dest = base + run + _dot(tri, oh.astype(BF16))
        p1 = jnp.sum(jnp.where(lane == e1, dest, 0.0), axis=1, keepdims=True)
        p2 = jnp.sum(jnp.where(lane == e2, dest, 0.0), axis=1, keepdims=True)
        pos = jnp.where(lane == 0.0, p1, jnp.where(lane == 1.0, p2, 0.0))
        pos_ref[pl.ds(pl.multiple_of(t * ROW_TILE, ROW_TILE), ROW_TILE), :] = pos.astype(jnp.int32)
        return run + jnp.sum(oh, axis=0, keepdims=True)

    lax.fori_loop(0, n_tiles, place, jnp.zeros((1, LANES), F32))
    row = lax.broadcasted_iota(jnp.int32, (8, LANES), 0)
    meta_ref[...] = jnp.where(row == 0, tiles_per, jnp.where(row == 1, off_tiles, 0.0))


def route(rt):
    n = rt.shape[0]
    assert n % ROW_TILE == 0
    return pl.pallas_call(
        functools.partial(_route_kernel, n_tiles=n // ROW_TILE),
        out_shape=(
            jax.ShapeDtypeStruct((n, LANES), jnp.int32),
            jax.ShapeDtypeStruct((8, LANES), F32),
        ),
        compiler_params=pltpu.CompilerParams(vmem_limit_bytes=VMEM_LIMIT),
        name="route",
    )(rt)


ROW_UNROLL = 8


def _dispatch_kernel(p1_ref, p2_ref, zt_ref, xa_ref, xb_ref, xs_ref, zero_ref, sem, zsem, *, tm, tiles_a, n_zero):
    i = pl.program_id(0)

    @pl.when(i == 0)
    def _():
        zero_ref[...] = jnp.zeros(zero_ref.shape, zero_ref.dtype)

        def zcopy(z):
            row0 = pl.multiple_of(zt_ref[z] * MOE_TILE, MOE_TILE)
            return pltpu.make_async_copy(zero_ref, xs_ref.at[pl.ds(row0, MOE_TILE)], zsem)

        for z in range(n_zero):
            @pl.when(zt_ref[z] >= 0)
            def _():
                zcopy(z).start()
        for z in range(n_zero):
            @pl.when(zt_ref[z] >= 0)
            def _():
                zcopy(z).wait()

    def scatter(x_ref):
        t0 = i * tm

        def issue(r, c):
            src = x_ref.at[pl.ds(r, 1)]
            pltpu.make_async_copy(src, xs_ref.at[pl.ds(p1_ref[t0 + r], 1)], sem).start()
            pltpu.make_async_copy(src, xs_ref.at[pl.ds(p2_ref[t0 + r], 1)], sem).start()
            return c

        lax.fori_loop(0, tm, issue, 0, unroll=ROW_UNROLL)
        for _ in range(2):
            pltpu.make_async_copy(x_ref, xs_ref.at[pl.ds(0, tm)], sem).wait()

    @pl.when(i < tiles_a)
    def _():
        scatter(xa_ref)

    @pl.when(i >= tiles_a)
    def _():
        scatter(xb_ref)


def dispatch(p1, p2, zero_tiles, xa, xb, rows, *, tm):
    tiles_a, tiles_b = xa.shape[0] // tm, xb.shape[0] // tm
    grid_spec = pltpu.PrefetchScalarGridSpec(
        num_scalar_prefetch=3,
        grid=(tiles_a + tiles_b,),
        in_specs=[
            pl.BlockSpec((tm, D_MODEL), lambda i, a, b, z: (jnp.minimum(i, tiles_a - 1), 0)),
            pl.BlockSpec((tm, D_MODEL), lambda i, a, b, z: (jnp.maximum(i - tiles_a, 0), 0)),
        ],
        out_specs=pl.BlockSpec(memory_space=pl.ANY),
        scratch_shapes=[pltpu.VMEM((MOE_TILE, D_MODEL), F32), pltpu.SemaphoreType.DMA(()), pltpu.SemaphoreType.DMA(())],
    )
    return pl.pallas_call(
        functools.partial(_dispatch_kernel, tm=tm, tiles_a=tiles_a, n_zero=zero_tiles.shape[0]),
        grid_spec=grid_spec,
        out_shape=jax.ShapeDtypeStruct((rows, D_MODEL), F32),
        compiler_params=_cparams(("arbitrary",)),
        name="dispatch",
    )(p1, p2, zero_tiles, xa, xb)


def _experts_kernel(te_ref, ts_ref, tf_ref, sl_ref, nx_ref, nu_ref, x_ref, gf_ref, wg_hbm, wu_hbm, wd_hbm, o_ref,
                    wg_buf, wu_buf, wd_buf, wsem, *, layer):
    j = pl.program_id(0)
    used = j < nu_ref[0]
    first = jnp.logical_and(used, tf_ref[j] == 1)
    slot = sl_ref[j]

    def copies(e, sl):
        return (pltpu.make_async_copy(wg_hbm.at[layer, e], wg_buf.at[sl], wsem.at[sl]),
                pltpu.make_async_copy(wu_hbm.at[layer, e], wu_buf.at[sl], wsem.at[sl]),
                pltpu.make_async_copy(wd_hbm.at[layer, e], wd_buf.at[sl], wsem.at[sl]))

    @pl.when(j == 0)
    def _():
        for c in copies(te_ref[0], 0):
            c.start()

    @pl.when(jnp.logical_and(first, nx_ref[j] >= 0))
    def _():
        for c in copies(nx_ref[j], 1 - slot):
            c.start()

    @pl.when(first)
    def _():
        for c in copies(te_ref[j], slot):
            c.wait()

    @pl.when(used)
    def _():
        h = x_ref[...]
        ms = jnp.mean(h * h, axis=-1, keepdims=True)
        x = (h * lax.rsqrt(ms + EPS) * gf_ref[...]).astype(BF16).astype(F32)
        a = _dot(x, wg_buf[slot])
        b = _dot(x, wu_buf[slot])
        act = (a * (1.0 / (1.0 + jnp.exp(-a)))) * b
        o_ref[...] = _dot(act.astype(BF16).astype(F32), wd_buf[slot])

    @pl.when(jnp.logical_not(used))
    def _():
        o_ref[...] = jnp.zeros(o_ref.shape, o_ref.dtype)


def experts(tile_expert, tile_src, tile_first, tile_slot, tile_next, n_used, xs, g_ffn, w_gate, w_up, w_down, *, layer):
    rows, w = xs.shape
    idx = lambda f: (lambda j, te, ts, tf, sl, nx, nu: f(j, ts))
    grid_spec = pltpu.PrefetchScalarGridSpec(
        num_scalar_prefetch=6,
        grid=(rows // MOE_TILE,),
        in_specs=[
            pl.BlockSpec((MOE_TILE, w), idx(lambda j, ts: (ts[j], 0))),
            pl.BlockSpec((1, D_MODEL), idx(lambda j, ts: (0, 0))),
            pl.BlockSpec(memory_space=pl.ANY), pl.BlockSpec(memory_space=pl.ANY), pl.BlockSpec(memory_space=pl.ANY),
        ],
        out_specs=pl.BlockSpec((MOE_TILE, D_MODEL), idx(lambda j, ts: (j, 0))),
        scratch_shapes=[
            pltpu.VMEM((2, D_MODEL, D_EXPERT), F32), pltpu.VMEM((2, D_MODEL, D_EXPERT), F32),
            pltpu.VMEM((2, D_EXPERT, D_MODEL), F32), pltpu.SemaphoreType.DMA((2,)),
        ],
    )
    return pl.pallas_call(
        functools.partial(_experts_kernel, layer=layer),
        grid_spec=grid_spec,
        out_shape=jax.ShapeDtypeStruct((rows, D_MODEL), F32),
        compiler_params=_cparams(("arbitrary",), EXPERT_VMEM_LIMIT),
        name="experts",
    )(tile_expert, tile_src, tile_first, tile_slot, tile_next, n_used, xs, g_ffn, w_gate, w_up, w_down)


def _combine_kernel(p1_ref, p2_ref, h2_ref, rt_ref, os_ref, y_ref, buf_ref, sem, *, tm):
    i = pl.program_id(0)
    slot = i % 2

    def gather(step, sl):
        t0 = step * tm

        def issue(r, c):
            pltpu.make_async_copy(os_ref.at[pl.ds(p1_ref[t0 + r], 1)], buf_ref.at[sl, 0, pl.ds(r, 1)],
                                  sem.at[sl]).start()
            pltpu.make_async_copy(os_ref.at[pl.ds(p2_ref[t0 + r], 1)], buf_ref.at[sl, 1, pl.ds(r, 1)],
                                  sem.at[sl]).start()
            return c

        lax.fori_loop(0, tm, issue, 0, unroll=ROW_UNROLL)

    @pl.when(i == 0)
    def _():
        gather(0, 0)

    @pl.when(i + 1 < pl.num_programs(0))
    def _():
        gather(i + 1, 1 - slot)

    for s in range(2):
        pltpu.make_async_copy(os_ref.at[pl.ds(0, tm)], buf_ref.at[slot, s], sem.at[slot]).wait()
    rt = rt_ref[...]
    y_ref[...] = h2_ref[...] + rt[:, 2:3] * buf_ref[slot, 0] + rt[:, 3:4] * buf_ref[slot, 1]


def combine(p1, p2, h2, rt, os, *, tm):
    n = h2.shape[0]
    row = lambda i, a, b: (i, 0)
    grid_spec = pltpu.PrefetchScalarGridSpec(
        num_scalar_prefetch=2,
        grid=(n // tm,),
        in_specs=[pl.BlockSpec((tm, D_MODEL), row), pl.BlockSpec((tm, LANES), row),
                  pl.BlockSpec(memory_space=pl.ANY)],
        out_specs=pl.BlockSpec((tm, D_MODEL), row),
        scratch_shapes=[pltpu.VMEM((2, 2, tm, D_MODEL), F32), pltpu.SemaphoreType.DMA((2,))],
    )
    return pl.pallas_call(
        functools.partial(_combine_kernel, tm=tm),
        grid_spec=grid_spec,
        out_shape=jax.ShapeDtypeStruct((n, D_MODEL), F32),
        compiler_params=_cparams(("arbitrary",)),
        name="combine",
    )(p1, p2, h2, rt, os)


def kernel(x_prompt, x_sample, cache_k, cache_v, state_pool, page_table, g_mix, w_in, g_q, g_k, w_pool, pool_scale, w_out, g_ffn, w_group_router, b_group_router, w_expert_router, b_expert_router, w_gate, w_up, w_down):
    B, S, _ = x_prompt.shape
    DB, T, _ = x_sample.shape
    depth = w_in.shape[0]
    assert B == 1 and depth == 1
    past_len = page_table.shape[1] * cache_k.shape[2]
    l = 0

    w_in_bf = w_in[l].astype(BF16)
    w_out_bf = w_out[l].astype(BF16)
    gm, gq, gk, gf = g_mix[l][None], g_q[l][None], g_k[l][None], g_ffn[l][None]
    ps = pool_scale[l][None]
    n_r = N_EXPERT_GROUPS + N_EXPERTS
    w_router = jnp.concatenate([w_group_router[l], w_expert_router[l].reshape(D_MODEL, N_EXPERTS)], axis=1)
    w_router = jnp.pad(w_router, ((0, 0), (0, LANES - n_r)))
    w_router_hi = w_router.astype(BF16)
    w_router = jnp.concatenate([w_router_hi, (w_router - w_router_hi.astype(F32)).astype(BF16)], axis=1)
    b_router = jnp.concatenate([b_group_router[l], b_expert_router[l].reshape(N_EXPERTS)])
    b_router = jnp.pad(b_router, (0, LANES - n_r))[None]

    def mixer_tail(x2d, o_attn, d):
        return out_proj(x2d, o_attn, d, w_pool[l], ps, w_out_bf, gf, w_router, b_router, tm=PROJ_TILE)

    xp = x_prompt.reshape(S, D_MODEL)
    n_s = DB * T
    xs = x_sample.reshape(n_s, D_MODEL)
    q_p, k_p, kb_p, km_p, v_p, u_p, d_p, vt_p = in_proj(xp, gm, w_in_bf, gq, gk, tm=PROJ_TILE, with_pool=True)
    q_s, k_s, _, _, v_s, u_s = in_proj(xs, gm, w_in_bf, gq, gk, tm=n_s, with_pool=False)
    r3 = lambda a: a.reshape(DB, T, ATTN_WIDTH)
    o_p, o_s = moba_attention(q_p, kb_p, vt_p, km_p.reshape(S // MOBA_BLOCK, ATTN_WIDTH),
                              r3(q_s), r3(k_s), r3(v_s), cache_k, cache_v, page_table + l * cache_k.shape[1])
    d_s, pool_s = pool_sample(u_s.reshape(DB, T, POOL_WIDTH), state_pool[l], start_pos=past_len)
    h2_p, rt_p = mixer_tail(xp, o_p, d_p)
    h2_s, rt_s = mixer_tail(xs, o_s.reshape(n_s, ATTN_WIDTH), d_s.reshape(n_s, POOL_WIDTH))

    pos, meta = route(jnp.concatenate([rt_p, rt_s], axis=0))
    p1, p2 = pos[:, 0], pos[:, 1]
    max_tiles = -(-2 * (S + n_s) // MOE_TILE) + N_EXPERTS
    tiles_per = meta[0, :N_EXPERTS].astype(jnp.int32)
    ends = tiles_per + meta[1, :N_EXPERTS].astype(jnp.int32)
    n_used = ends[N_EXPERTS - 1]
    tile_src = jnp.minimum(jnp.arange(max_tiles, dtype=jnp.int32), n_used - 1)
    tile_expert = jnp.minimum(jnp.sum(tile_src[:, None] >= ends[None, :], axis=1), N_EXPERTS - 1).astype(jnp.int32)
    tail = n_used + jnp.arange(N_EXPERTS, dtype=jnp.int32)
    zero_tiles = jnp.concatenate([jnp.where(tiles_per > 0, ends - 1, -1), jnp.where(tail < max_tiles, tail, -1)])
    x_sorted = dispatch(p1, p2, zero_tiles, h2_p, h2_s, max_tiles * MOE_TILE, tm=ROW_TILE)
    tile_ids = jnp.arange(max_tiles, dtype=jnp.int32)
    tile_first = jnp.logical_and(tile_ids < n_used,
                                 jnp.logical_or(tile_ids == 0, tile_expert != jnp.roll(tile_expert, 1))).astype(jnp.int32)
    tile_slot = ((jnp.cumsum(tile_first) - 1) % 2).astype(jnp.int32)
    e_ids = jnp.arange(N_EXPERTS, dtype=jnp.int32)
    later = jnp.logical_and(e_ids[None, :] > e_ids[:, None], tiles_per[None, :] > 0)
    next_nonempty = jnp.min(jnp.where(later, e_ids[None, :], N_EXPERTS), axis=1)
    tile_next = jnp.where(next_nonempty < N_EXPERTS, next_nonempty, -1)[tile_expert].astype(jnp.int32)
    o_sorted = experts(tile_expert, tile_src, tile_first, tile_slot, tile_next, n_used[None], x_sorted, gf,
                       w_gate, w_up, w_down, layer=l)
    y_prompt = combine(p1[:S], p2[:S], h2_p, rt_p, o_sorted, tm=ROW_TILE).reshape(B, S, D_MODEL)
    y_sample = combine(p1[S:], p2[S:], h2_s, rt_s, o_sorted, tm=ROW_TILE).reshape(DB, T, D_MODEL)

    hd = (N_HEADS, HEAD_DIM)
    return (
        y_prompt,
        y_sample,
        k_p.reshape(1, B, S, *hd),
        v_p.reshape(1, B, S, *hd),
        u_p[S - POOL_STATE:].reshape(1, B, POOL_STATE, POOL_WIDTH),
        k_s.reshape(1, DB, T, *hd),
        v_s.reshape(1, DB, T, *hd),
        pool_s.reshape(1, DB, POOL_STATE, POOL_WIDTH),
    )
```

```python
import functools

import jax
import jax.numpy as jnp
from jax import lax
from jax.experimental import pallas as pl
from jax.experimental.pallas import tpu as pltpu

D_MODEL = 2048
ATTN_WIDTH = 1024
POOL_WIDTH = 1024
HEAD_DIM = 128
N_HEADS = 8
POOL_WINDOWS = (2, 4, 8, 16)
POOL_GROUP_W = 256
POOL_STATE = 15
MOBA_BLOCK = 256
MOBA_TOPK = 3
N_EXPERT_GROUPS = 4
EXPERTS_PER_GROUP = 4
N_EXPERTS = 16
D_EXPERT = 768
EPS = 1e-6
NEG = -1e30
LANES = 128
VMEM_LIMIT = 56 * 1024 * 1024
PROJ_TILE = 512
ATTN_VMEM_LIMIT = 60 * 1024 * 1024
EXPERT_VMEM_LIMIT = 60 * 1024 * 1024

BF16 = jnp.bfloat16
F32 = jnp.float32


def _cparams(sem, vmem_limit=VMEM_LIMIT):
    return pltpu.CompilerParams(dimension_semantics=sem, vmem_limit_bytes=vmem_limit)


def _dot(a, b):
    return jnp.dot(a, b, preferred_element_type=F32)


def _dot_nt(a, b, precision=None):
    return lax.dot_general(a, b, (((1,), (1,)), ((), ())), precision=precision,
                           preferred_element_type=F32)


def _top_mask(g, ids, k, axis=-1):
    sel = jnp.zeros(g.shape, F32)
    for _ in range(k):
        m = jnp.max(g, axis=axis, keepdims=True)
        idx = jnp.min(jnp.where(g == m, ids, jnp.float32(g.shape[axis])), axis=axis, keepdims=True)
        pick = ids == idx
        sel = jnp.where(pick, 1.0, sel)
        g = jnp.where(pick, -jnp.inf, g)
    return sel


def _in_proj_kernel(x_ref, gmix_ref, w_ref, gq_ref, gk_ref,
                    q_ref, k_ref, kb_ref, km_ref, v_ref, vb_ref, u_ref, *rest, tm, with_pool):
    if with_pool:
        d_ref, vt_ref, hb_ref, ext_ref = rest
    else:
        (hb_ref,) = rest
    xf = x_ref[...]
    ms = jnp.mean(xf * xf, axis=-1, keepdims=True)
    hb_ref[...] = (xf * lax.rsqrt(ms + EPS) * gmix_ref[...]).astype(BF16)

    def head_norm(z, out_ref, g_ref):
        for h in range(N_HEADS):
            zh = z[:, h * HEAD_DIM:(h + 1) * HEAD_DIM]
            ms = jnp.mean(zh * zh, axis=-1, keepdims=True)
            out_ref[:, h * HEAD_DIM:(h + 1) * HEAD_DIM] = zh * lax.rsqrt(ms + EPS) * g_ref[...]

    head_norm(_dot(hb_ref[...], w_ref[:, 0:ATTN_WIDTH]), q_ref, gq_ref)
    head_norm(_dot(hb_ref[...], w_ref[:, ATTN_WIDTH:2 * ATTN_WIDTH]), k_ref, gk_ref)
    kb_ref[...] = k_ref[...].astype(BF16)
    for g in range(tm // MOBA_BLOCK):
        blk = k_ref[g * MOBA_BLOCK:(g + 1) * MOBA_BLOCK, :]
        km_ref[g] = jnp.mean(blk, axis=0, keepdims=True)
    zv = _dot(hb_ref[...], w_ref[:, 2 * ATTN_WIDTH:3 * ATTN_WIDTH])
    v_ref[...] = zv
    vb_ref[...] = zv.astype(BF16)
    if with_pool:
        for g in range(tm // MOBA_BLOCK):
            for h in range(N_HEADS):
                blk = zv[g * MOBA_BLOCK:(g + 1) * MOBA_BLOCK, h * HEAD_DIM:(h + 1) * HEAD_DIM]
                vt_ref[g, h * VT_ROWS:h * VT_ROWS + HEAD_DIM, :] = blk.T.astype(BF16)
                vt_ref[g, h * VT_ROWS + HEAD_DIM:(h + 1) * VT_ROWS, :] = jnp.ones(
                    (VT_ROWS - HEAD_DIM, MOBA_BLOCK), BF16)
    zu = _dot(hb_ref[...], w_ref[:, 3 * ATTN_WIDTH:])
    u_ref[...] = zu
    if with_pool:
        _pool_differences(zu, d_ref, ext_ref, tm=tm)


POOL_PAD = 16


def _pool_differences(u, d_ref, ext_ref, *, tm):
    i = pl.program_id(0)

    @pl.when(i == 0)
    def _():
        ext_ref[0:POOL_PAD, :] = jnp.zeros((POOL_PAD, POOL_WIDTH), F32)

    ext_ref[POOL_PAD:POOL_PAD + tm, :] = u
    pos = (i * tm + lax.broadcasted_iota(jnp.int32, (tm, 1), 0)).astype(F32)
    for g, w in enumerate(POOL_WINDOWS):
        c0, c1 = g * POOL_GROUP_W, (g + 1) * POOL_GROUP_W
        wsum = u[:, c0:c1]
        for jj in range(1, w):
            wsum = wsum + ext_ref[POOL_PAD - jj:POOL_PAD - jj + tm, c0:c1]
        inv = 1.0 / jnp.minimum(jnp.float32(w), pos + 1.0)
        d_ref[:, c0:c1] = (wsum * inv - u[:, c0:c1]).astype(d_ref.dtype)
    ext_ref[0:POOL_PAD, :] = ext_ref[tm:tm + POOL_PAD, :]


def in_proj(x, g_mix, w_in_bf, g_q, g_k, *, tm, with_pool):
    n = x.shape[0]
    wide = lambda i: (i, 0)
    const = lambda i: (0, 0)
    out_shape = (
        jax.ShapeDtypeStruct((n, ATTN_WIDTH), F32),
        jax.ShapeDtypeStruct((n, ATTN_WIDTH), F32),
        jax.ShapeDtypeStruct((n, ATTN_WIDTH), BF16),
        jax.ShapeDtypeStruct((n // MOBA_BLOCK, 1, ATTN_WIDTH), F32),
        jax.ShapeDtypeStruct((n, ATTN_WIDTH), F32),
        jax.ShapeDtypeStruct((n, ATTN_WIDTH), BF16),
        jax.ShapeDtypeStruct((n, POOL_WIDTH), F32),
    )
    blk = pl.BlockSpec((tm, ATTN_WIDTH), wide)
    vt_shape = (n // MOBA_BLOCK, N_HEADS * VT_ROWS, MOBA_BLOCK)
    pool_out = (jax.ShapeDtypeStruct((n, POOL_WIDTH), BF16), jax.ShapeDtypeStruct(vt_shape, BF16)) if with_pool else ()
    pool_specs = (blk, pl.BlockSpec((tm // MOBA_BLOCK,) + vt_shape[1:], lambda i: (i, 0, 0))) if with_pool else ()
    pool_scratch = [pltpu.VMEM((POOL_PAD + tm, POOL_WIDTH), F32)] if with_pool else []
    return pl.pallas_call(
        functools.partial(_in_proj_kernel, tm=tm, with_pool=with_pool),
        grid=(n // tm,),
        in_specs=[
            pl.BlockSpec((tm, D_MODEL), wide),
            pl.BlockSpec((1, D_MODEL), const),
            pl.BlockSpec(w_in_bf.shape, const, pipeline_mode=pl.Buffered(1)),
            pl.BlockSpec((1, HEAD_DIM), const),
            pl.BlockSpec((1, HEAD_DIM), const),
        ],
        out_specs=(blk, blk, blk,
                   pl.BlockSpec((tm // MOBA_BLOCK, 1, ATTN_WIDTH), lambda i: (i, 0, 0)),
                   blk, blk, blk) + pool_specs,
        out_shape=out_shape + pool_out,
        scratch_shapes=[pltpu.VMEM((tm, D_MODEL), BF16)] + pool_scratch,
        compiler_params=_cparams(("arbitrary",)),
        name="in_proj",
    )(x, g_mix, w_in_bf, g_q, g_k)


LOG2E = 1.4426950408889634
ATT_HEADS = 2
ATT_GROUP = 4
VT_ROWS = HEAD_DIM + 16


def _moba_prompt_prologue(own, q_ref, k_ref, vt_ref, km_ref, qs_ref, bias_ref, m_ref, acc_ref, *, nb, hp):
    base = pl.multiple_of(own * MOBA_BLOCK, MOBA_BLOCK)
    blk_id = lax.broadcasted_iota(jnp.int32, (nb, MOBA_BLOCK), 0).astype(F32)
    valid = blk_id < own.astype(F32)
    key_i = lax.broadcasted_iota(jnp.int32, (MOBA_BLOCK, MOBA_BLOCK), 0)
    qry_i = lax.broadcasted_iota(jnp.int32, (MOBA_BLOCK, MOBA_BLOCK), 1)

    for hh in range(hp):
        cols = slice(hh * HEAD_DIM, (hh + 1) * HEAD_DIM)
        q = q_ref[:, cols]
        gate = _dot_nt(km_ref[:, cols], q, precision=lax.Precision.HIGHEST)
        sel = _top_mask(jnp.where(valid, gate, NEG), blk_id, MOBA_TOPK, axis=0)
        bias = jnp.where(jnp.logical_and(sel > 0.5, valid), 0.0, NEG)
        for n in range(nb):
            bias_ref[hh, n] = bias[n:n + 1, :]
        qs = (q * (HEAD_DIM ** -0.5 * LOG2E)).astype(BF16)
        qs_ref[hh] = qs
        s = _dot_nt(k_ref[pl.ds(base, MOBA_BLOCK), cols], qs)
        s = jnp.where(key_i <= qry_i, s, NEG)
        m0 = jnp.max(s, axis=0, keepdims=True)
        p = jnp.exp2(s - m0)
        m_ref[hh] = m0
        acc_ref[hh] = _dot(vt_ref[own, hh * VT_ROWS:(hh + 1) * VT_ROWS, :], p.astype(BF16))


def _moba_prompt_main(own, k_ref, vt_ref, o_ref, qs_ref, bias_ref, m_ref, acc_ref, s_ref, mb_ref,
                      *, nb, hp, grp):
    n_iter = (own + grp - 1) // grp

    def scores(i, slot):
        for hh in range(hp):
            cols = slice(hh * HEAD_DIM, (hh + 1) * HEAD_DIM)
            qs = qs_ref[hh]
            mb = None
            for gg in range(grp):
                n = jnp.minimum(i * grp + gg, nb - 1)
                off = pl.multiple_of(n * MOBA_BLOCK, MOBA_BLOCK)
                s = _dot_nt(k_ref[pl.ds(off, MOBA_BLOCK), cols], qs) + bias_ref[hh, n]
                s_ref[slot, hh, gg] = s
                smax = jnp.max(s, axis=0, keepdims=True)
                mb = smax if mb is None else jnp.maximum(mb, smax)
            mb_ref[slot, hh] = mb

    def softmax_pv(i, slot):
        for hh in range(hp):
            rows = slice(hh * VT_ROWS, (hh + 1) * VT_ROWS)
            m_prev = m_ref[hh]
            m_new = jnp.maximum(m_prev, mb_ref[slot, hh])
            acc = jnp.exp2(m_prev - m_new) * acc_ref[hh]
            for gg in range(grp):
                p = jnp.exp2(s_ref[slot, hh, gg] - m_new)
                acc = acc + _dot(vt_ref[i * grp + gg, rows, :], p.astype(BF16))
            m_ref[hh] = m_new
            acc_ref[hh] = acc

    scores(0, 0)

    def body(k, carry):
        scores(2 * k + 1, 1)
        softmax_pv(2 * k, 0)
        scores(2 * k + 2, 0)
        softmax_pv(2 * k + 1, 1)
        return carry

    lax.fori_loop(0, n_iter // 2, body, 0)

    @pl.when(n_iter % 2 == 1)
    def _():
        softmax_pv(n_iter - 1, 0)

    for hh in range(hp):
        acc = acc_ref[hh]
        o = acc[:HEAD_DIM] * (1.0 / acc[HEAD_DIM:HEAD_DIM + 1])
        o_ref[:, hh * HEAD_DIM:(hh + 1) * HEAD_DIM] = o.T.astype(o_ref.dtype)


def _prompt_scratch(nb, hp, grp):
    return [
        pltpu.VMEM((hp, MOBA_BLOCK, HEAD_DIM), BF16),
        pltpu.VMEM((hp, nb, 1, MOBA_BLOCK), F32),
        pltpu.VMEM((hp, 1, MOBA_BLOCK), F32),
        pltpu.VMEM((hp, VT_ROWS, MOBA_BLOCK), F32),
        pltpu.VMEM((2, hp, grp, MOBA_BLOCK, MOBA_BLOCK), F32),
        pltpu.VMEM((2, hp, 1, MOBA_BLOCK), F32),
    ]


def _moba_sample_body(q_ref, kn_ref, vn_ref, k_pages, v_pages, o_ref, s_ref, *, t_new, between):
    n_pages = len(k_pages)
    pw = k_pages[0].shape[0]
    page = pw // N_HEADS
    rows = t_new * N_HEADS
    per_blk = MOBA_BLOCK // page
    n_blocks = n_pages // per_blk

    qa = (q_ref[0] * (HEAD_DIM ** -0.5 * LOG2E)).astype(BF16)
    r_i = lax.broadcasted_iota(jnp.int32, (rows, pw), 0)
    c_i = lax.broadcasted_iota(jnp.int32, (rows, pw), 1)
    same_head = (c_i % N_HEADS) == (r_i % N_HEADS)

    gcol = lax.broadcasted_iota(jnp.int32, (rows, LANES), 1).astype(F32)
    gate = jnp.full((rows, LANES), NEG, F32)
    bmax = jnp.full((rows, LANES), NEG, F32)
    for n in range(n_blocks):
        ssum = jnp.zeros((rows, pw), F32)
        smax = jnp.full((rows, pw), NEG, F32)
        for pp in range(per_blk):
            p_i = n * per_blk + pp
            sp = _dot_nt(qa, k_pages[p_i][...].astype(BF16))
            s_ref[:, p_i * pw:(p_i + 1) * pw] = sp
            ssum = ssum + jnp.where(same_head, sp, 0.0)
            smax = jnp.maximum(smax, jnp.where(same_head, sp, NEG))
        tot = jnp.sum(ssum, axis=-1, keepdims=True) * (1.0 / MOBA_BLOCK)
        gate = jnp.where(gcol == float(n), tot, gate)
        bmax = jnp.where(gcol == float(n), jnp.max(smax, axis=-1, keepdims=True), bmax)
    sel = _top_mask(gate, gcol, min(MOBA_TOPK, n_blocks))
    between()

    s_own = _dot_nt(qa, kn_ref[0].astype(BF16))
    orow = lax.broadcasted_iota(jnp.int32, s_own.shape, 0)
    ocol = lax.broadcasted_iota(jnp.int32, s_own.shape, 1)
    own_ok = jnp.logical_and(ocol % N_HEADS == orow % N_HEADS, ocol // N_HEADS <= orow // N_HEADS)
    s_own = jnp.where(own_ok, s_own, NEG)

    m = jnp.maximum(jnp.max(s_own, axis=-1, keepdims=True),
                    jnp.max(jnp.where(sel > 0.5, bmax, NEG), axis=-1, keepdims=True))
    p_own = jnp.exp2(s_own - m)
    l = jnp.sum(p_own, axis=-1, keepdims=True)
    acc = _dot(p_own.astype(BF16), vn_ref[0].astype(BF16))
    lsum = jnp.zeros((rows, pw), F32)
    for p_i in range(n_pages):
        n = p_i // per_blk
        sp = s_ref[:, p_i * pw:(p_i + 1) * pw]
        keep = jnp.logical_and(same_head, sel[:, n:n + 1] > 0.5)
        pp = jnp.where(keep, jnp.exp2(sp - m), 0.0)
        lsum = lsum + pp
        acc = acc + _dot(pp.astype(BF16), v_pages[p_i][...].astype(BF16))
    l = l + jnp.sum(lsum, axis=-1, keepdims=True)
    o_ref[0] = acc * (1.0 / l)


def _moba_kernel(pt_ref, q_ref, k_ref, vt_ref, km_ref, qs_ref, kn_ref, vn_ref, ck_ref, cv_ref, o_ref, os_ref,
                 qsc_ref, bias_ref, m_ref, acc_ref, s_ref, mb_ref, sc_ref, kbuf_ref, vbuf_ref, page_sem,
                 *, nb, hp, grp, n_pages, t_new):
    i = pl.program_id(0)
    own = i % nb
    slot = i % 2

    def fetch(row, sl):
        for p in range(n_pages):
            pg = pt_ref[row, p]
            pltpu.make_async_copy(ck_ref.at[pg], kbuf_ref.at[sl, p], page_sem.at[sl]).start()
            pltpu.make_async_copy(cv_ref.at[pg], vbuf_ref.at[sl, p], page_sem.at[sl]).start()

    @pl.when(i == 0)
    def _():
        fetch(0, 0)

    @pl.when(i + 1 < pl.num_programs(0))
    def _():
        fetch(i + 1, 1 - slot)

    for src, buf in ((ck_ref, kbuf_ref), (cv_ref, vbuf_ref)):
        pltpu.make_async_copy(src.at[pl.ds(0, n_pages)], buf.at[slot], page_sem.at[slot]).wait()
    k_pages = [kbuf_ref.at[slot, p] for p in range(n_pages)]
    v_pages = [vbuf_ref.at[slot, p] for p in range(n_pages)]

    def prompt_prologue():
        _moba_prompt_prologue(own, q_ref, k_ref, vt_ref, km_ref, qsc_ref, bias_ref, m_ref, acc_ref,
                              nb=nb, hp=hp)

    _moba_sample_body(qs_ref, kn_ref, vn_ref, k_pages, v_pages, os_ref, sc_ref, t_new=t_new,
                      between=prompt_prologue)
    _moba_prompt_main(own, k_ref, vt_ref, o_ref, qsc_ref, bias_ref, m_ref, acc_ref, s_ref, mb_ref,
                      nb=nb, hp=hp, grp=grp)


def moba_attention(q, k_bf, vt, kmean, q_s, k_new, v_new, cache_k, cache_v, page_table):
    s = q.shape[0]
    nb = s // MOBA_BLOCK
    hp = ATT_HEADS
    grp = ATT_GROUP if nb % ATT_GROUP == 0 else 1
    wide = hp * HEAD_DIM
    db, t_new, _ = q_s.shape
    th = t_new * N_HEADS
    q_s, k_new, v_new = (a.reshape(db, th, HEAD_DIM) for a in (q_s, k_new, v_new))
    assert db == (N_HEADS // hp) * nb, "one sample row per prompt tile step"
    n_pages = page_table.shape[1]
    page = cache_k.shape[-3]
    ck = cache_k.reshape(-1, page * N_HEADS, HEAD_DIM)
    cv = cache_v.reshape(-1, page * N_HEADS, HEAD_DIM)
    tok = pl.BlockSpec((1, th, HEAD_DIM), lambda i, pt: (i, 0, 0))
    tile = pl.BlockSpec((MOBA_BLOCK, wide), lambda i, pt: (i % nb, i // nb))

    page_rows = page * N_HEADS
    grid_spec = pltpu.PrefetchScalarGridSpec(
        num_scalar_prefetch=1,
        grid=(db,),
        in_specs=[
            tile,
            pl.BlockSpec((s, wide), lambda i, pt: (0, i // nb)),
            pl.BlockSpec((nb, hp * VT_ROWS, MOBA_BLOCK), lambda i, pt: (0, i // nb, 0)),
            pl.BlockSpec((nb, wide), lambda i, pt: (0, i // nb)),
            tok, tok, tok,
            pl.BlockSpec(memory_space=pl.ANY), pl.BlockSpec(memory_space=pl.ANY),
        ],
        out_specs=(tile, tok),
        scratch_shapes=_prompt_scratch(nb, hp, grp) + [
            pltpu.VMEM((th, n_pages * page_rows), F32),
            pltpu.VMEM((2, n_pages, page_rows, HEAD_DIM), F32),
            pltpu.VMEM((2, n_pages, page_rows, HEAD_DIM), F32),
            pltpu.SemaphoreType.DMA((2,)),
        ],
    )
    o_p, o_s = pl.pallas_call(
        functools.partial(_moba_kernel, nb=nb, hp=hp, grp=grp, n_pages=n_pages, t_new=t_new),
        grid_spec=grid_spec,
        out_shape=(jax.ShapeDtypeStruct((s, ATTN_WIDTH), BF16),
                   jax.ShapeDtypeStruct((db, th, HEAD_DIM), F32)),
        compiler_params=_cparams(("arbitrary",), ATTN_VMEM_LIMIT),
        name="moba_attention",
    )(page_table, q, k_bf, vt, kmean, q_s, k_new, v_new, ck, cv)
    return o_p, o_s.reshape(db, t_new, ATTN_WIDTH)


def _pool_sample_kernel(u_ref, st_ref, d_ref, new_ref, *, t_new, start_pos):
    ext = [st_ref[:, r, :] for r in range(POOL_STATE)] + [u_ref[:, t, :] for t in range(t_new)]
    for t in range(t_new):
        e = POOL_STATE + t
        parts = []
        for g, w in enumerate(POOL_WINDOWS):
            c0, c1 = g * POOL_GROUP_W, (g + 1) * POOL_GROUP_W
            wsum = ext[e][:, c0:c1]
            for jj in range(1, w):
                wsum = wsum + ext[e - jj][:, c0:c1]
            count = min(float(w), float(start_pos + t) + 1.0)
            parts.append(wsum * (1.0 / count) - ext[e][:, c0:c1])
        d_ref[:, t, :] = jnp.concatenate(parts, axis=-1).astype(d_ref.dtype)
    for r in range(POOL_STATE):
        new_ref[:, r, :] = ext[t_new + r]


def pool_sample(u, state, *, start_pos, bb=32):
    db, t_new, _ = u.shape
    return pl.pallas_call(
        functools.partial(_pool_sample_kernel, t_new=t_new, start_pos=start_pos),
        grid=(db // bb,),
        in_specs=[
            pl.BlockSpec((bb, t_new, POOL_WIDTH), lambda i: (i, 0, 0)),
            pl.BlockSpec((bb, POOL_STATE, POOL_WIDTH), lambda i: (i, 0, 0)),
        ],
        out_specs=(
            pl.BlockSpec((bb, t_new, POOL_WIDTH), lambda i: (i, 0, 0)),
            pl.BlockSpec((bb, POOL_STATE, POOL_WIDTH), lambda i: (i, 0, 0)),
        ),
        out_shape=(
            jax.ShapeDtypeStruct((db, t_new, POOL_WIDTH), F32),
            jax.ShapeDtypeStruct((db, POOL_STATE, POOL_WIDTH), F32),
        ),
        compiler_params=_cparams(("arbitrary",)),
        name="pool_sample",
    )(u, state)


def _out_proj_kernel(x_ref, oa_ref, d_ref, wp_ref, ps_ref, wo_ref, gf_ref, wr_ref, br_ref,
                     h2_ref, rt_ref, mix_ref):
    mix_ref[:, 0:ATTN_WIDTH] = oa_ref[...].astype(BF16)
    dd = d_ref[...].astype(BF16)
    for g in range(len(POOL_WINDOWS)):
        c0, c1 = g * POOL_GROUP_W, (g + 1) * POOL_GROUP_W
        yg = _dot(dd[:, c0:c1], wp_ref[g].astype(BF16)) * ps_ref[:, c0:c1]
        mix_ref[:, ATTN_WIDTH + c0:ATTN_WIDTH + c1] = yg.astype(BF16)
    h2 = x_ref[...] + _dot(mix_ref[...], wo_ref[...])
    h2_ref[...] = h2
    ms = jnp.mean(h2 * h2, axis=-1, keepdims=True)
    hn = h2 * lax.rsqrt(ms + EPS) * gf_ref[...]

    hn_hi = hn.astype(BF16)
    hn_lo = (hn - hn_hi.astype(F32)).astype(BF16)
    t = _dot(hn_hi, wr_ref[...])
    logits = t[:, :LANES] + t[:, LANES:] + _dot(hn_lo, wr_ref[:, :LANES]) + br_ref[...]
    lane = lax.broadcasted_iota(jnp.int32, logits.shape, 1).astype(F32)
    far = jnp.float32(LANES)
    is_g = lane < N_EXPERT_GROUPS
    gl = jnp.where(is_g, logits, -jnp.inf)
    g_max = jnp.max(gl, axis=-1, keepdims=True)
    g_top = jnp.min(jnp.where(gl == g_max, lane, far), axis=-1, keepdims=True)
    g_p = 1.0 / jnp.sum(jnp.where(is_g, jnp.exp(gl - g_max), 0.0), axis=-1, keepdims=True)
    lo = N_EXPERT_GROUPS + g_top * EXPERTS_PER_GROUP
    in_grp = jnp.logical_and(lane >= lo, lane < lo + EXPERTS_PER_GROUP)
    el = jnp.where(in_grp, logits, -jnp.inf)
    e1 = jnp.max(el, axis=-1, keepdims=True)
    i1 = jnp.min(jnp.where(el == e1, lane, far), axis=-1, keepdims=True)
    el2 = jnp.where(lane == i1, -jnp.inf, el)
    e2 = jnp.max(el2, axis=-1, keepdims=True)
    i2 = jnp.min(jnp.where(el2 == e2, lane, far), axis=-1, keepdims=True)
    ex2 = jnp.exp(e2 - e1)
    den = 1.0 + ex2
    w1 = (1.0 / den) * g_p
    w2 = (ex2 / den) * g_p
    rt_ref[...] = jnp.where(lane == 0.0, i1 - N_EXPERT_GROUPS,
                            jnp.where(lane == 1.0, i2 - N_EXPERT_GROUPS,
                                      jnp.where(lane == 2.0, w1, jnp.where(lane == 3.0, w2, 0.0))))


def out_proj(x, o_attn, d, w_pool, pool_scale, w_out_bf, g_ffn, w_router, b_router, *, tm):
    n = x.shape[0]
    row = lambda i: (i, 0)
    const = lambda i: (0, 0)
    return pl.pallas_call(
        _out_proj_kernel,
        grid=(n // tm,),
        in_specs=[
            pl.BlockSpec((tm, D_MODEL), row),
            pl.BlockSpec((tm, ATTN_WIDTH), row),
            pl.BlockSpec((tm, POOL_WIDTH), row),
            pl.BlockSpec(w_pool.shape, lambda i: (0, 0, 0)),
            pl.BlockSpec((1, POOL_WIDTH), const),
            pl.BlockSpec((D_MODEL, D_MODEL), const),
            pl.BlockSpec((1, D_MODEL), const),
            pl.BlockSpec((D_MODEL, 2 * LANES), const),
            pl.BlockSpec((1, LANES), const),
        ],
        out_specs=(
            pl.BlockSpec((tm, D_MODEL), row),
            pl.BlockSpec((tm, LANES), row),
        ),
        out_shape=(
            jax.ShapeDtypeStruct((n, D_MODEL), F32),
            jax.ShapeDtypeStruct((n, LANES), F32),
        ),
        scratch_shapes=[pltpu.VMEM((tm, D_MODEL), BF16)],
        compiler_params=_cparams(("arbitrary",)),
        name="out_proj",
    )(x, o_attn, d, w_pool, pool_scale, w_out_bf, g_ffn, w_router, b_router)


MOE_TILE = 256
ROW_TILE = 256


def _route_kernel(rt_ref, pos_ref, meta_ref, *, n_tiles):
    lane = lax.broadcasted_iota(jnp.int32, (ROW_TILE, LANES), 1).astype(F32)
    r_i = lax.broadcasted_iota(jnp.int32, (ROW_TILE, ROW_TILE), 0)
    c_i = lax.broadcasted_iota(jnp.int32, (ROW_TILE, ROW_TILE), 1)
    tri = jnp.where(c_i < r_i, 1.0, 0.0).astype(BF16)

    def one_hot(t):
        rt = rt_ref[pl.ds(pl.multiple_of(t * ROW_TILE, ROW_TILE), ROW_TILE), :]
        e1, e2 = rt[:, 0:1], rt[:, 1:2]
        return e1, e2, jnp.where(jnp.logical_or(lane == e1, lane == e2), 1.0, 0.0)

    def count(t, cnt):
        return cnt + jnp.sum(one_hot(t)[2], axis=0, keepdims=True)

    cnt = lax.fori_loop(0, n_tiles, count, jnp.zeros((1, LANES), F32))
    tiles_per = jnp.floor((cnt + (MOE_TILE - 1)) * (1.0 / MOE_TILE))
    e_r = lax.broadcasted_iota(jnp.int32, (LANES, LANES), 0)
    e_c = lax.broadcasted_iota(jnp.int32, (LANES, LANES), 1)
    upper = jnp.where(e_r < e_c, 1.0, 0.0).astype(BF16)
    off_tiles = _dot(jnp.broadcast_to(tiles_per, (8, LANES)).astype(BF16), upper)[0:1]
    base = off_tiles * MOE_TILE

    def place(t, run):
        e1, e2, oh = one_hot(t)
        dest = base + run + _dot(tri, oh.astype(BF16))
        p1 = jnp.sum(jnp.where(lane == e1, dest, 0.0), axis=1, keepdims=True)
        p2 = jnp.sum(jnp.where(lane == e2, dest, 0.0), axis=1, keepdims=True)
        pos = jnp.where(lane == 0.0, p1, jnp.where(lane == 1.0, p2, 0.0))
        pos_ref[pl.ds(pl.multiple_of(t * ROW_TILE, ROW_TILE), ROW_TILE), :] = pos.astype(jnp.int32)
        return run + jnp.sum(oh, axis=0, keepdims=True)

    lax.fori_loop(0, n_tiles, place, jnp.zeros((1, LANES), F32))
    row = lax.broadcasted_iota(jnp.int32, (8, LANES), 0)
    meta_ref[...] = jnp.where(row == 0, tiles_per, jnp.where(row == 1, off_tiles, 0.0))


def route(rt):
    n = rt.shape[0]
    assert n % ROW_TILE == 0
    return pl.pallas_call(
        functools.partial(_route_kernel, n_tiles=n // ROW_TILE),
        out_shape=(
            jax.ShapeDtypeStruct((n, LANES), jnp.int32),
            jax.ShapeDtypeStruct((8, LANES), F32),
        ),
        compiler_params=pltpu.CompilerParams(vmem_limit_bytes=VMEM_LIMIT),
        name="route",
    )(rt)


ROW_UNROLL = 8


def _dispatch_kernel(p1_ref, p2_ref, zt_ref, xa_ref, xb_ref, xs_ref, zero_ref, sem, zsem, *, tm, tiles_a, n_zero):
    i = pl.program_id(0)

    @pl.when(i == 0)
    def _():
        zero_ref[...] = jnp.zeros(zero_ref.shape, zero_ref.dtype)

        def zcopy(z):
            row0 = pl.multiple_of(zt_ref[z] * MOE_TILE, MOE_TILE)
            return pltpu.make_async_copy(zero_ref, xs_ref.at[pl.ds(row0, MOE_TILE)], zsem)

        for z in range(n_zero):
            @pl.when(zt_ref[z] >= 0)
            def _():
                zcopy(z).start()
        for z in range(n_zero):
            @pl.when(zt_ref[z] >= 0)
            def _():
                zcopy(z).wait()

    def scatter(x_ref):
        t0 = i * tm

        def issue(r, c):
            src = x_ref.at[pl.ds(r, 1)]
            pltpu.make_async_copy(src, xs_ref.at[pl.ds(p1_ref[t0 + r], 1)], sem).start(priority=0)
            pltpu.make_async_copy(src, xs_ref.at[pl.ds(p2_ref[t0 + r], 1)], sem).start(priority=1)
            return c

        lax.fori_loop(0, tm, issue, 0, unroll=ROW_UNROLL)
        for _ in range(2):
            pltpu.make_async_copy(x_ref, xs_ref.at[pl.ds(0, tm)], sem).wait()

    @pl.when(i < tiles_a)
    def _():
        scatter(xa_ref)

    @pl.when(i >= tiles_a)
    def _():
        scatter(xb_ref)


def dispatch(p1, p2, zero_tiles, xa, xb, rows, *, tm):
    tiles_a, tiles_b = xa.shape[0] // tm, xb.shape[0] // tm
    grid_spec = pltpu.PrefetchScalarGridSpec(
        num_scalar_prefetch=3,
        grid=(tiles_a + tiles_b,),
        in_specs=[
            pl.BlockSpec((tm, D_MODEL), lambda i, a, b, z: (jnp.minimum(i, tiles_a - 1), 0)),
            pl.BlockSpec((tm, D_MODEL), lambda i, a, b, z: (jnp.maximum(i - tiles_a, 0), 0)),
        ],
        out_specs=pl.BlockSpec(memory_space=pl.ANY),
        scratch_shapes=[pltpu.VMEM((MOE_TILE, D_MODEL), F32), pltpu.SemaphoreType.DMA(()), pltpu.SemaphoreType.DMA(())],
    )
    return pl.pallas_call(
        functools.partial(_dispatch_kernel, tm=tm, tiles_a=tiles_a, n_zero=zero_tiles.shape[0]),
        grid_spec=grid_spec,
        out_shape=jax.ShapeDtypeStruct((rows, D_MODEL), F32),
        compiler_params=_cparams(("arbitrary",)),
        name="dispatch",
    )(p1, p2, zero_tiles, xa, xb)


def _experts_kernel(te_ref, ts_ref, tf_ref, sl_ref, nx_ref, nu_ref, x_ref, gf_ref, wg_hbm, wu_hbm, wd_hbm, o_ref,
                    wg_buf, wu_buf, wd_buf, wsem, *, layer):
    j = pl.program_id(0)
    used = j < nu_ref[0]
    first = jnp.logical_and(used, tf_ref[j] == 1)
    slot = sl_ref[j]

    def copies(e, sl):
        return (pltpu.make_async_copy(wg_hbm.at[layer, e], wg_buf.at[sl], wsem.at[sl]),
                pltpu.make_async_copy(wu_hbm.at[layer, e], wu_buf.at[sl], wsem.at[sl]),
                pltpu.make_async_copy(wd_hbm.at[layer, e], wd_buf.at[sl], wsem.at[sl]))

    @pl.when(j == 0)
    def _():
        for c in copies(te_ref[0], 0):
            c.start()

    @pl.when(jnp.logical_and(first, nx_ref[j] >= 0))
    def _():
        for c in copies(nx_ref[j], 1 - slot):
            c.start()

    @pl.when(first)
    def _():
        for c in copies(te_ref[j], slot):
            c.wait()

    @pl.when(used)
    def _():
        h = x_ref[...]
        ms = jnp.mean(h * h, axis=-1, keepdims=True)
        x = (h * lax.rsqrt(ms + EPS) * gf_ref[...]).astype(BF16).astype(F32)
        a = _dot(x, wg_buf[slot])
        b = _dot(x, wu_buf[slot])
        act = (a * (1.0 / (1.0 + jnp.exp(-a)))) * b
        o_ref[...] = _dot(act.astype(BF16).astype(F32), wd_buf[slot])

    @pl.when(jnp.logical_not(used))
    def _():
        o_ref[...] = jnp.zeros(o_ref.shape, o_ref.dtype)


def experts(tile_expert, tile_src, tile_first, tile_slot, tile_next, n_used, xs, g_ffn, w_gate, w_up, w_down, *, layer):
    rows, w = xs.shape
    idx = lambda f: (lambda j, te, ts, tf, sl, nx, nu: f(j, ts))
    grid_spec = pltpu.PrefetchScalarGridSpec(
        num_scalar_prefetch=6,
        grid=(rows // MOE_TILE,),
        in_specs=[
            pl.BlockSpec((MOE_TILE, w), idx(lambda j, ts: (ts[j], 0))),
            pl.BlockSpec((1, D_MODEL), idx(lambda j, ts: (0, 0))),
            pl.BlockSpec(memory_space=pl.ANY), pl.BlockSpec(memory_space=pl.ANY), pl.BlockSpec(memory_space=pl.ANY),
        ],
        out_specs=pl.BlockSpec((MOE_TILE, D_MODEL), idx(lambda j, ts: (j, 0))),
        scratch_shapes=[
            pltpu.VMEM((2, D_MODEL, D_EXPERT), F32), pltpu.VMEM((2, D_MODEL, D_EXPERT), F32),
            pltpu.VMEM((2, D_EXPERT, D_MODEL), F32), pltpu.SemaphoreType.DMA((2,)),
        ],
    )
    return pl.pallas_call(
        functools.partial(_experts_kernel, layer=layer),
        grid_spec=grid_spec,
        out_shape=jax.ShapeDtypeStruct((rows, D_MODEL), F32),
        compiler_params=_cparams(("arbitrary",), EXPERT_VMEM_LIMIT),
        name="experts",
    )(tile_expert, tile_src, tile_first, tile_slot, tile_next, n_used, xs, g_ffn, w_gate, w_up, w_down)


def _combine_kernel(p1_ref, p2_ref, h2_ref, rt_ref, os_ref, y_ref, buf_ref, sem, *, tm):
    i = pl.program_id(0)
    slot = i % 2

    def gather(step, sl):
        t0 = step * tm

        def issue(r, c):
            pltpu.make_async_copy(os_ref.at[pl.ds(p1_ref[t0 + r], 1)], buf_ref.at[sl, 0, pl.ds(r, 1)],
                                  sem.at[sl]).start(priority=0)
            pltpu.make_async_copy(os_ref.at[pl.ds(p2_ref[t0 + r], 1)], buf_ref.at[sl, 1, pl.ds(r, 1)],
                                  sem.at[sl]).start(priority=1)
            return c

        lax.fori_loop(0, tm, issue, 0, unroll=ROW_UNROLL)

    @pl.when(i == 0)
    def _():
        gather(0, 0)

    @pl.when(i + 1 < pl.num_programs(0))
    def _():
        gather(i + 1, 1 - slot)

    for s in range(2):
        pltpu.make_async_copy(os_ref.at[pl.ds(0, tm)], buf_ref.at[slot, s], sem.at[slot]).wait()
    rt = rt_ref[...]
    y_ref[...] = h2_ref[...] + rt[:, 2:3] * buf_ref[slot, 0] + rt[:, 3:4] * buf_ref[slot, 1]


def combine(p1, p2, h2, rt, os, *, tm):
    n = h2.shape[0]
    row = lambda i, a, b: (i, 0)
    grid_spec = pltpu.PrefetchScalarGridSpec(
        num_scalar_prefetch=2,
        grid=(n // tm,),
        in_specs=[pl.BlockSpec((tm, D_MODEL), row), pl.BlockSpec((tm, LANES), row),
                  pl.BlockSpec(memory_space=pl.ANY)],
        out_specs=pl.BlockSpec((tm, D_MODEL), row),
        scratch_shapes=[pltpu.VMEM((2, 2, tm, D_MODEL), F32), pltpu.SemaphoreType.DMA((2,))],
    )
    return pl.pallas_call(
        functools.partial(_combine_kernel, tm=tm),
        grid_spec=grid_spec,
        out_shape=jax.ShapeDtypeStruct((n, D_MODEL), F32),
        compiler_params=_cparams(("arbitrary",)),
        name="combine",
    )(p1, p2, h2, rt, os)


def kernel(x_prompt, x_sample, cache_k, cache_v, state_pool, page_table, g_mix, w_in, g_q, g_k, w_pool, pool_scale, w_out, g_ffn, w_group_router, b_group_router, w_expert_router, b_expert_router, w_gate, w_up, w_down):
    B, S, _ = x_prompt.shape
    DB, T, _ = x_sample.shape
    depth = w_in.shape[0]
    assert B == 1 and depth == 1
    past_len = page_table.shape[1] * cache_k.shape[2]
    l = 0

    w_in_bf = w_in[l].astype(BF16)
    w_out_bf = w_out[l].astype(BF16)
    gm, gq, gk, gf = g_mix[l][None], g_q[l][None], g_k[l][None], g_ffn[l][None]
    ps = pool_scale[l][None]
    n_r = N_EXPERT_GROUPS + N_EXPERTS
    w_router = jnp.concatenate([w_group_router[l], w_expert_router[l].reshape(D_MODEL, N_EXPERTS)], axis=1)
    w_router = jnp.pad(w_router, ((0, 0), (0, LANES - n_r)))
    w_router_hi = w_router.astype(BF16)
    w_router = jnp.concatenate([w_router_hi, (w_router - w_router_hi.astype(F32)).astype(BF16)], axis=1)
    b_router = jnp.concatenate([b_group_router[l], b_expert_router[l].reshape(N_EXPERTS)])
    b_router = jnp.pad(b_router, (0, LANES - n_r))[None]

    def mixer_tail(x2d, o_attn, d):
        return out_proj(x2d, o_attn, d, w_pool[l], ps, w_out_bf, gf, w_router, b_router, tm=PROJ_TILE)

    xp = x_prompt.reshape(S, D_MODEL)
    n_s = DB * T
    xs = x_sample.reshape(n_s, D_MODEL)
    q_p, k_p, kb_p, km_p, v_p, _, u_p, d_p, vt_p = in_proj(xp, gm, w_in_bf, gq, gk, tm=PROJ_TILE, with_pool=True)
    q_s, k_s, _, _, v_s, _, u_s = in_proj(xs, gm, w_in_bf, gq, gk, tm=n_s, with_pool=False)
    r3 = lambda a: a.reshape(DB, T, ATTN_WIDTH)
    o_p, o_s = moba_attention(q_p, kb_p, vt_p, km_p.reshape(S // MOBA_BLOCK, ATTN_WIDTH),
                              r3(q_s), r3(k_s), r3(v_s), cache_k, cache_v, page_table + l * cache_k.shape[1])
    d_s, pool_s = pool_sample(u_s.reshape(DB, T, POOL_WIDTH), state_pool[l], start_pos=past_len)
    h2_p, rt_p = mixer_tail(xp, o_p, d_p)
    h2_s, rt_s = mixer_tail(xs, o_s.reshape(n_s, ATTN_WIDTH), d_s.reshape(n_s, POOL_WIDTH))

    pos, meta = route(jnp.concatenate([rt_p, rt_s], axis=0))
    p1, p2 = pos[:, 0], pos[:, 1]
    max_tiles = -(-2 * (S + n_s) // MOE_TILE) + N_EXPERTS
    tiles_per = meta[0, :N_EXPERTS].astype(jnp.int32)
    ends = tiles_per + meta[1, :N_EXPERTS].astype(jnp.int32)
    n_used = ends[N_EXPERTS - 1]
    tile_src = jnp.minimum(jnp.arange(max_tiles, dtype=jnp.int32), n_used - 1)
    tile_expert = jnp.minimum(jnp.sum(tile_src[:, None] >= ends[None, :], axis=1), N_EXPERTS - 1).astype(jnp.int32)
    tail = n_used + jnp.arange(N_EXPERTS, dtype=jnp.int32)
    zero_tiles = jnp.concatenate([jnp.where(tiles_per > 0, ends - 1, -1), jnp.where(tail < max_tiles, tail, -1)])
    x_sorted = dispatch(p1, p2, zero_tiles, h2_p, h2_s, max_tiles * MOE_TILE, tm=ROW_TILE)
    tile_ids = jnp.arange(max_tiles, dtype=jnp.int32)
    tile_first = jnp.logical_and(tile_ids < n_used,
                                 jnp.logical_or(tile_ids == 0, tile_expert != jnp.roll(tile_expert, 1))).astype(jnp.int32)
    tile_slot = ((jnp.cumsum(tile_first) - 1) % 2).astype(jnp.int32)
    e_ids = jnp.arange(N_EXPERTS, dtype=jnp.int32)
    later = jnp.logical_and(e_ids[None, :] > e_ids[:, None], tiles_per[None, :] > 0)
    next_nonempty = jnp.min(jnp.where(later, e_ids[None, :], N_EXPERTS), axis=1)
    tile_next = jnp.where(next_nonempty < N_EXPERTS, next_nonempty, -1)[tile_expert].astype(jnp.int32)
    o_sorted = experts(tile_expert, tile_src, tile_first, tile_slot, tile_next, n_used[None], x_sorted, gf,
                       w_gate, w_up, w_down, layer=l)
    y_prompt = combine(p1[:S], p2[:S], h2_p, rt_p, o_sorted, tm=ROW_TILE).reshape(B, S, D_MODEL)
    y_sample = combine(p1[S:], p2[S:], h2_s, rt_s, o_sorted, tm=ROW_TILE).reshape(DB, T, D_MODEL)

    hd = (N_HEADS, HEAD_DIM)
    return (
        y_prompt,
        y_sample,
        k_p.reshape(1, B, S, *hd),
        v_p.reshape(1, B, S, *hd),
        u_p[S - POOL_STATE:].reshape(1, B, POOL_STATE, POOL_WIDTH),
        k_s.reshape(1, DB, T, *hd),
        v_s.reshape(1, DB, T, *hd),
        pool_s.reshape(1, DB, POOL_STATE, POOL_WIDTH),
    )
```
